```python
import jax, jax.numpy as jnp
from jax import lax
import numpy as np

D_MODEL = 1024
BATCH = 8
SEQ = 4096
DEPTH = 1

N_META = 16
ATT_HEADS = 8
ATT_KV_HEADS = 2
ATT_HEAD_DIM = 64
ATT_REP = ATT_HEADS // ATT_KV_HEADS
WINDOW = 128
BLOCK = 128
GLA_HEADS = 4
GLA_DK = 64
GLA_DV = 128
GLA_RANK = 16
GLA_TAU = 16.0
GLA_CHUNK = 64
N_DIR = 2
ATT_WIDTH = ATT_HEADS * ATT_HEAD_DIM
ATT_KV_WIDTH = ATT_KV_HEADS * ATT_HEAD_DIM
GLA_QK_WIDTH = GLA_HEADS * GLA_DK
GLA_WIDTH = GLA_HEADS * GLA_DV
MIX_WIDTH = ATT_WIDTH + GLA_WIDTH
PROJ_WIDTHS = (ATT_WIDTH, ATT_KV_WIDTH, ATT_KV_WIDTH, GLA_QK_WIDTH, GLA_QK_WIDTH, GLA_WIDTH, GLA_WIDTH, N_DIR * GLA_RANK)
IN_WIDTH = sum(PROJ_WIDTHS)
SPLIT_POINTS = tuple(int(v) for v in np.cumsum(PROJ_WIDTHS)[:-1])
D_FF = 2816
LN_EPS = 1e-5
DN_ALPHA = (2.0 * DEPTH) ** 0.25
DN_BETA = (8.0 * DEPTH) ** -0.25

kernel_name = "hymba_swa_gla_macaron_deepnorm_encoder"


def layer_norm(x, g, b):
    xf = x.astype(jnp.float32)
    mu = jnp.mean(xf, axis=-1, keepdims=True)
    xc = xf - mu
    var = jnp.mean(xc * xc, axis=-1, keepdims=True)
    return (xc * lax.rsqrt(var + LN_EPS) * g + b).astype(x.dtype)


def swiglu(h, w_gate, w_up, w_down):
    return (jax.nn.silu(h @ w_gate) * (h @ w_up)) @ w_down


def alibi_slopes():
    return np.array([2.0 ** (-8.0 * (i + 1) / ATT_HEADS) for i in range(ATT_HEADS)], np.float32)


def window_attention(q, k, v, sink):
    b, t = q.shape[:2]
    s = t - N_META
    nb = s // BLOCK
    G, R, Dh = ATT_KV_HEADS, ATT_REP, ATT_HEAD_DIM
    slopes = jnp.asarray(alibi_slopes()).reshape(G, R)
    sink = sink.reshape(G, R).astype(jnp.float32)
    q = q.reshape(b, t, G, R, Dh) * (Dh ** -0.5)
    qm, qr = q[:, :N_META], q[:, N_META:]
    km, kr = k[:, :N_META], k[:, N_META:]
    vm, vr = v[:, :N_META], v[:, N_META:]

    def band(z):
        zp = jnp.pad(z, ((0, 0), (BLOCK, BLOCK), (0, 0), (0, 0))).reshape(b, nb + 2, BLOCK, G, Dh)
        zb = jnp.concatenate([zp[:, :-2], zp[:, 1:-1], zp[:, 2:]], axis=2)
        return jnp.moveaxis(zb, 1, 0)

    qb = jnp.moveaxis(qr.reshape(b, nb, BLOCK, G, R, Dh), 1, 0)
    kb, vb = band(kr), band(vr)
    rel = jnp.arange(3 * BLOCK)[None, :] - BLOCK - jnp.arange(BLOCK)[:, None]
    in_win = jnp.abs(rel) <= WINDOW
    band_bias = -slopes[:, :, None, None] * jnp.abs(rel).astype(jnp.float32)
    key_idx = (jnp.arange(nb)[:, None] - 1) * BLOCK + jnp.arange(3 * BLOCK)[None, :]
    key_ok = (key_idx >= 0) & (key_idx < s)

    def block_fn(args):
        qn, kn, vn, okn = args
        s_band = jnp.einsum('bqgrd,bkgd->bgrqk', qn, kn) + band_bias
        s_band = jnp.where(in_win & okn[None, :], s_band, -jnp.inf)
        s_meta = jnp.einsum('bqgrd,bmgd->bgrqm', qn, km)
        s_sink = jnp.broadcast_to(sink[None, :, :, None, None], s_band.shape[:-1] + (1,))
        p = jax.nn.softmax(jnp.concatenate([s_band, s_meta, s_sink], axis=-1), axis=-1)
        return (jnp.einsum('bgrqk,bkgd->bqgrd', p[..., :3 * BLOCK], vn)
                + jnp.einsum('bgrqm,bmgd->bqgrd', p[..., 3 * BLOCK:3 * BLOCK + N_META], vm))

    o_real = lax.map(block_fn, (qb, kb, vb, key_ok))
    o_real = jnp.moveaxis(o_real, 0, 1).reshape(b, s, ATT_WIDTH)

    k_mq = jnp.concatenate([km, kr[:, :BLOCK]], axis=1)
    v_mq = jnp.concatenate([vm, vr[:, :BLOCK]], axis=1)
    qpos = jnp.arange(N_META)
    kpos = jnp.arange(N_META + BLOCK)
    rel_m = kpos[None, :] - qpos[:, None]
    is_meta_key = (kpos < N_META)[None, :]
    ok_m = is_meta_key | (jnp.abs(rel_m) <= WINDOW)
    bias_m = jnp.where(is_meta_key, 0.0, -slopes[:, :, None, None] * jnp.abs(rel_m).astype(jnp.float32))
    s_m = jnp.where(ok_m, jnp.einsum('bqgrd,bkgd->bgrqk', qm, k_mq) + bias_m, -jnp.inf)
    s_m_sink = jnp.broadcast_to(sink[None, :, :, None, None], s_m.shape[:-1] + (1,))
    p_m = jax.nn.softmax(jnp.concatenate([s_m, s_m_sink], axis=-1), axis=-1)
    o_meta = jnp.einsum('bgrqk,bkgd->bqgrd', p_m[..., :-1], v_mq).reshape(b, N_META, ATT_WIDTH)
    return jnp.concatenate([o_meta, o_real], axis=1)


def gla_direction(q, k, v, lg, inclusive):
    b, l = q.shape[:2]
    nc = l // GLA_CHUNK

    def chunks(z):
        return z.reshape(b, nc, GLA_CHUNK, GLA_HEADS, z.shape[-1]).transpose(1, 0, 3, 2, 4)

    qc, kc, vc, gc = chunks(q), chunks(k), chunks(v), chunks(lg)
    bc = jnp.cumsum(gc, axis=-2)
    ti = jnp.arange(GLA_CHUNK)
    mask = (ti[:, None] >= ti[None, :]) if inclusive else (ti[:, None] > ti[None, :])

    def step(state, xs):
        qn, kn, vn, bn = xs
        o_inter = jnp.einsum('bhtd,bhde->bhte', qn * jnp.exp(bn), state)
        decay = jnp.exp(jnp.where(mask[:, :, None], bn[:, :, :, None, :] - bn[:, :, None, :, :], -jnp.inf))
        a = jnp.einsum('bhtd,bhsd,bhtsd->bhts', qn, kn, decay)
        o = o_inter + jnp.einsum('bhts,bhse->bhte', a, vn)
        b_last = bn[:, :, -1:, :]
        state = (state * jnp.exp(b_last[:, :, 0, :])[..., None]
                 + jnp.einsum('bhsd,bhse->bhde', kn * jnp.exp(b_last - bn), vn))
        return state, o

    state0 = jnp.zeros((b, GLA_HEADS, GLA_DK, GLA_DV), jnp.float32)
    _, o = lax.scan(step, state0, (qc, kc, vc, bc))
    return o.transpose(1, 0, 3, 2, 4).reshape(b, l, GLA_HEADS, GLA_DV)


def gla_mixer(q, k, v, g, z, w2, bias, norm_g):
    b, t = q.shape[:2]
    z = z.reshape(b, t, N_DIR, GLA_RANK)
    lg = jax.nn.log_sigmoid(jnp.einsum('btnr,nrk->btnk', z, w2) + bias) / GLA_TAU
    lead = GLA_CHUNK - N_META
    pad = ((0, 0), (lead, 0), (0, 0), (0, 0))

    def prep(a, d):
        return jnp.pad(a.reshape(b, t, GLA_HEADS, d), pad)

    qh = prep(q * (GLA_DK ** -0.5), GLA_DK)
    kh = prep(k, GLA_DK)
    vh = prep(v, GLA_DV)
    lf = prep(lg[:, :, 0], GLA_DK)
    lb = prep(lg[:, :, 1], GLA_DK)
    o_f = gla_direction(qh, kh, vh, lf, True)
    o_b = jnp.flip(gla_direction(jnp.flip(qh, 1), jnp.flip(kh, 1), jnp.flip(vh, 1), jnp.flip(lb, 1), False), 1)
    o = (o_f + o_b)[:, lead:]
    o = o * lax.rsqrt(jnp.mean(o * o, axis=-1, keepdims=True) + LN_EPS) * norm_g
    return o.reshape(b, t, GLA_WIDTH) * jax.nn.silu(g)


def token_mixer(h, w_in, attn_sink, gla_w2, gla_b, gla_norm_g, w_out):
    b, t = h.shape[:2]
    proj = (h @ w_in).astype(jnp.float32)
    aq, ak, av, gq, gk, gv, gg, gz = jnp.split(proj, SPLIT_POINTS, axis=-1)
    o_att = window_attention(aq.reshape(b, t, ATT_HEADS, ATT_HEAD_DIM),
                             ak.reshape(b, t, ATT_KV_HEADS, ATT_HEAD_DIM),
                             av.reshape(b, t, ATT_KV_HEADS, ATT_HEAD_DIM), attn_sink)
    o_gla = gla_mixer(gq, gk, gv, gg, gz, gla_w2, gla_b, gla_norm_g)
    return jnp.concatenate([o_att, o_gla], axis=-1).astype(h.dtype) @ w_out


def setup_inputs(seed: int = 0) -> dict:
    key = jax.random.key(seed)
    ks = jax.random.split(key, 24)
    nrm = jax.random.normal
    f32 = jnp.float32

    def gain(k, shape):
        return 1.0 + 0.02 * nrm(k, shape, f32)

    def small(k, shape, s=0.02):
        return s * nrm(k, shape, f32)

    return {
        'x': nrm(ks[0], (BATCH, SEQ, D_MODEL), f32),
        'meta_tokens': nrm(ks[1], (N_META, D_MODEL), f32),
        'ln_in_g': gain(ks[2], (D_MODEL,)),
        'ln_in_b': small(ks[3], (D_MODEL,)),
        'ffn1_ln_g': gain(ks[4], (DEPTH, D_MODEL)),
        'ffn1_ln_b': small(ks[5], (DEPTH, D_MODEL)),
        'ffn1_w_gate': nrm(ks[6], (DEPTH, D_MODEL, D_FF), f32) * D_MODEL ** -0.5,
        'ffn1_w_up': nrm(ks[7], (DEPTH, D_MODEL, D_FF), f32) * D_MODEL ** -0.5,
        'ffn1_w_down': nrm(ks[8], (DEPTH, D_FF, D_MODEL), f32) * (D_FF ** -0.5 * DN_BETA),
        'w_in': nrm(ks[9], (DEPTH, D_MODEL, IN_WIDTH), f32) * D_MODEL ** -0.5,
        'attn_sink': small(ks[10], (DEPTH, ATT_HEADS), 0.5),
        'gla_gate_w2': nrm(ks[11], (DEPTH, N_DIR, GLA_RANK, GLA_QK_WIDTH), f32) * GLA_RANK ** -0.5,
        'gla_gate_b': small(ks[12], (DEPTH, N_DIR, GLA_QK_WIDTH), 0.1),
        'gla_norm_g': gain(ks[13], (DEPTH, GLA_DV)),
        'w_out': nrm(ks[14], (DEPTH, MIX_WIDTH, D_MODEL), f32) * (MIX_WIDTH ** -0.5 * DN_BETA),
        'mix_ln_g': gain(ks[15], (DEPTH, D_MODEL)),
        'mix_ln_b': small(ks[16], (DEPTH, D_MODEL)),
        'ffn2_w_gate': nrm(ks[17], (DEPTH, D_MODEL, D_FF), f32) * D_MODEL ** -0.5,
        'ffn2_w_up': nrm(ks[18], (DEPTH, D_MODEL, D_FF), f32) * D_MODEL ** -0.5,
        'ffn2_w_down': nrm(ks[19], (DEPTH, D_FF, D_MODEL), f32) * (D_FF ** -0.5 * DN_BETA),
        'ffn2_ln_g': gain(ks[20], (DEPTH, D_MODEL)),
        'ffn2_ln_b': small(ks[21], (DEPTH, D_MODEL)),
    }


def reference(x, meta_tokens, ln_in_g, ln_in_b, ffn1_ln_g, ffn1_ln_b, ffn1_w_gate, ffn1_w_up, ffn1_w_down,
              w_in, attn_sink, gla_gate_w2, gla_gate_b, gla_norm_g, w_out, mix_ln_g, mix_ln_b,
              ffn2_w_gate, ffn2_w_up, ffn2_w_down, ffn2_ln_g, ffn2_ln_b):
    b = x.shape[0]
    meta = jnp.broadcast_to(meta_tokens[None].astype(x.dtype), (b, N_META, D_MODEL))
    h = layer_norm(jnp.concatenate([meta, x], axis=1), ln_in_g, ln_in_b)
    for i in range(DEPTH):
        h = layer_norm(DN_ALPHA * h + 0.5 * swiglu(h, ffn1_w_gate[i], ffn1_w_up[i], ffn1_w_down[i]),
                       ffn1_ln_g[i], ffn1_ln_b[i])
        h = layer_norm(DN_ALPHA * h + token_mixer(h, w_in[i], attn_sink[i], gla_gate_w2[i], gla_gate_b[i],
                                                  gla_norm_g[i], w_out[i]),
                       mix_ln_g[i], mix_ln_b[i])
        h = layer_norm(DN_ALPHA * h + 0.5 * swiglu(h, ffn2_w_gate[i], ffn2_w_up[i], ffn2_w_down[i]),
                       ffn2_ln_g[i], ffn2_ln_b[i])
    return h[:, N_META:]
```

```python
import functools

import numpy as np
import jax
import jax.numpy as jnp
from jax import lax
from jax.experimental import pallas as pl
from jax.experimental.pallas import tpu as pltpu

N_META = 16
ATT_HEADS = 8
ATT_KV_HEADS = 2
ATT_HEAD_DIM = 64
ATT_REP = ATT_HEADS // ATT_KV_HEADS
WINDOW = 128
BLOCK = 128
GLA_HEADS = 4
GLA_DK = 64
GLA_DV = 128
GLA_RANK = 16
GLA_TAU = 16.0
N_DIR = 2
ATT_WIDTH = ATT_HEADS * ATT_HEAD_DIM
ATT_KV_WIDTH = ATT_KV_HEADS * ATT_HEAD_DIM
GLA_QK_WIDTH = GLA_HEADS * GLA_DK
GLA_WIDTH = GLA_HEADS * GLA_DV
LN_EPS = 1e-5
DEPTH = 1
DN_ALPHA = (2.0 * DEPTH) ** 0.25

LANES = 128
GLA_BLK = 16
GLA_GROUP = 128
BLK_PER_GROUP = GLA_GROUP // GLA_BLK
FF_CHUNK = 256
ROW_TILE = 512
PREP_TILE = 1024
VMEM_LIMIT = 56 * 1024 * 1024

_BF = jnp.bfloat16
_F32 = jnp.float32


def _dot(a, b):
    return jnp.dot(a, b, preferred_element_type=_F32)


def _dot_nt(a, b):
    return lax.dot_general(a, b, (((1,), (1,)), ((), ())), preferred_element_type=_F32)


def _layer_norm(x, g, b):
    mu = jnp.mean(x, axis=-1, keepdims=True)
    xc = x - mu
    var = jnp.mean(xc * xc, axis=-1, keepdims=True)
    return xc * lax.rsqrt(var + LN_EPS) * g + b


def _silu(x):
    return x * (1.0 / (1.0 + jnp.exp(-x)))


def _split3(x):
    hi = x.astype(_BF)
    r1 = x - hi.astype(_F32)
    mid = r1.astype(_BF)
    lo = (r1 - mid.astype(_F32)).astype(_BF)
    return hi, mid, lo


def _swiglu_ln(h, wg_ref, wu_ref, wd_ref, g_ref, b_ref, act_ref):
    hb = h.astype(_BF)
    n_ff = wg_ref.shape[1]
    for c in range(n_ff // FF_CHUNK):
        sl = slice(c * FF_CHUNK, (c + 1) * FF_CHUNK)
        gate = _dot(hb, wg_ref[:, sl])
        up = _dot(hb, wu_ref[:, sl])
        act_ref[:, sl] = (_silu(gate) * up).astype(_BF)
    y = _dot(act_ref[...], wd_ref[...])
    return _layer_norm(DN_ALPHA * h + 0.5 * y, g_ref[...], b_ref[...])


_P_AQ = (0, 512)
_P_AK = (512, 768)
_P_AV = (768, 1024)
_P_GQ = (1024, 1280)
_P_GK = (1280, 1536)
_P_GV = (1536, 2048)
_P_GG = (2048, 2560)
_P_GZ = (2560, 2688)
_P_WIDTH = 2688


def _stage_a_kernel(x_ref, g0_ref, b0_ref, wg_ref, wu_ref, wd_ref, g1_ref, b1_ref, win_ref,
                    h1_ref, aq_ref, ak_ref, av_ref, gq_ref, gk_ref, gv_ref, gg_ref, gz_ref, act_ref):
    h0 = _layer_norm(x_ref[...], g0_ref[...], b0_ref[...])
    h1 = _swiglu_ln(h0, wg_ref, wu_ref, wd_ref, g1_ref, b1_ref, act_ref)
    h1_ref[...] = h1
    hb = h1.astype(_BF)

    def proj(cols):
        return _dot(hb, win_ref[:, cols[0]:cols[1]])

    aq_ref[...] = (proj(_P_AQ) * (ATT_HEAD_DIM ** -0.5)).astype(_BF)
    ak_ref[...] = proj(_P_AK).astype(_BF)
    av_ref[...] = proj(_P_AV).astype(_BF)
    gq_ref[...] = proj(_P_GQ) * (GLA_DK ** -0.5)
    gk_ref[...] = proj(_P_GK)
    gv_ref[...] = proj(_P_GV).astype(_BF)
    gg_ref[...] = proj(_P_GG)
    gz_ref[...] = proj(_P_GZ)


def _const_spec(shape):
    nd = len(shape)
    return pl.BlockSpec(shape, lambda *_: (0,) * nd, pipeline_mode=pl.Buffered(1))


def _stage_a(x2d, g0, b0, wg, wu, wd, g1, b1, win):
    m, d = x2d.shape
    tm = min(ROW_TILE, m)
    assert m % tm == 0
    n_ff = wg.shape[1]
    row = lambda w: pl.BlockSpec((tm, w), lambda i: (i, 0))
    out_widths = [(d, _F32), (512, _BF), (256, _BF), (256, _BF), (256, _F32), (256, _F32), (512, _BF), (512, _F32),
                  (128, _F32)]
    return pl.pallas_call(
        _stage_a_kernel,
        grid=(m // tm,),
        in_specs=[row(d), _const_spec((1, d)), _const_spec((1, d)), _const_spec((d, n_ff)), _const_spec((d, n_ff)),
                  _const_spec((n_ff, d)), _const_spec((1, d)), _const_spec((1, d)), _const_spec((d, _P_WIDTH))],
        out_specs=[row(w) for w, _ in out_widths],
        out_shape=[jax.ShapeDtypeStruct((m, w), dt) for w, dt in out_widths],
        scratch_shapes=[pltpu.VMEM((tm, n_ff), _BF)],
        compiler_params=pltpu.CompilerParams(dimension_semantics=("arbitrary",), vmem_limit_bytes=VMEM_LIMIT),
        name="stage_a",
    )(x2d, g0, b0, wg, wu, wd, g1, b1, win)


_KWIN = 3 * BLOCK


def _alibi_slopes():
    return [2.0 ** (-8.0 * (i + 1) / ATT_HEADS) for i in range(ATT_HEADS)]


def _attn_kernel(sink_ref, q_ref, k_ref, v_ref, km_ref, vm_ref, o_ref):
    i = pl.program_id(1)
    s_len = k_ref.shape[0]
    start = jnp.clip((i - 1) * BLOCK, 0, s_len - _KWIN)
    start = pl.multiple_of(start, BLOCK)
    kwin = k_ref[pl.ds(start, _KWIN), :]
    vwin = v_ref[pl.ds(start, _KWIN), :]
    km = km_ref[...]
    vm = vm_ref[...]
    rel = (lax.broadcasted_iota(jnp.int32, (BLOCK, _KWIN), 1) - lax.broadcasted_iota(jnp.int32, (BLOCK, _KWIN), 0)
           + (start - i * BLOCK))
    absrel = jnp.abs(rel).astype(_F32)
    in_win = absrel <= float(WINDOW)
    meta_ok = lax.broadcasted_iota(jnp.int32, (BLOCK, LANES), 1) < N_META
    low_half = lax.broadcasted_iota(jnp.int32, (BLOCK, LANES), 1) < ATT_HEAD_DIM
    slopes = _alibi_slopes()
    q = q_ref[...]
    for pair in range(ATT_HEADS // 2):
        qp = q[:, pair * LANES:(pair + 1) * LANES]
        g = (2 * pair) // ATT_REP
        kg = kwin[:, g * LANES:(g + 1) * LANES]
        vg = vwin[:, g * LANES:(g + 1) * LANES]
        kmg = km[:, g * LANES:(g + 1) * LANES]
        vmg = vm[:, g * LANES:(g + 1) * LANES]
        outs = []
        for e in range(2):
            h = 2 * pair + e
            qh = jnp.where(low_half if e == 0 else jnp.logical_not(low_half), qp, jnp.zeros_like(qp))
            s_band = _dot_nt(qh, kg) - slopes[h] * absrel
            s_band = jnp.where(in_win, s_band, -jnp.inf)
            s_meta = jnp.where(meta_ok, _dot_nt(qh, kmg), -jnp.inf)
            sink = sink_ref[h]
            mx = jnp.maximum(jnp.maximum(jnp.max(s_band, axis=-1, keepdims=True),
                                         jnp.max(s_meta, axis=-1, keepdims=True)), sink)
            p_band = jnp.exp(s_band - mx)
            p_meta = jnp.exp(s_meta - mx)
            den = (jnp.sum(p_band, axis=-1, keepdims=True) + jnp.sum(p_meta, axis=-1, keepdims=True)
                   + jnp.exp(sink - mx))
            o = _dot(p_band.astype(_BF), vg) + _dot(p_meta.astype(_BF), vmg)
            outs.append(o * (1.0 / den))
        o_ref[:, pair * LANES:(pair + 1) * LANES] = jnp.where(low_half, outs[0], outs[1]).astype(_BF)


def _attention(aq, ak2, av2, km2, vm2, sink, batch, seq):
    m = aq.shape[0]
    nq = seq // BLOCK
    return pl.pallas_call(
        _attn_kernel,
        grid=(batch, nq),
        in_specs=[pl.BlockSpec(memory_space=pltpu.SMEM),
                  pl.BlockSpec((BLOCK, ATT_WIDTH), lambda b, i: (b * nq + i, 0)),
                  pl.BlockSpec((seq, 2 * LANES), lambda b, i: (b, 0)),
                  pl.BlockSpec((seq, 2 * LANES), lambda b, i: (b, 0)),
                  pl.BlockSpec((LANES, 2 * LANES), lambda b, i: (0, 0)),
                  pl.BlockSpec((LANES, 2 * LANES), lambda b, i: (0, 0))],
        out_specs=pl.BlockSpec((BLOCK, ATT_WIDTH), lambda b, i: (b * nq + i, 0)),
        out_shape=jax.ShapeDtypeStruct((m, ATT_WIDTH), _BF),
        compiler_params=pltpu.CompilerParams(dimension_semantics=("arbitrary", "arbitrary"),
                                             vmem_limit_bytes=VMEM_LIMIT),
        name="attention",
    )(sink, aq, ak2, av2, km2, vm2)


def _gla_log_gates(z, w2_ref, bias_ref):
    zh, zm, zl = _split3(z)
    w_hi, w_mid, w_lo = w2_ref[0], w2_ref[1], w2_ref[2]
    logits = (_dot(zh, w_hi) + (_dot(zh, w_mid) + _dot(zm, w_hi))
              + (_dot(zh, w_lo) + _dot(zm, w_mid) + _dot(zl, w_hi))) + bias_ref[...]
    log_sig = jnp.minimum(logits, 0.0) - jnp.log1p(jnp.exp(-jnp.abs(logits)))
    return log_sig * (1.0 / GLA_TAU)


def _sel_matrix():
    sel = np.zeros((N_DIR, 2, GLA_BLK, LANES, LANES), np.float32)
    for d in range(N_DIR):
        for lg in range(2):
            for s in range(GLA_BLK):
                for hl in range(2):
                    col = d * 64 + (2 * lg + hl) * GLA_BLK + s
                    sel[d, lg, s, hl * GLA_DK:(hl + 1) * GLA_DK, col] = 1.0
    return sel.reshape(N_DIR, 2 * GLA_BLK * LANES, LANES)


def _gla_prep_kernel(q0_ref, q1_ref, k0_ref, k1_ref, z_ref, w2_ref, bias_ref, sel_ref,
                     qe_ref, ket_ref, dec_ref, r_ref,
                     g_scr, b_scr, k_scr, qe_scr, ke_scr, racc_scr, rtm_scr, dec_scr):
    tt_rows = z_ref.shape[0]
    nb = tt_rows // GLA_BLK
    rc = min(512, tt_rows)
    for c in range(tt_rows // rc):
        rows = pl.ds(c * rc, rc)
        g = _gla_log_gates(z_ref[rows, :], w2_ref, bias_ref)
        for gi in range(4):
            g_scr[gi, rows, :] = g[:, gi * LANES:(gi + 1) * LANES]

    def slab(ref, tt):
        return ref[pl.ds(tt, nb, stride=GLA_BLK), :]

    racc_scr[...] = jnp.zeros_like(racc_scr)
    for lg, (q_ref, k_ref) in enumerate(((q0_ref, k0_ref), (q1_ref, k1_ref))):
        for tt in range(GLA_BLK):
            k_scr[tt] = slab(k_ref, tt)
        for d in range(N_DIR):
            gi = d * 2 + lg
            acc = None
            for tt in (range(GLA_BLK) if d == 0 else reversed(range(GLA_BLK))):
                gs = g_scr[gi, pl.ds(tt, nb, stride=GLA_BLK), :]
                acc = gs if acc is None else acc + gs
                b_scr[tt] = acc
            last = acc
            dec_scr[gi] = jnp.exp(last)
            for tt in range(GLA_BLK):
                q = slab(q_ref, tt)
                b = b_scr[tt]
                qe_scr[gi, pl.ds(tt, nb, stride=GLA_BLK), :] = q * jnp.exp(b)
                ke_scr[gi, pl.ds(tt, nb, stride=GLA_BLK), :] = k_scr[tt] * jnp.exp(last - b)
                positions = list(range(0, tt + 1)) if d == 0 else list(range(tt + 1, GLA_BLK))
                if not positions:
                    continue
                prods = []
                for s in positions:
                    p = q * k_scr[s]
                    if s != tt:
                        p = p * jnp.exp(b - b_scr[s])
                    prods.append(p.astype(_BF))
                pcat = jnp.concatenate(prods, axis=1) if len(prods) > 1 else prods[0]
                r0 = lg * GLA_BLK * LANES + positions[0] * LANES
                racc_scr[tt] += _dot(pcat, sel_ref[d, r0:r0 + len(positions) * LANES, :])
    for tt in range(GLA_BLK):
        rtm_scr[pl.ds(tt, nb, stride=GLA_BLK), :] = racc_scr[tt]
    r_ref[...] = rtm_scr[...].astype(_BF)
    for gi in range(4):
        qe_ref[:, gi * LANES:(gi + 1) * LANES] = qe_scr[gi].astype(_BF)
        ket_ref[gi * LANES:(gi + 1) * LANES, :] = ke_scr[gi].T.astype(_BF)
        dec_ref[:, gi * LANES:(gi + 1) * LANES] = dec_scr[gi]


def _gla_prep(gq, gk, gz, w2s, bias, sel):
    m = gq.shape[0]
    tt = min(PREP_TILE, m)
    assert m % tt == 0 and tt % (8 * GLA_BLK) == 0
    nb = tt // GLA_BLK
    col = lambda j: pl.BlockSpec((tt, LANES), lambda i: (i, j))
    return pl.pallas_call(
        _gla_prep_kernel,
        grid=(m // tt,),
        in_specs=[col(0), col(1), col(0), col(1), col(0),
                  _const_spec(w2s.shape), _const_spec(bias.shape), _const_spec(sel.shape)],
        out_specs=[pl.BlockSpec((tt, 4 * LANES), lambda i: (i, 0)),
                   pl.BlockSpec((4 * LANES, tt), lambda i: (0, i)),
                   pl.BlockSpec((nb, 4 * LANES), lambda i: (i, 0)),
                   pl.BlockSpec((tt, LANES), lambda i: (i, 0))],
        out_shape=[jax.ShapeDtypeStruct((m, 4 * LANES), _BF),
                   jax.ShapeDtypeStruct((4 * LANES, m), _BF),
                   jax.ShapeDtypeStruct((m // GLA_BLK, 4 * LANES), _F32),
                   jax.ShapeDtypeStruct((m, LANES), _BF)],
        scratch_shapes=[pltpu.VMEM((4, tt, LANES), _F32),
                        pltpu.VMEM((GLA_BLK, nb, LANES), _F32),
                        pltpu.VMEM((GLA_BLK, nb, LANES), _F32),
                        pltpu.VMEM((4, tt, LANES), _F32),
                        pltpu.VMEM((4, tt, LANES), _F32),
                        pltpu.VMEM((GLA_BLK, nb, LANES), _F32),
                        pltpu.VMEM((tt, LANES), _F32),
                        pltpu.VMEM((4, nb, LANES), _F32)],
        compiler_params=pltpu.CompilerParams(dimension_semantics=("arbitrary",), vmem_limit_bytes=VMEM_LIMIT),
        name="gla_prep",
    )(gq, gq, gk, gk, gz, w2s, bias, sel)


def _gla_meta_state_kernel(k_ref, z_ref, v_ref, w2_ref, bias_ref, s0_ref):
    g = _gla_log_gates(z_ref[...], w2_ref, bias_ref)[:, :GLA_QK_WIDTH]
    r = lax.broadcasted_iota(jnp.int32, (LANES, LANES), 0)
    c = lax.broadcasted_iota(jnp.int32, (LANES, LANES), 1)
    tail = jnp.where((c > r) & (c < N_META), 1.0, 0.0).astype(_BF)
    gh, gm, gl = _split3(g)
    rest = _dot(tail, gh) + _dot(tail, gm) + _dot(tail, gl)
    is_meta = lax.broadcasted_iota(jnp.int32, (LANES, GLA_QK_WIDTH), 0) < N_META
    ke = jnp.where(is_meta, k_ref[...] * jnp.exp(jnp.where(is_meta, rest, 0.0)), 0.0)
    ket = ke.T.astype(_BF)
    v = v_ref[...]
    for h in range(GLA_HEADS):
        s0_ref[h * GLA_DK:(h + 1) * GLA_DK, :] = _dot(ket[h * GLA_DK:(h + 1) * GLA_DK, :],
                                                      v[:, h * GLA_DV:(h + 1) * GLA_DV])


def _gla_meta_state(gk_m, gz_m, gv_m, w2s, bias):
    return pl.pallas_call(
        _gla_meta_state_kernel,
        out_shape=jax.ShapeDtypeStruct((GLA_QK_WIDTH, GLA_DV), _F32),
        name="gla_meta_state",
    )(gk_m, gz_m, gv_m, w2s, bias)


def _gla_scan_kernel(qf_ref, qb_ref, rf_ref, rb_ref, kf_ref, kb_ref, df_ref, db_ref, vf_ref, vb_ref, s0_ref,
                     of_ref, ob_ref, sf_scr, sb_scr):
    @pl.when(pl.program_id(1) == 0)
    def _():
        sf_scr[...] = s0_ref[...]
        sb_scr[...] = jnp.zeros_like(sb_scr)

    def lane_band(shape, width, n):
        lane = lax.broadcasted_iota(jnp.int32, shape, 1)
        return [(lane >= c * width) & (lane < (c + 1) * width) for c in range(n)]

    head_lanes = lane_band((GLA_BLK, GLA_QK_WIDTH), GLA_DK, GLA_HEADS)
    a_lanes = lane_band((GLA_BLK, LANES), GLA_BLK, N_DIR * GLA_HEADS)
    tok_lanes = lane_band((GLA_DK, LANES), GLA_BLK, BLK_PER_GROUP)
    dirs = ((qf_ref, rf_ref, kf_ref, df_ref, vf_ref, of_ref, sf_scr),
            (qb_ref, rb_ref, kb_ref, db_ref, vb_ref, ob_ref, sb_scr))
    dec_t = [df_ref[...].T, db_ref[...].T]
    for step in range(BLK_PER_GROUP):
        for d, (q_ref, r_ref, k_ref, _, v_ref, o_ref, s_scr) in enumerate(dirs):
            j = step if d == 0 else BLK_PER_GROUP - 1 - step
            rows = slice(j * GLA_BLK, (j + 1) * GLA_BLK)
            state = s_scr[...]
            qe = q_ref[rows, :]
            lhs1 = jnp.concatenate(
                [jnp.where(head_lanes[h], qe, jnp.zeros_like(qe)) for h in range(GLA_HEADS)], axis=0)
            rr = r_ref[rows, :]
            lhs2 = jnp.concatenate(
                [jnp.where(a_lanes[d * GLA_HEADS + h], rr, jnp.zeros_like(rr)) for h in range(GLA_HEADS)], axis=0)
            vblk = v_ref[rows, :]
            vstack = jnp.concatenate([vblk[:, h * GLA_DV:(h + 1) * GLA_DV] for h in range(GLA_HEADS)], axis=0)
            vv = jnp.concatenate([vstack, vstack], axis=0)
            o = _dot(lhs1, state.astype(_BF)) + _dot(lhs2, vv)
            for h in range(GLA_HEADS):
                o_ref[h, rows, :] = o[h * GLA_BLK:(h + 1) * GLA_BLK, :]
            dcol = dec_t[d][:, j:j + 1]
            for h in range(GLA_HEADS):
                hs = slice(h * GLA_DK, (h + 1) * GLA_DK)
                kt = k_ref[hs, :]
                kt = jnp.where(tok_lanes[j], kt, jnp.zeros_like(kt))
                upd = _dot(kt, v_ref[:, h * GLA_DV:(h + 1) * GLA_DV])
                s_scr[hs, :] = state[hs, :] * dcol[hs, :] + upd


def _gla_scan(qe, r, ket, dec, gv, s0, batch, seq):
    m = qe.shape[0]
    ng = seq // GLA_GROUP
    fwd = lambda b, i: b * ng + i
    bwd = lambda b, i: b * ng + (ng - 1 - i)
    o_shape = jax.ShapeDtypeStruct((GLA_HEADS, m, GLA_DV), _F32)
    return pl.pallas_call(
        _gla_scan_kernel,
        grid=(batch, ng),
        in_specs=[pl.BlockSpec((GLA_GROUP, GLA_QK_WIDTH), lambda b, i: (fwd(b, i), 0)),
                  pl.BlockSpec((GLA_GROUP, GLA_QK_WIDTH), lambda b, i: (bwd(b, i), 1)),
                  pl.BlockSpec((GLA_GROUP, LANES), lambda b, i: (fwd(b, i), 0)),
                  pl.BlockSpec((GLA_GROUP, LANES), lambda b, i: (bwd(b, i), 0)),
                  pl.BlockSpec((GLA_QK_WIDTH, GLA_GROUP), lambda b, i: (0, fwd(b, i))),
                  pl.BlockSpec((GLA_QK_WIDTH, GLA_GROUP), lambda b, i: (1, bwd(b, i))),
                  pl.BlockSpec((BLK_PER_GROUP, GLA_QK_WIDTH), lambda b, i: (fwd(b, i), 0)),
                  pl.BlockSpec((BLK_PER_GROUP, GLA_QK_WIDTH), lambda b, i: (bwd(b, i), 1)),
                  pl.BlockSpec((GLA_GROUP, GLA_WIDTH), lambda b, i: (fwd(b, i), 0)),
                  pl.BlockSpec((GLA_GROUP, GLA_WIDTH), lambda b, i: (bwd(b, i), 0)),
                  pl.BlockSpec((GLA_QK_WIDTH, GLA_DV), lambda b, i: (0, 0))],
        out_specs=[pl.BlockSpec((GLA_HEADS, GLA_GROUP, GLA_DV), lambda b, i: (0, fwd(b, i), 0)),
                   pl.BlockSpec((GLA_HEADS, GLA_GROUP, GLA_DV), lambda b, i: (0, bwd(b, i), 0))],
        out_shape=[o_shape, o_shape],
        scratch_shapes=[pltpu.VMEM((GLA_QK_WIDTH, GLA_DV), _F32), pltpu.VMEM((GLA_QK_WIDTH, GLA_DV), _F32)],
        compiler_params=pltpu.CompilerParams(dimension_semantics=("arbitrary", "arbitrary"),
                                             vmem_limit_bytes=VMEM_LIMIT),
        name="gla_scan",
    )(qe, qe, r, r, ket, ket, dec, dec, gv, gv, s0)


def _stage_c_kernel(h1_ref, oatt_ref, of_ref, ob_ref, gg_ref, ng_ref, wout_ref, gm_ref, bm_ref,
                    wg_ref, wu_ref, wd_ref, g2_ref, b2_ref, out_ref, act_ref):
    gg = gg_ref[...]
    pieces = []
    for h in range(GLA_HEADS):
        o = of_ref[h] + ob_ref[h]
        o = o * lax.rsqrt(jnp.mean(o * o, axis=-1, keepdims=True) + LN_EPS) * ng_ref[...]
        pieces.append((o * _silu(gg[:, h * GLA_DV:(h + 1) * GLA_DV])).astype(_BF))
    mix = jnp.concatenate([oatt_ref[...]] + pieces, axis=1)
    h2 = _layer_norm(DN_ALPHA * h1_ref[...] + _dot(mix, wout_ref[...]), gm_ref[...], bm_ref[...])
    out_ref[...] = _swiglu_ln(h2, wg_ref, wu_ref, wd_ref, g2_ref, b2_ref, act_ref)


def _stage_c(h1, oatt, o_f, o_b, gg, ng, wout, gm, bm, wg, wu, wd, g2, b2):
    m, d = h1.shape
    tm = min(ROW_TILE, m)
    assert m % tm == 0
    n_ff = wg.shape[1]
    row = lambda w: pl.BlockSpec((tm, w), lambda i: (i, 0))
    hrow = pl.BlockSpec((GLA_HEADS, tm, GLA_DV), lambda i: (0, i, 0))
    return pl.pallas_call(
        _stage_c_kernel,
        grid=(m // tm,),
        in_specs=[row(d), row(ATT_WIDTH), hrow, hrow, row(GLA_WIDTH), _const_spec((1, GLA_DV)),
                  _const_spec(wout.shape), _const_spec((1, d)), _const_spec((1, d)),
                  _const_spec((d, n_ff)), _const_spec((d, n_ff)), _const_spec((n_ff, d)),
                  _const_spec((1, d)), _const_spec((1, d))],
        out_specs=row(d),
        out_shape=jax.ShapeDtypeStruct((m, d), _F32),
        scratch_shapes=[pltpu.VMEM((tm, n_ff), _BF)],
        compiler_params=pltpu.CompilerParams(dimension_semantics=("arbitrary",), vmem_limit_bytes=VMEM_LIMIT),
        name="stage_c",
    )(h1, oatt, o_f, o_b, gg, ng, wout, gm, bm, wg, wu, wd, g2, b2)


def _projection_weight(w_in):
    aq = w_in[:, 0:512]
    ak = w_in[:, 512:640]
    av = w_in[:, 640:768]
    rest = w_in[:, 768:2304]
    gz = w_in[:, 2304:2336]

    def dup(w):
        return jnp.concatenate([w[:, 0:64], w[:, 0:64], w[:, 64:128], w[:, 64:128]], axis=1)

    gz_pad = jnp.pad(gz, ((0, 0), (0, LANES - gz.shape[1])))
    return jnp.concatenate([aq, dup(ak), dup(av), rest, gz_pad], axis=1).astype(_BF)


def _gate_weight(w2, bias):
    wcat = jnp.zeros((LANES, N_DIR * GLA_QK_WIDTH), _F32)
    for n in range(N_DIR):
        wcat = wcat.at[n * GLA_RANK:(n + 1) * GLA_RANK, n * GLA_QK_WIDTH:(n + 1) * GLA_QK_WIDTH].set(w2[n])
    hi = wcat.astype(_BF)
    r1 = wcat - hi.astype(_F32)
    mid = r1.astype(_BF)
    lo = (r1 - mid.astype(_F32)).astype(_BF)
    return jnp.stack([hi, mid, lo]), bias.reshape(1, N_DIR * GLA_QK_WIDTH)


def _pad_rows(a, rows):
    return jnp.pad(a, ((0, rows - a.shape[0]), (0, 0)))


def kernel(x, meta_tokens, ln_in_g, ln_in_b, ffn1_ln_g, ffn1_ln_b, ffn1_w_gate, ffn1_w_up, ffn1_w_down, w_in,
           attn_sink, gla_gate_w2, gla_gate_b, gla_norm_g, w_out, mix_ln_g, mix_ln_b, ffn2_w_gate, ffn2_w_up,
           ffn2_w_down, ffn2_ln_g, ffn2_ln_b):
    batch, seq, d = x.shape
    assert seq % BLOCK == 0 and seq >= _KWIN and seq % GLA_GROUP == 0
    vec = lambda a: a.reshape(1, -1)
    x2d = x.reshape(batch * seq, d)
    wg1, wu1, wd1 = ffn1_w_gate[0].astype(_BF), ffn1_w_up[0].astype(_BF), ffn1_w_down[0].astype(_BF)
    wg2, wu2, wd2 = ffn2_w_gate[0].astype(_BF), ffn2_w_up[0].astype(_BF), ffn2_w_down[0].astype(_BF)
    win = _projection_weight(w_in[0])
    w2s, gbias = _gate_weight(gla_gate_w2[0], gla_gate_b[0])
    sel = jnp.asarray(_sel_matrix(), _BF)

    a_args = (vec(ln_in_g), vec(ln_in_b), wg1, wu1, wd1, vec(ffn1_ln_g[0]), vec(ffn1_ln_b[0]), win)
    h1, aq, ak2, av2, gq, gk, gv, gg, gz = _stage_a(x2d, *a_args)
    _, _, ak2_m, av2_m, _, gk_m, gv_m, _, gz_m = _stage_a(meta_tokens.astype(x.dtype), *a_args)

    o_att = _attention(aq, ak2, av2, _pad_rows(ak2_m, LANES), _pad_rows(av2_m, LANES),
                       attn_sink[0].astype(_F32), batch, seq)

    qe, ket, dec, r = _gla_prep(gq, gk, gz, w2s, gbias, sel)
    s0 = _gla_meta_state(_pad_rows(gk_m, LANES), _pad_rows(gz_m, LANES), _pad_rows(gv_m, LANES), w2s, gbias)
    o_f, o_b = _gla_scan(qe, r, ket, dec, gv, s0, batch, seq)

    out = _stage_c(h1, o_att, o_f, o_b, gg, vec(gla_norm_g[0]), w_out[0].astype(_BF),
                   vec(mix_ln_g[0]), vec(mix_ln_b[0]), wg2, wu2, wd2, vec(ffn2_ln_g[0]), vec(ffn2_ln_b[0]))
    return out.reshape(batch, seq, d)
```

```python
import functools

import numpy as np
import jax
import jax.numpy as jnp
from jax import lax
from jax.experimental import pallas as pl
from jax.experimental.pallas import tpu as pltpu

N_META = 16
ATT_HEADS = 8
ATT_KV_HEADS = 2
ATT_HEAD_DIM = 64
ATT_REP = ATT_HEADS // ATT_KV_HEADS
WINDOW = 128
BLOCK = 128
GLA_HEADS = 4
GLA_DK = 64
GLA_DV = 128
GLA_RANK = 16
GLA_TAU = 16.0
N_DIR = 2
ATT_WIDTH = ATT_HEADS * ATT_HEAD_DIM
ATT_KV_WIDTH = ATT_KV_HEADS * ATT_HEAD_DIM
GLA_QK_WIDTH = GLA_HEADS * GLA_DK
GLA_WIDTH = GLA_HEADS * GLA_DV
LN_EPS = 1e-5
DEPTH = 1
DN_ALPHA = (2.0 * DEPTH) ** 0.25

LANES = 128
GLA_BLK = 16
GLA_GROUP = 128
BLK_PER_GROUP = GLA_GROUP // GLA_BLK
FF_CHUNK = 256
ROW_TILE = 512
PREP_TILE = 1024
VMEM_LIMIT = 56 * 1024 * 1024

_BF = jnp.bfloat16
_F32 = jnp.float32


def _dot(a, b):
    return jnp.dot(a, b, preferred_element_type=_F32)


def _dot_nt(a, b):
    return lax.dot_general(a, b, (((1,), (1,)), ((), ())), preferred_element_type=_F32)


def _layer_norm(x, g, b):
    mu = jnp.mean(x, axis=-1, keepdims=True)
    xc = x - mu
    var = jnp.mean(xc * xc, axis=-1, keepdims=True)
    return xc * lax.rsqrt(var + LN_EPS) * g + b


def _silu(x):
    return x * (1.0 / (1.0 + jnp.exp(-x)))


def _split3(x):
    hi = x.astype(_BF)
    r1 = x - hi.astype(_F32)
    mid = r1.astype(_BF)
    lo = (r1 - mid.astype(_F32)).astype(_BF)
    return hi, mid, lo


def _swiglu_ln(h, wg_ref, wu_ref, wd_ref, g_ref, b_ref, act_ref):
    hb = h.astype(_BF)
    n_ff = wg_ref.shape[1]
    for c in range(n_ff // FF_CHUNK):
        sl = slice(c * FF_CHUNK, (c + 1) * FF_CHUNK)
        gate = _dot(hb, wg_ref[:, sl])
        up = _dot(hb, wu_ref[:, sl])
        act_ref[:, sl] = (_silu(gate) * up).astype(_BF)
    y = _dot(act_ref[...], wd_ref[...])
    return _layer_norm(DN_ALPHA * h + 0.5 * y, g_ref[...], b_ref[...])


_P_AQ = (0, 512)
_P_AK = (512, 768)
_P_AV = (768, 1024)
_P_GQ = (1024, 1280)
_P_GK = (1280, 1536)
_P_GV = (1536, 2048)
_P_GG = (2048, 2560)
_P_GZ = (2560, 2688)
_P_WIDTH = 2688


def _stage_a_kernel(x_ref, g0_ref, b0_ref, wg_ref, wu_ref, wd_ref, g1_ref, b1_ref, win_ref,
                    h1_ref, aq_ref, ak_ref, av_ref, gq_ref, gk_ref, gv_ref, gg_ref, gz_ref, act_ref):
    h0 = _layer_norm(x_ref[...], g0_ref[...], b0_ref[...])
    h1 = _swiglu_ln(h0, wg_ref, wu_ref, wd_ref, g1_ref, b1_ref, act_ref)
    h1_ref[...] = h1
    hb = h1.astype(_BF)

    def proj(cols):
        return _dot(hb, win_ref[:, cols[0]:cols[1]])

    aq_ref[...] = (proj(_P_AQ) * (ATT_HEAD_DIM ** -0.5)).astype(_BF)
    ak_ref[...] = proj(_P_AK).astype(_BF)
    av_ref[...] = proj(_P_AV).astype(_BF)
    gq_ref[...] = proj(_P_GQ) * (GLA_DK ** -0.5)
    gk_ref[...] = proj(_P_GK)
    gv_ref[...] = proj(_P_GV).astype(_BF)
    gg_ref[...] = proj(_P_GG)
    gz_ref[...] = proj(_P_GZ)


def _const_spec(shape):
    nd = len(shape)
    return pl.BlockSpec(shape, lambda *_: (0,) * nd, pipeline_mode=pl.Buffered(1))


def _stage_a(x2d, g0, b0, wg, wu, wd, g1, b1, win):
    m, d = x2d.shape
    tm = min(ROW_TILE, m)
    assert m % tm == 0
    n_ff = wg.shape[1]
    row = lambda w: pl.BlockSpec((tm, w), lambda i: (i, 0))
    out_widths = [(d, _F32), (512, _BF), (256, _BF), (256, _BF), (256, _F32), (256, _F32), (512, _BF), (512, _F32),
                  (128, _F32)]
    return pl.pallas_call(
        _stage_a_kernel,
        grid=(m // tm,),
        in_specs=[row(d), _const_spec((1, d)), _const_spec((1, d)), _const_spec((d, n_ff)), _const_spec((d, n_ff)),
                  _const_spec((n_ff, d)), _const_spec((1, d)), _const_spec((1, d)), _const_spec((d, _P_WIDTH))],
        out_specs=[row(w) for w, _ in out_widths],
        out_shape=[jax.ShapeDtypeStruct((m, w), dt) for w, dt in out_widths],
        scratch_shapes=[pltpu.VMEM((tm, n_ff), _BF)],
        compiler_params=pltpu.CompilerParams(dimension_semantics=("arbitrary",), vmem_limit_bytes=VMEM_LIMIT),
        name="stage_a",
    )(x2d, g0, b0, wg, wu, wd, g1, b1, win)


_KWIN = 3 * BLOCK


def _alibi_slopes():
    return [2.0 ** (-8.0 * (i + 1) / ATT_HEADS) for i in range(ATT_HEADS)]


_WIN_OFFSETS = (0, -BLOCK, -2 * BLOCK)


def _attn_kernel(sink_ref, q_ref, k_ref, v_ref, km_ref, vm_ref, o_ref, band_scr, meta_scr):
    i = pl.program_id(1)
    s_len = k_ref.shape[0]
    nq = s_len // BLOCK

    @pl.when((pl.program_id(0) == 0) & (i == 0))
    def _():
        rel0 = (lax.broadcasted_iota(jnp.int32, (BLOCK, _KWIN), 1)
                - lax.broadcasted_iota(jnp.int32, (BLOCK, _KWIN), 0))
        lane = lax.broadcasted_iota(jnp.int32, (BLOCK, LANES), 1)
        slopes = _alibi_slopes()
        for h in range(ATT_HEADS):
            g, hl = divmod(h, ATT_REP)
            rows = slice(hl * BLOCK, (hl + 1) * BLOCK)
            for var, off in enumerate(_WIN_OFFSETS):
                absrel = jnp.abs(rel0 + off).astype(_F32)
                band_scr[var, g, rows, :] = jnp.where(absrel <= float(WINDOW), -slopes[h] * absrel, -jnp.inf)
            meta_scr[g, rows, :] = jnp.where(lane < N_META, 0.0, jnp.where(lane == N_META, sink_ref[h], -jnp.inf))

    start = jnp.clip((i - 1) * BLOCK, 0, s_len - _KWIN)
    start = pl.multiple_of(start, BLOCK)
    var = jnp.where(i == 0, 0, jnp.where(i == nq - 1, 2, 1))
    low_half = lax.broadcasted_iota(jnp.int32, (BLOCK, LANES), 1) < ATT_HEAD_DIM
    q = q_ref[...]
    zero = jnp.zeros((BLOCK, LANES), _BF)
    for g in range(ATT_KV_HEADS):
        lanes_g = slice(g * LANES, (g + 1) * LANES)
        kg = k_ref[pl.ds(start, _KWIN), lanes_g]
        vg = v_ref[pl.ds(start, _KWIN), lanes_g]
        kmg = km_ref[:, lanes_g]
        vmg = vm_ref[:, lanes_g]
        qs = []
        for hl in range(ATT_REP):
            pair, e = divmod(g * ATT_REP + hl, 2)
            qp = q[:, pair * LANES:(pair + 1) * LANES]
            qs.append(jnp.where(low_half if e == 0 else jnp.logical_not(low_half), qp, zero))
        qs = jnp.concatenate(qs, axis=0)
        s_band = _dot_nt(qs, kg) + band_scr[var, g]
        s_meta = _dot_nt(qs, kmg) + meta_scr[g]
        mx = jnp.maximum(jnp.max(s_band, axis=-1, keepdims=True), jnp.max(s_meta, axis=-1, keepdims=True))
        p_band = jnp.exp(s_band - mx)
        p_meta = jnp.exp(s_meta - mx)
        den = jnp.sum(p_band, axis=-1, keepdims=True) + jnp.sum(p_meta, axis=-1, keepdims=True)
        o = (_dot(p_band.astype(_BF), vg) + _dot(p_meta.astype(_BF), vmg)) * (1.0 / den)
        for pl_ in range(ATT_REP // 2):
            even = o[(2 * pl_) * BLOCK:(2 * pl_ + 1) * BLOCK, :]
            odd = o[(2 * pl_ + 1) * BLOCK:(2 * pl_ + 2) * BLOCK, :]
            pair = g * (ATT_REP // 2) + pl_
            o_ref[:, pair * LANES:(pair + 1) * LANES] = jnp.where(low_half, even, odd).astype(_BF)


def _attention(aq, ak2, av2, km2, vm2, sink, batch, seq):
    m = aq.shape[0]
    nq = seq // BLOCK
    return pl.pallas_call(
        _attn_kernel,
        grid=(batch, nq),
        in_specs=[pl.BlockSpec(memory_space=pltpu.SMEM),
                  pl.BlockSpec((BLOCK, ATT_WIDTH), lambda b, i: (b * nq + i, 0)),
                  pl.BlockSpec((seq, 2 * LANES), lambda b, i: (b, 0)),
                  pl.BlockSpec((seq, 2 * LANES), lambda b, i: (b, 0)),
                  pl.BlockSpec((LANES, 2 * LANES), lambda b, i: (0, 0)),
                  pl.BlockSpec((LANES, 2 * LANES), lambda b, i: (0, 0))],
        out_specs=pl.BlockSpec((BLOCK, ATT_WIDTH), lambda b, i: (b * nq + i, 0)),
        out_shape=jax.ShapeDtypeStruct((m, ATT_WIDTH), _BF),
        scratch_shapes=[pltpu.VMEM((len(_WIN_OFFSETS), ATT_KV_HEADS, ATT_REP * BLOCK, _KWIN), _F32),
                        pltpu.VMEM((ATT_KV_HEADS, ATT_REP * BLOCK, LANES), _F32)],
        compiler_params=pltpu.CompilerParams(dimension_semantics=("arbitrary", "arbitrary"),
                                             vmem_limit_bytes=VMEM_LIMIT),
        name="attention",
    )(sink, aq, ak2, av2, km2, vm2)


_LOG2E = 1.4426950408889634


def _gla_log2_gates(z, w2_ref, bias_ref):
    zh, zm, zl = _split3(z)
    w_hi, w_mid, w_lo = w2_ref[0], w2_ref[1], w2_ref[2]
    logits = (_dot(zh, w_hi) + (_dot(zh, w_mid) + _dot(zm, w_hi))
              + (_dot(zh, w_lo) + _dot(zm, w_mid) + _dot(zl, w_hi))) + bias_ref[...]
    log_sig = jnp.minimum(logits, 0.0) - jnp.log1p(jnp.exp(-jnp.abs(logits)))
    return log_sig * (_LOG2E / GLA_TAU)


def _sel_matrix():
    sel = np.zeros((N_DIR, 2, GLA_BLK, LANES, LANES), np.float32)
    for d in range(N_DIR):
        for lg in range(2):
            for s in range(GLA_BLK):
                for hl in range(2):
                    col = d * 64 + (2 * lg + hl) * GLA_BLK + s
                    sel[d, lg, s, hl * GLA_DK:(hl + 1) * GLA_DK, col] = 1.0
    return sel.reshape(N_DIR, 2 * GLA_BLK * LANES, LANES)


def _gla_prep_kernel(q0_ref, q1_ref, k0_ref, k1_ref, z_ref, w2_ref, bias_ref, sel_ref,
                     qe_ref, ke_ref, dec_ref, r_ref,
                     g_scr, b_scr, kbf_scr, qe_scr, ke_scr, racc_scr, rtm_scr, dec_scr):
    tt_rows = z_ref.shape[0]
    nb = tt_rows // GLA_BLK
    rc = min(512, tt_rows)
    for c in range(tt_rows // rc):
        rows = pl.ds(c * rc, rc)
        g = _gla_log2_gates(z_ref[rows, :], w2_ref, bias_ref)
        for gi in range(4):
            g_scr[gi, rows, :] = g[:, gi * LANES:(gi + 1) * LANES]

    def slab(ref, tt):
        return ref[pl.ds(tt, nb, stride=GLA_BLK), :]

    racc_scr[...] = jnp.zeros_like(racc_scr)
    for lg, (q_ref, k_ref) in enumerate(((q0_ref, k0_ref), (q1_ref, k1_ref))):
        for tt in range(GLA_BLK):
            kbf_scr[tt] = slab(k_ref, tt).astype(_BF)
        for d in range(N_DIR):
            gi = d * 2 + lg
            acc = None
            for tt in (range(GLA_BLK) if d == 0 else reversed(range(GLA_BLK))):
                gs = g_scr[gi, pl.ds(tt, nb, stride=GLA_BLK), :]
                acc = gs if acc is None else acc + gs
                b_scr[tt] = acc
            last = acc
            dec_scr[gi] = jnp.exp2(last)
            for tt in range(GLA_BLK):
                q = slab(q_ref, tt)
                b = b_scr[tt]
                qe_scr[gi, pl.ds(tt, nb, stride=GLA_BLK), :] = q * jnp.exp2(b)
                ke_scr[gi, pl.ds(tt, nb, stride=GLA_BLK), :] = slab(k_ref, tt) * jnp.exp2(last - b)
                positions = list(range(0, tt + 1)) if d == 0 else list(range(tt + 1, GLA_BLK))
                if not positions:
                    continue
                qb = q.astype(_BF)
                prods = []
                for s in positions:
                    p = qb * kbf_scr[s]
                    if s != tt:
                        p = p * jnp.exp2(b - b_scr[s]).astype(_BF)
                    prods.append(p)
                pcat = jnp.concatenate(prods, axis=1) if len(prods) > 1 else prods[0]
                r0 = lg * GLA_BLK * LANES + positions[0] * LANES
                racc_scr[tt] += _dot(pcat, sel_ref[d, r0:r0 + len(positions) * LANES, :])
    for tt in range(GLA_BLK):
        rtm_scr[pl.ds(tt, nb, stride=GLA_BLK), :] = racc_scr[tt]
    r_ref[...] = rtm_scr[...].astype(_BF)
    for gi in range(4):
        qe_ref[:, gi * LANES:(gi + 1) * LANES] = qe_scr[gi].astype(_BF)
        ke_ref[:, gi * LANES:(gi + 1) * LANES] = ke_scr[gi].astype(_BF)
        dec_ref[:, gi * LANES:(gi + 1) * LANES] = dec_scr[gi]


def _gla_prep(gq, gk, gz, w2s, bias, sel):
    m = gq.shape[0]
    tt = min(PREP_TILE, m)
    assert m % tt == 0 and tt % (8 * GLA_BLK) == 0
    nb = tt // GLA_BLK
    col = lambda j: pl.BlockSpec((tt, LANES), lambda i: (i, j))
    return pl.pallas_call(
        _gla_prep_kernel,
        grid=(m // tt,),
        in_specs=[col(0), col(1), col(0), col(1), col(0),
                  _const_spec(w2s.shape), _const_spec(bias.shape), _const_spec(sel.shape)],
        out_specs=[pl.BlockSpec((tt, 4 * LANES), lambda i: (i, 0)),
                   pl.BlockSpec((tt, 4 * LANES), lambda i: (i, 0)),
                   pl.BlockSpec((nb, 4 * LANES), lambda i: (i, 0)),
                   pl.BlockSpec((tt, LANES), lambda i: (i, 0))],
        out_shape=[jax.ShapeDtypeStruct((m, 4 * LANES), _BF),
                   jax.ShapeDtypeStruct((m, 4 * LANES), _BF),
                   jax.ShapeDtypeStruct((m // GLA_BLK, 4 * LANES), _F32),
                   jax.ShapeDtypeStruct((m, LANES), _BF)],
        scratch_shapes=[pltpu.VMEM((4, tt, LANES), _F32),
                        pltpu.VMEM((GLA_BLK, nb, LANES), _F32),
                        pltpu.VMEM((GLA_BLK, nb, LANES), _BF),
                        pltpu.VMEM((4, tt, LANES), _F32),
                        pltpu.VMEM((4, tt, LANES), _F32),
                        pltpu.VMEM((GLA_BLK, nb, LANES), _F32),
                        pltpu.VMEM((tt, LANES), _F32),
                        pltpu.VMEM((4, nb, LANES), _F32)],
        compiler_params=pltpu.CompilerParams(dimension_semantics=("arbitrary",), vmem_limit_bytes=VMEM_LIMIT),
        name="gla_prep",
    )(gq, gq, gk, gk, gz, w2s, bias, sel)


def _gla_meta_state_kernel(k_ref, z_ref, v_ref, w2_ref, bias_ref, s0_ref):
    g = _gla_log2_gates(z_ref[...], w2_ref, bias_ref)[:, :GLA_QK_WIDTH]
    r = lax.broadcasted_iota(jnp.int32, (LANES, LANES), 0)
    c = lax.broadcasted_iota(jnp.int32, (LANES, LANES), 1)
    tail = jnp.where((c > r) & (c < N_META), 1.0, 0.0).astype(_BF)
    gh, gm, gl = _split3(g)
    rest = _dot(tail, gh) + _dot(tail, gm) + _dot(tail, gl)
    is_meta = lax.broadcasted_iota(jnp.int32, (LANES, GLA_QK_WIDTH), 0) < N_META
    ke = jnp.where(is_meta, k_ref[...] * jnp.exp2(jnp.where(is_meta, rest, 0.0)), 0.0).astype(_BF)
    lane = lax.broadcasted_iota(jnp.int32, (GLA_DV, GLA_QK_WIDTH), 1)
    v = v_ref[...].astype(_F32)
    s0 = jnp.zeros((GLA_DV, GLA_QK_WIDTH), _F32)
    for h in range(GLA_HEADS):
        vt = v[:, h * GLA_DV:(h + 1) * GLA_DV].T.astype(_BF)
        in_head = (lane >= h * GLA_DK) & (lane < (h + 1) * GLA_DK)
        s0 = s0 + jnp.where(in_head, _dot(vt, ke), 0.0)
    s0_ref[...] = s0


def _gla_meta_state(gk_m, gz_m, gv_m, w2s, bias):
    return pl.pallas_call(
        _gla_meta_state_kernel,
        out_shape=jax.ShapeDtypeStruct((GLA_DV, GLA_QK_WIDTH), _F32),
        name="gla_meta_state",
    )(gk_m, gz_m, gv_m, w2s, bias)


def _gla_scan_kernel(qf_ref, qb_ref, rf_ref, rb_ref, kf_ref, kb_ref, df_ref, db_ref, vf_ref, vb_ref, s0_ref,
                     of_ref, ob_ref, sf_scr, sb_scr):
    @pl.when(pl.program_id(1) == 0)
    def _():
        sf_scr[...] = s0_ref[...]
        sb_scr[...] = jnp.zeros_like(sb_scr)

    def lane_band(shape, width, n):
        lane = lax.broadcasted_iota(jnp.int32, shape, 1)
        return [(lane >= c * width) & (lane < (c + 1) * width) for c in range(n)]

    head_lanes = lane_band((GLA_BLK, GLA_QK_WIDTH), GLA_DK, GLA_HEADS)
    a_lanes = lane_band((GLA_BLK, LANES), GLA_BLK, N_DIR * GLA_HEADS)

    def head_stack(x, bands):
        return jnp.concatenate([jnp.where(bands[h], x, jnp.zeros_like(x)) for h in range(GLA_HEADS)], axis=0)

    dirs = ((qf_ref, rf_ref, kf_ref, df_ref, vf_ref, of_ref, sf_scr),
            (qb_ref, rb_ref, kb_ref, db_ref, vb_ref, ob_ref, sb_scr))
    for step in range(BLK_PER_GROUP):
        for d, (q_ref, r_ref, k_ref, d_ref, v_ref, o_ref, s_scr) in enumerate(dirs):
            j = step if d == 0 else BLK_PER_GROUP - 1 - step
            rows = slice(j * GLA_BLK, (j + 1) * GLA_BLK)
            state_t = s_scr[...]
            lhs1 = head_stack(q_ref[rows, :], head_lanes)
            lhs2 = head_stack(r_ref[rows, :], a_lanes[d * GLA_HEADS:(d + 1) * GLA_HEADS])
            vblk = v_ref[rows, :]
            vstack = jnp.concatenate([vblk[:, h * GLA_DV:(h + 1) * GLA_DV] for h in range(GLA_HEADS)], axis=0)
            vv = jnp.concatenate([vstack, vstack], axis=0)
            o = _dot_nt(lhs1, state_t.astype(_BF)) + _dot(lhs2, vv)
            for h in range(GLA_HEADS):
                o_ref[h, rows, :] = o[h * GLA_BLK:(h + 1) * GLA_BLK, :]
            kbd = head_stack(k_ref[rows, :], head_lanes)
            vst = vstack.astype(_F32).T.astype(_BF)
            s_scr[...] = state_t * d_ref[j:j + 1, :] + _dot(vst, kbd)


def _gla_scan(qe, r, ke, dec, gv, s0, batch, seq):
    m = qe.shape[0]
    ng = seq // GLA_GROUP
    fwd = lambda b, i: b * ng + i
    bwd = lambda b, i: b * ng + (ng - 1 - i)
    o_shape = jax.ShapeDtypeStruct((GLA_HEADS, m, GLA_DV), _F32)
    return pl.pallas_call(
        _gla_scan_kernel,
        grid=(batch, ng),
        in_specs=[pl.BlockSpec((GLA_GROUP, GLA_QK_WIDTH), lambda b, i: (fwd(b, i), 0)),
                  pl.BlockSpec((GLA_GROUP, GLA_QK_WIDTH), lambda b, i: (bwd(b, i), 1)),
                  pl.BlockSpec((GLA_GROUP, LANES), lambda b, i: (fwd(b, i), 0)),
                  pl.BlockSpec((GLA_GROUP, LANES), lambda b, i: (bwd(b, i), 0)),
                  pl.BlockSpec((GLA_GROUP, GLA_QK_WIDTH), lambda b, i: (fwd(b, i), 0)),
                  pl.BlockSpec((GLA_GROUP, GLA_QK_WIDTH), lambda b, i: (bwd(b, i), 1)),
                  pl.BlockSpec((BLK_PER_GROUP, GLA_QK_WIDTH), lambda b, i: (fwd(b, i), 0)),
                  pl.BlockSpec((BLK_PER_GROUP, GLA_QK_WIDTH), lambda b, i: (bwd(b, i), 1)),
                  pl.BlockSpec((GLA_GROUP, GLA_WIDTH), lambda b, i: (fwd(b, i), 0)),
                  pl.BlockSpec((GLA_GROUP, GLA_WIDTH), lambda b, i: (bwd(b, i), 0)),
                  pl.BlockSpec((GLA_DV, GLA_QK_WIDTH), lambda b, i: (0, 0))],
        out_specs=[pl.BlockSpec((GLA_HEADS, GLA_GROUP, GLA_DV), lambda b, i: (0, fwd(b, i), 0)),
                   pl.BlockSpec((GLA_HEADS, GLA_GROUP, GLA_DV), lambda b, i: (0, bwd(b, i), 0))],
        out_shape=[o_shape, o_shape],
        scratch_shapes=[pltpu.VMEM((GLA_DV, GLA_QK_WIDTH), _F32), pltpu.VMEM((GLA_DV, GLA_QK_WIDTH), _F32)],
        compiler_params=pltpu.CompilerParams(dimension_semantics=("arbitrary", "arbitrary"),
                                             vmem_limit_bytes=VMEM_LIMIT),
        name="gla_scan",
    )(qe, qe, r, r, ke, ke, dec, dec, gv, gv, s0)


def _stage_c_kernel(h1_ref, oatt_ref, of_ref, ob_ref, gg_ref, ng_ref, wout_ref, gm_ref, bm_ref,
                    wg_ref, wu_ref, wd_ref, g2_ref, b2_ref, out_ref, act_ref):
    gg = gg_ref[...]
    pieces = []
    for h in range(GLA_HEADS):
        o = of_ref[h] + ob_ref[h]
        o = o * lax.rsqrt(jnp.mean(o * o, axis=-1, keepdims=True) + LN_EPS) * ng_ref[...]
        pieces.append((o * _silu(gg[:, h * GLA_DV:(h + 1) * GLA_DV])).astype(_BF))
    mix = jnp.concatenate([oatt_ref[...]] + pieces, axis=1)
    h2 = _layer_norm(DN_ALPHA * h1_ref[...] + _dot(mix, wout_ref[...]), gm_ref[...], bm_ref[...])
    out_ref[...] = _swiglu_ln(h2, wg_ref, wu_ref, wd_ref, g2_ref, b2_ref, act_ref)


def _stage_c(h1, oatt, o_f, o_b, gg, ng, wout, gm, bm, wg, wu, wd, g2, b2):
    m, d = h1.shape
    tm = min(ROW_TILE, m)
    assert m % tm == 0
    n_ff = wg.shape[1]
    row = lambda w: pl.BlockSpec((tm, w), lambda i: (i, 0))
    hrow = pl.BlockSpec((GLA_HEADS, tm, GLA_DV), lambda i: (0, i, 0))
    return pl.pallas_call(
        _stage_c_kernel,
        grid=(m // tm,),
        in_specs=[row(d), row(ATT_WIDTH), hrow, hrow, row(GLA_WIDTH), _const_spec((1, GLA_DV)),
                  _const_spec(wout.shape), _const_spec((1, d)), _const_spec((1, d)),
                  _const_spec((d, n_ff)), _const_spec((d, n_ff)), _const_spec((n_ff, d)),
                  _const_spec((1, d)), _const_spec((1, d))],
        out_specs=row(d),
        out_shape=jax.ShapeDtypeStruct((m, d), _F32),
        scratch_shapes=[pltpu.VMEM((tm, n_ff), _BF)],
        compiler_params=pltpu.CompilerParams(dimension_semantics=("arbitrary",), vmem_limit_bytes=VMEM_LIMIT),
        name="stage_c",
    )(h1, oatt, o_f, o_b, gg, ng, wout, gm, bm, wg, wu, wd, g2, b2)


def _projection_weight(w_in):
    aq = w_in[:, 0:512]
    ak = w_in[:, 512:640]
    av = w_in[:, 640:768]
    rest = w_in[:, 768:2304]
    gz = w_in[:, 2304:2336]

    def dup(w):
        return jnp.concatenate([w[:, 0:64], w[:, 0:64], w[:, 64:128], w[:, 64:128]], axis=1)

    gz_pad = jnp.pad(gz, ((0, 0), (0, LANES - gz.shape[1])))
    return jnp.concatenate([aq, dup(ak), dup(av), rest, gz_pad], axis=1).astype(_BF)


def _gate_weight(w2, bias):
    wcat = jnp.zeros((LANES, N_DIR * GLA_QK_WIDTH), _F32)
    for n in range(N_DIR):
        wcat = wcat.at[n * GLA_RANK:(n + 1) * GLA_RANK, n * GLA_QK_WIDTH:(n + 1) * GLA_QK_WIDTH].set(w2[n])
    hi = wcat.astype(_BF)
    r1 = wcat - hi.astype(_F32)
    mid = r1.astype(_BF)
    lo = (r1 - mid.astype(_F32)).astype(_BF)
    return jnp.stack([hi, mid, lo]), bias.reshape(1, N_DIR * GLA_QK_WIDTH)


def _pad_rows(a, rows):
    return jnp.pad(a, ((0, rows - a.shape[0]), (0, 0)))


def kernel(x, meta_tokens, ln_in_g, ln_in_b, ffn1_ln_g, ffn1_ln_b, ffn1_w_gate, ffn1_w_up, ffn1_w_down, w_in,
           attn_sink, gla_gate_w2, gla_gate_b, gla_norm_g, w_out, mix_ln_g, mix_ln_b, ffn2_w_gate, ffn2_w_up,
           ffn2_w_down, ffn2_ln_g, ffn2_ln_b):
    batch, seq, d = x.shape
    assert seq % BLOCK == 0 and seq >= _KWIN and seq % GLA_GROUP == 0
    vec = lambda a: a.reshape(1, -1)
    x2d = x.reshape(batch * seq, d)
    wg1, wu1, wd1 = ffn1_w_gate[0].astype(_BF), ffn1_w_up[0].astype(_BF), ffn1_w_down[0].astype(_BF)
    wg2, wu2, wd2 = ffn2_w_gate[0].astype(_BF), ffn2_w_up[0].astype(_BF), ffn2_w_down[0].astype(_BF)
    win = _projection_weight(w_in[0])
    w2s, gbias = _gate_weight(gla_gate_w2[0], gla_gate_b[0])
    sel = jnp.asarray(_sel_matrix(), _BF)

    a_args = (vec(ln_in_g), vec(ln_in_b), wg1, wu1, wd1, vec(ffn1_ln_g[0]), vec(ffn1_ln_b[0]), win)
    h1, aq, ak2, av2, gq, gk, gv, gg, gz = _stage_a(x2d, *a_args)
    _, _, ak2_m, av2_m, _, gk_m, gv_m, _, gz_m = _stage_a(meta_tokens.astype(x.dtype), *a_args)

    o_att = _attention(aq, ak2, av2, _pad_rows(ak2_m, LANES), _pad_rows(av2_m, LANES),
                       attn_sink[0].astype(_F32), batch, seq)

    qe, ke, dec, r = _gla_prep(gq, gk, gz, w2s, gbias, sel)
    s0 = _gla_meta_state(_pad_rows(gk_m, LANES), _pad_rows(gz_m, LANES), _pad_rows(gv_m, LANES), w2s, gbias)
    o_f, o_b = _gla_scan(qe, r, ke, dec, gv, s0, batch, seq)

    out = _stage_c(h1, o_att, o_f, o_b, gg, vec(gla_norm_g[0]), w_out[0].astype(_BF),
                   vec(mix_ln_g[0]), vec(mix_ln_b[0]), wg2, wu2, wd2, vec(ffn2_ln_g[0]), vec(ffn2_ln_b[0]))
    return out.reshape(batch, seq, d)
```

```python
import functools

import numpy as np
import jax
import jax.numpy as jnp
from jax import lax
from jax.experimental import pallas as pl
from jax.experimental.pallas import tpu as pltpu

N_META = 16
ATT_HEADS = 8
ATT_KV_HEADS = 2
ATT_HEAD_DIM = 64
ATT_REP = ATT_HEADS // ATT_KV_HEADS
WINDOW = 128
BLOCK = 128
GLA_HEADS = 4
GLA_DK = 64
GLA_DV = 128
GLA_RANK = 16
GLA_TAU = 16.0
N_DIR = 2
ATT_WIDTH = ATT_HEADS * ATT_HEAD_DIM
ATT_KV_WIDTH = ATT_KV_HEADS * ATT_HEAD_DIM
GLA_QK_WIDTH = GLA_HEADS * GLA_DK
GLA_WIDTH = GLA_HEADS * GLA_DV
LN_EPS = 1e-5
DEPTH = 1
DN_ALPHA = (2.0 * DEPTH) ** 0.25

LANES = 128
GLA_BLK = 16
GLA_GROUP = 128
BLK_PER_GROUP = GLA_GROUP // GLA_BLK
SCAN_SEQS = 2
FF_CHUNK = 256
ROW_TILE = 512
ROW_SUB_TILES = 2
PREP_TILE = 1024
VMEM_LIMIT = 56 * 1024 * 1024

_BF = jnp.bfloat16
_F32 = jnp.float32


def _dot(a, b):
    return jnp.dot(a, b, preferred_element_type=_F32)


def _dot_nt(a, b):
    return lax.dot_general(a, b, (((1,), (1,)), ((), ())), preferred_element_type=_F32)


def _layer_norm(x, g, b):
    mu = jnp.mean(x, axis=-1, keepdims=True)
    xc = x - mu
    var = jnp.mean(xc * xc, axis=-1, keepdims=True)
    return xc * lax.rsqrt(var + LN_EPS) * g + b


def _silu(x):
    return x * (1.0 / (1.0 + jnp.exp(-x)))


def _split3(x):
    hi = x.astype(_BF)
    r1 = x - hi.astype(_F32)
    mid = r1.astype(_BF)
    lo = (r1 - mid.astype(_F32)).astype(_BF)
    return hi, mid, lo


def _sub_tiles(rows):
    n = ROW_SUB_TILES if rows % (ROW_SUB_TILES * 16) == 0 else 1
    return [pl.ds(s * (rows // n), rows // n) for s in range(n)]


def _swiglu_ln(h, wg_ref, wu_ref, wd_ref, g_ref, b_ref, act_ref):
    hb = h.astype(_BF)
    n_ff = wg_ref.shape[1]
    for c in range(n_ff // FF_CHUNK):
        sl = slice(c * FF_CHUNK, (c + 1) * FF_CHUNK)
        gate = _dot(hb, wg_ref[:, sl])
        up = _dot(hb, wu_ref[:, sl])
        act_ref[:, sl] = (_silu(gate) * up).astype(_BF)
    y = _dot(act_ref[...], wd_ref[...])
    return _layer_norm(DN_ALPHA * h + 0.5 * y, g_ref[...], b_ref[...])


_P_AQ = (0, 512)
_P_AK = (512, 768)
_P_AV = (768, 1024)
_P_GQ = (1024, 1280)
_P_GK = (1280, 1536)
_P_GV = (1536, 2048)
_P_GG = (2048, 2560)
_P_GZ = (2560, 2688)
_P_WIDTH = 2688


def _stage_a_kernel(x_ref, g0_ref, b0_ref, wg_ref, wu_ref, wd_ref, g1_ref, b1_ref, win_ref,
                    h1_ref, aq_ref, ak_ref, av_ref, gq_ref, gk_ref, gv_ref, gg_ref, gz_ref, act_ref):
    for rows in _sub_tiles(x_ref.shape[0]):
        h0 = _layer_norm(x_ref[rows, :], g0_ref[...], b0_ref[...])
        h1 = _swiglu_ln(h0, wg_ref, wu_ref, wd_ref, g1_ref, b1_ref, act_ref.at[rows])
        h1_ref[rows, :] = h1
        hb = h1.astype(_BF)

        def proj(cols):
            return _dot(hb, win_ref[:, cols[0]:cols[1]])

        aq_ref[rows, :] = (proj(_P_AQ) * (ATT_HEAD_DIM ** -0.5)).astype(_BF)
        ak_ref[rows, :] = proj(_P_AK).astype(_BF)
        av_ref[rows, :] = proj(_P_AV).astype(_BF)
        gq_ref[rows, :] = proj(_P_GQ) * (GLA_DK ** -0.5)
        gk_ref[rows, :] = proj(_P_GK)
        gv_ref[rows, :] = proj(_P_GV).astype(_BF)
        gg_ref[rows, :] = proj(_P_GG)
        gz_ref[rows, :] = proj(_P_GZ)


def _const_spec(shape):
    nd = len(shape)
    return pl.BlockSpec(shape, lambda *_: (0,) * nd, pipeline_mode=pl.Buffered(1))


def _stage_a(x2d, g0, b0, wg, wu, wd, g1, b1, win):
    m, d = x2d.shape
    tm = min(ROW_TILE, m)
    assert m % tm == 0
    n_ff = wg.shape[1]
    row = lambda w: pl.BlockSpec((tm, w), lambda i: (i, 0))
    out_widths = [(d, _F32), (512, _BF), (256, _BF), (256, _BF), (256, _F32), (256, _F32), (512, _BF), (512, _F32),
                  (128, _F32)]
    return pl.pallas_call(
        _stage_a_kernel,
        grid=(m // tm,),
        in_specs=[row(d), _const_spec((1, d)), _const_spec((1, d)), _const_spec((d, n_ff)), _const_spec((d, n_ff)),
                  _const_spec((n_ff, d)), _const_spec((1, d)), _const_spec((1, d)), _const_spec((d, _P_WIDTH))],
        out_specs=[row(w) for w, _ in out_widths],
        out_shape=[jax.ShapeDtypeStruct((m, w), dt) for w, dt in out_widths],
        scratch_shapes=[pltpu.VMEM((tm, n_ff), _BF)],
        compiler_params=pltpu.CompilerParams(dimension_semantics=("arbitrary",), vmem_limit_bytes=VMEM_LIMIT),
        name="stage_a",
    )(x2d, g0, b0, wg, wu, wd, g1, b1, win)


_KWIN = 3 * BLOCK


def _alibi_slopes():
    return [2.0 ** (-8.0 * (i + 1) / ATT_HEADS) for i in range(ATT_HEADS)]


_WIN_OFFSETS = (0, -BLOCK, -2 * BLOCK)
_Q_SPLIT = 2


def _attn_kernel(sink_ref, q_ref, k_ref, v_ref, km_ref, vm_ref, o_ref, band_scr, meta_scr):
    i = pl.program_id(1)
    s_len = k_ref.shape[0]
    nq = s_len // BLOCK

    qc = BLOCK // _Q_SPLIT

    @pl.when((pl.program_id(0) == 0) & (i == 0))
    def _():
        rel0 = (lax.broadcasted_iota(jnp.int32, (qc, _KWIN), 1) - lax.broadcasted_iota(jnp.int32, (qc, _KWIN), 0))
        lane = lax.broadcasted_iota(jnp.int32, (qc, LANES), 1)
        slopes = _alibi_slopes()
        for h in range(ATT_HEADS):
            g, hl = divmod(h, ATT_REP)
            rows = slice(hl * qc, (hl + 1) * qc)
            for var, off in enumerate(_WIN_OFFSETS):
                for part in range(_Q_SPLIT):
                    absrel = jnp.abs(rel0 + (off - part * qc)).astype(_F32)
                    band_scr[var, g, part, rows, :] = jnp.where(absrel <= float(WINDOW), -slopes[h] * absrel,
                                                                -jnp.inf)
            meta_scr[g, rows, :] = jnp.where(lane < N_META, 0.0, jnp.where(lane == N_META, sink_ref[h], -jnp.inf))

    start = jnp.clip((i - 1) * BLOCK, 0, s_len - _KWIN)
    start = pl.multiple_of(start, BLOCK)
    var = jnp.where(i == 0, 0, jnp.where(i == nq - 1, 2, 1))
    low_half = lax.broadcasted_iota(jnp.int32, (qc, LANES), 1) < ATT_HEAD_DIM
    zero = jnp.zeros((qc, LANES), _BF)

    chains = [(g, part) for g in range(ATT_KV_HEADS) for part in range(_Q_SPLIT)]

    def scores(g, part):
        lanes_g = slice(g * LANES, (g + 1) * LANES)
        qrows = slice(part * qc, (part + 1) * qc)
        qs = []
        for hl in range(ATT_REP):
            pair, e = divmod(g * ATT_REP + hl, 2)
            qp = q_ref[qrows, pair * LANES:(pair + 1) * LANES]
            qs.append(jnp.where(low_half if e == 0 else jnp.logical_not(low_half), qp, zero))
        qs = jnp.concatenate(qs, axis=0)
        kg = k_ref[pl.ds(start, _KWIN), lanes_g]
        s_band = _dot_nt(qs, kg) + band_scr[var, g, part]
        s_meta = _dot_nt(qs, km_ref[:, lanes_g]) + meta_scr[g]
        return s_band, s_meta

    def softmax(s_band, s_meta):
        mx = jnp.maximum(jnp.max(s_band, axis=-1, keepdims=True), jnp.max(s_meta, axis=-1, keepdims=True))
        p_band = jnp.exp(s_band - mx)
        p_meta = jnp.exp(s_meta - mx)
        den = jnp.sum(p_band, axis=-1, keepdims=True) + jnp.sum(p_meta, axis=-1, keepdims=True)
        return p_band.astype(_BF), p_meta.astype(_BF), den

    def values(g, part, p_band, p_meta, den):
        lanes_g = slice(g * LANES, (g + 1) * LANES)
        vg = v_ref[pl.ds(start, _KWIN), lanes_g]
        o = (_dot(p_band, vg) + _dot(p_meta, vm_ref[:, lanes_g])) * (1.0 / den)
        for pl_ in range(ATT_REP // 2):
            even = o[(2 * pl_) * qc:(2 * pl_ + 1) * qc, :]
            odd = o[(2 * pl_ + 1) * qc:(2 * pl_ + 2) * qc, :]
            pair = g * (ATT_REP // 2) + pl_
            o_ref[part * qc:(part + 1) * qc, pair * LANES:(pair + 1) * LANES] = (
                jnp.where(low_half, even, odd).astype(_BF))

    s_vals, p_vals = {}, {}
    for t in range(len(chains) + 2):
        if t < len(chains):
            s_vals[t] = scores(*chains[t])
        if 0 <= t - 1 < len(chains):
            p_vals[t - 1] = softmax(*s_vals.pop(t - 1))
        if 0 <= t - 2 < len(chains):
            values(*chains[t - 2], *p_vals.pop(t - 2))


def _attention(aq, ak2, av2, km2, vm2, sink, batch, seq):
    m = aq.shape[0]
    nq = seq // BLOCK
    return pl.pallas_call(
        _attn_kernel,
        grid=(batch, nq),
        in_specs=[pl.BlockSpec(memory_space=pltpu.SMEM),
                  pl.BlockSpec((BLOCK, ATT_WIDTH), lambda b, i: (b * nq + i, 0)),
                  pl.BlockSpec((seq, 2 * LANES), lambda b, i: (b, 0)),
                  pl.BlockSpec((seq, 2 * LANES), lambda b, i: (b, 0)),
                  pl.BlockSpec((LANES, 2 * LANES), lambda b, i: (0, 0)),
                  pl.BlockSpec((LANES, 2 * LANES), lambda b, i: (0, 0))],
        out_specs=pl.BlockSpec((BLOCK, ATT_WIDTH), lambda b, i: (b * nq + i, 0)),
        out_shape=jax.ShapeDtypeStruct((m, ATT_WIDTH), _BF),
        scratch_shapes=[pltpu.VMEM((len(_WIN_OFFSETS), ATT_KV_HEADS, _Q_SPLIT, ATT_REP * BLOCK // _Q_SPLIT, _KWIN),
                                   _F32),
                        pltpu.VMEM((ATT_KV_HEADS, ATT_REP * BLOCK // _Q_SPLIT, LANES), _F32)],
        compiler_params=pltpu.CompilerParams(dimension_semantics=("arbitrary", "arbitrary"),
                                             vmem_limit_bytes=VMEM_LIMIT),
        name="attention",
    )(sink, aq, ak2, av2, km2, vm2)


_LOG2E = 1.4426950408889634


def _gla_log2_gates(z, w2_ref, bias_ref):
    zh, zm, zl = _split3(z)
    w_hi, w_mid, w_lo = w2_ref[0], w2_ref[1], w2_ref[2]
    logits = (_dot(zh, w_hi) + (_dot(zh, w_mid) + _dot(zm, w_hi))
              + (_dot(zh, w_lo) + _dot(zm, w_mid) + _dot(zl, w_hi))) + bias_ref[...]
    log_sig = jnp.minimum(logits, 0.0) - jnp.log1p(jnp.exp(-jnp.abs(logits)))
    return log_sig * (_LOG2E / GLA_TAU)


def _sel_matrix():
    sel = np.zeros((N_DIR, 2, GLA_BLK, LANES, LANES), np.float32)
    for d in range(N_DIR):
        for lg in range(2):
            for s in range(GLA_BLK):
                for hl in range(2):
                    col = d * 64 + (2 * lg + hl) * GLA_BLK + s
                    sel[d, lg, s, hl * GLA_DK:(hl + 1) * GLA_DK, col] = 1.0
    return sel.reshape(N_DIR, 2 * GLA_BLK * LANES, LANES)


def _gla_prep_kernel(q0_ref, q1_ref, k0_ref, k1_ref, z_ref, w2_ref, bias_ref, sel_ref,
                     qk_ref, dec_ref, r_ref,
                     g_scr, b_scr, q_scr, k_scr, kbf_scr, qk_scr, racc_scr, rtm_scr, dec_scr):
    tt_rows = z_ref.shape[0]
    nb = tt_rows // GLA_BLK
    rc = min(512, tt_rows)
    for c in range(tt_rows // rc):
        rows = pl.ds(c * rc, rc)
        g = _gla_log2_gates(z_ref[rows, :], w2_ref, bias_ref)
        for gi in range(4):
            g_scr[gi, rows, :] = g[:, gi * LANES:(gi + 1) * LANES]

    def slab(ref, tt):
        return ref[pl.ds(tt, nb, stride=GLA_BLK), :]

    racc_scr[...] = jnp.zeros_like(racc_scr)
    for lg, (q_ref, k_ref) in enumerate(((q0_ref, k0_ref), (q1_ref, k1_ref))):
        for tt in range(GLA_BLK):
            q_scr[tt] = slab(q_ref, tt)
            k = slab(k_ref, tt)
            k_scr[tt] = k
            kbf_scr[tt] = k.astype(_BF)
        for d in range(N_DIR):
            gi = d * 2 + lg
            acc = None
            for tt in (range(GLA_BLK) if d == 0 else reversed(range(GLA_BLK))):
                gs = g_scr[gi, pl.ds(tt, nb, stride=GLA_BLK), :]
                acc = gs if acc is None else acc + gs
                b_scr[tt] = acc
            last = acc
            dec_scr[gi] = jnp.exp2(last)
            for tt in range(GLA_BLK):
                q = q_scr[tt]
                b = b_scr[tt]
                qk_scr[d * 4 + lg, pl.ds(tt, nb, stride=GLA_BLK), :] = q * jnp.exp2(b)
                qk_scr[d * 4 + 2 + lg, pl.ds(tt, nb, stride=GLA_BLK), :] = k_scr[tt] * jnp.exp2(last - b)
                positions = list(range(0, tt + 1)) if d == 0 else list(range(tt + 1, GLA_BLK))
                if not positions:
                    continue
                qb = q.astype(_BF)
                prods = []
                for s in positions:
                    p = qb * kbf_scr[s]
                    if s != tt:
                        p = p * jnp.exp2(b - b_scr[s]).astype(_BF)
                    prods.append(p)
                pcat = jnp.concatenate(prods, axis=1) if len(prods) > 1 else prods[0]
                r0 = lg * GLA_BLK * LANES + positions[0] * LANES
                racc_scr[tt] += _dot(pcat, sel_ref[d, r0:r0 + len(positions) * LANES, :])
    for tt in range(GLA_BLK):
        rtm_scr[pl.ds(tt, nb, stride=GLA_BLK), :] = racc_scr[tt]
    r_ref[...] = rtm_scr[...].astype(_BF)
    for c8 in range(8):
        qk_ref[:, c8 * LANES:(c8 + 1) * LANES] = qk_scr[c8].astype(_BF)
    for gi in range(4):
        dec_ref[:, gi * LANES:(gi + 1) * LANES] = dec_scr[gi]


def _gla_prep(gq, gk, gz, w2s, bias, sel):
    m = gq.shape[0]
    tt = min(PREP_TILE, m)
    assert m % tt == 0 and tt % (8 * GLA_BLK) == 0
    nb = tt // GLA_BLK
    col = lambda j: pl.BlockSpec((tt, LANES), lambda i: (i, j))
    return pl.pallas_call(
        _gla_prep_kernel,
        grid=(m // tt,),
        in_specs=[col(0), col(1), col(0), col(1), col(0),
                  _const_spec(w2s.shape), _const_spec(bias.shape), _const_spec(sel.shape)],
        out_specs=[pl.BlockSpec((tt, 8 * LANES), lambda i: (i, 0)),
                   pl.BlockSpec((nb, 4 * LANES), lambda i: (i, 0)),
                   pl.BlockSpec((tt, LANES), lambda i: (i, 0))],
        out_shape=[jax.ShapeDtypeStruct((m, 8 * LANES), _BF),
                   jax.ShapeDtypeStruct((m // GLA_BLK, 4 * LANES), _F32),
                   jax.ShapeDtypeStruct((m, LANES), _BF)],
        scratch_shapes=[pltpu.VMEM((4, tt, LANES), _F32),
                        pltpu.VMEM((GLA_BLK, nb, LANES), _F32),
                        pltpu.VMEM((GLA_BLK, nb, LANES), _F32),
                        pltpu.VMEM((GLA_BLK, nb, LANES), _F32),
                        pltpu.VMEM((GLA_BLK, nb, LANES), _BF),
                        pltpu.VMEM((8, tt, LANES), _F32),
                        pltpu.VMEM((GLA_BLK, nb, LANES), _F32),
                        pltpu.VMEM((tt, LANES), _F32),
                        pltpu.VMEM((4, nb, LANES), _F32)],
        compiler_params=pltpu.CompilerParams(dimension_semantics=("arbitrary",), vmem_limit_bytes=VMEM_LIMIT),
        name="gla_prep",
    )(gq, gq, gk, gk, gz, w2s, bias, sel)


def _gla_meta_state_kernel(k_ref, z_ref, v_ref, w2_ref, bias_ref, s0_ref):
    g = _gla_log2_gates(z_ref[...], w2_ref, bias_ref)[:, :GLA_QK_WIDTH]
    r = lax.broadcasted_iota(jnp.int32, (LANES, LANES), 0)
    c = lax.broadcasted_iota(jnp.int32, (LANES, LANES), 1)
    tail = jnp.where((c > r) & (c < N_META), 1.0, 0.0).astype(_BF)
    gh, gm, gl = _split3(g)
    rest = _dot(tail, gh) + _dot(tail, gm) + _dot(tail, gl)
    is_meta = lax.broadcasted_iota(jnp.int32, (LANES, GLA_QK_WIDTH), 0) < N_META
    ke = jnp.where(is_meta, k_ref[...] * jnp.exp2(jnp.where(is_meta, rest, 0.0)), 0.0).astype(_BF)
    lane = lax.broadcasted_iota(jnp.int32, (GLA_DV, GLA_QK_WIDTH), 1)
    v = v_ref[...].astype(_F32)
    s0 = jnp.zeros((GLA_DV, GLA_QK_WIDTH), _F32)
    for h in range(GLA_HEADS):
        vt = v[:, h * GLA_DV:(h + 1) * GLA_DV].T.astype(_BF)
        in_head = (lane >= h * GLA_DK) & (lane < (h + 1) * GLA_DK)
        s0 = s0 + jnp.where(in_head, _dot(vt, ke), 0.0)
    s0_ref[...] = s0


def _gla_meta_state(gk_m, gz_m, gv_m, w2s, bias):
    return pl.pallas_call(
        _gla_meta_state_kernel,
        out_shape=jax.ShapeDtypeStruct((GLA_DV, GLA_QK_WIDTH), _F32),
        name="gla_meta_state",
    )(gk_m, gz_m, gv_m, w2s, bias)


def _gla_scan_kernel(qkf_ref, qkb_ref, rf_ref, rb_ref, df_ref, db_ref, vf_ref, vb_ref, s0_ref,
                     of_ref, ob_ref, sf_scr, sb_scr):
    n_seq = sf_scr.shape[0]

    @pl.when(pl.program_id(1) == 0)
    def _():
        for b in range(n_seq):
            sf_scr[b] = s0_ref[...]
        sb_scr[...] = jnp.zeros_like(sb_scr)

    def lane_band(shape, width, n):
        lane = lax.broadcasted_iota(jnp.int32, shape, 1)
        return [(lane >= c * width) & (lane < (c + 1) * width) for c in range(n)]

    head_lanes = lane_band((GLA_BLK, GLA_QK_WIDTH), GLA_DK, GLA_HEADS)
    a_lanes = lane_band((GLA_BLK, LANES), GLA_BLK, N_DIR * GLA_HEADS)

    def head_stack(x, bands):
        return jnp.concatenate([jnp.where(bands[h], x, jnp.zeros_like(x)) for h in range(GLA_HEADS)], axis=0)

    dirs = ((qkf_ref, rf_ref, df_ref, vf_ref, of_ref, sf_scr),
            (qkb_ref, rb_ref, db_ref, vb_ref, ob_ref, sb_scr))
    for step in range(BLK_PER_GROUP):
        for b in range(n_seq):
            for d, (qk_ref, r_ref, d_ref, v_ref, o_ref, s_scr) in enumerate(dirs):
                j = step if d == 0 else BLK_PER_GROUP - 1 - step
                rows = slice(j * GLA_BLK, (j + 1) * GLA_BLK)
                state_t = s_scr[b]
                lhs1 = head_stack(qk_ref[b, rows, :GLA_QK_WIDTH], head_lanes)
                lhs2 = head_stack(r_ref[b, rows, :], a_lanes[d * GLA_HEADS:(d + 1) * GLA_HEADS])
                vblk = v_ref[b, rows, :]
                vstack = jnp.concatenate([vblk[:, h * GLA_DV:(h + 1) * GLA_DV] for h in range(GLA_HEADS)], axis=0)
                vv = jnp.concatenate([vstack, vstack], axis=0)
                o = _dot_nt(lhs1, state_t.astype(_BF)) + _dot(lhs2, vv)
                for h in range(GLA_HEADS):
                    o_ref[h, b, rows, :] = o[h * GLA_BLK:(h + 1) * GLA_BLK, :]
                kbd = head_stack(qk_ref[b, rows, GLA_QK_WIDTH:], head_lanes)
                vst = vstack.astype(_F32).T.astype(_BF)
                s_scr[b] = state_t * d_ref[b, j:j + 1, :] + _dot(vst, kbd)


def _gla_scan(qk, r, dec, gv, s0, batch, seq):
    ng = seq // GLA_GROUP
    nbs = SCAN_SEQS if batch % SCAN_SEQS == 0 else 1
    qk = qk.reshape(batch, seq, N_DIR * 2 * GLA_QK_WIDTH)
    r = r.reshape(batch, seq, LANES)
    dec = dec.reshape(batch, seq // GLA_BLK, N_DIR * GLA_QK_WIDTH)
    gv = gv.reshape(batch, seq, GLA_WIDTH)
    fwd = lambda i: i
    bwd = lambda i: ng - 1 - i
    o_shape = jax.ShapeDtypeStruct((GLA_HEADS, batch, seq, GLA_DV), _F32)

    def spec(rows, width, grp, col):
        return pl.BlockSpec((nbs, rows, width), lambda b, i: (b, grp(i), col))

    o_f, o_b = pl.pallas_call(
        _gla_scan_kernel,
        grid=(batch // nbs, ng),
        in_specs=[spec(GLA_GROUP, 2 * GLA_QK_WIDTH, fwd, 0), spec(GLA_GROUP, 2 * GLA_QK_WIDTH, bwd, 1),
                  spec(GLA_GROUP, LANES, fwd, 0), spec(GLA_GROUP, LANES, bwd, 0),
                  spec(BLK_PER_GROUP, GLA_QK_WIDTH, fwd, 0), spec(BLK_PER_GROUP, GLA_QK_WIDTH, bwd, 1),
                  spec(GLA_GROUP, GLA_WIDTH, fwd, 0), spec(GLA_GROUP, GLA_WIDTH, bwd, 0),
                  pl.BlockSpec((GLA_DV, GLA_QK_WIDTH), lambda b, i: (0, 0))],
        out_specs=[pl.BlockSpec((GLA_HEADS, nbs, GLA_GROUP, GLA_DV), lambda b, i: (0, b, fwd(i), 0)),
                   pl.BlockSpec((GLA_HEADS, nbs, GLA_GROUP, GLA_DV), lambda b, i: (0, b, bwd(i), 0))],
        out_shape=[o_shape, o_shape],
        scratch_shapes=[pltpu.VMEM((nbs, GLA_DV, GLA_QK_WIDTH), _F32), pltpu.VMEM((nbs, GLA_DV, GLA_QK_WIDTH), _F32)],
        compiler_params=pltpu.CompilerParams(dimension_semantics=("arbitrary", "arbitrary"),
                                             vmem_limit_bytes=VMEM_LIMIT),
        name="gla_scan",
    )(qk, qk, r, r, dec, dec, gv, gv, s0)
    return (o_f.reshape(GLA_HEADS, batch * seq, GLA_DV), o_b.reshape(GLA_HEADS, batch * seq, GLA_DV))


def _stage_c_kernel(h1_ref, oatt_ref, of_ref, ob_ref, gg_ref, ng_ref, wout_ref, gm_ref, bm_ref,
                    wg_ref, wu_ref, wd_ref, g2_ref, b2_ref, out_ref, act_ref):
    for rows in _sub_tiles(h1_ref.shape[0]):
        pieces = []
        for h in range(GLA_HEADS):
            o = of_ref[h, rows, :] + ob_ref[h, rows, :]
            o = o * lax.rsqrt(jnp.mean(o * o, axis=-1, keepdims=True) + LN_EPS) * ng_ref[...]
            pieces.append((o * _silu(gg_ref[rows, h * GLA_DV:(h + 1) * GLA_DV])).astype(_BF))
        mix = jnp.concatenate([oatt_ref[rows, :]] + pieces, axis=1)
        h2 = _layer_norm(DN_ALPHA * h1_ref[rows, :] + _dot(mix, wout_ref[...]), gm_ref[...], bm_ref[...])
        out_ref[rows, :] = _swiglu_ln(h2, wg_ref, wu_ref, wd_ref, g2_ref, b2_ref, act_ref.at[rows])


def _stage_c(h1, oatt, o_f, o_b, gg, ng, wout, gm, bm, wg, wu, wd, g2, b2):
    m, d = h1.shape
    tm = min(ROW_TILE, m)
    assert m % tm == 0
    n_ff = wg.shape[1]
    row = lambda w: pl.BlockSpec((tm, w), lambda i: (i, 0))
    hrow = pl.BlockSpec((GLA_HEADS, tm, GLA_DV), lambda i: (0, i, 0))
    return pl.pallas_call(
        _stage_c_kernel,
        grid=(m // tm,),
        in_specs=[row(d), row(ATT_WIDTH), hrow, hrow, row(GLA_WIDTH), _const_spec((1, GLA_DV)),
                  _const_spec(wout.shape), _const_spec((1, d)), _const_spec((1, d)),
                  _const_spec((d, n_ff)), _const_spec((d, n_ff)), _const_spec((n_ff, d)),
                  _const_spec((1, d)), _const_spec((1, d))],
        out_specs=row(d),
        out_shape=jax.ShapeDtypeStruct((m, d), _F32),
        scratch_shapes=[pltpu.VMEM((tm, n_ff), _BF)],
        compiler_params=pltpu.CompilerParams(dimension_semantics=("arbitrary",), vmem_limit_bytes=VMEM_LIMIT),
        name="stage_c",
    )(h1, oatt, o_f, o_b, gg, ng, wout, gm, bm, wg, wu, wd, g2, b2)


def _projection_weight(w_in):
    aq = w_in[:, 0:512]
    ak = w_in[:, 512:640]
    av = w_in[:, 640:768]
    rest = w_in[:, 768:2304]
    gz = w_in[:, 2304:2336]

    def dup(w):
        return jnp.concatenate([w[:, 0:64], w[:, 0:64], w[:, 64:128], w[:, 64:128]], axis=1)

    gz_pad = jnp.pad(gz, ((0, 0), (0, LANES - gz.shape[1])))
    return jnp.concatenate([aq, dup(ak), dup(av), rest, gz_pad], axis=1).astype(_BF)


def _gate_weight(w2, bias):
    wcat = jnp.zeros((LANES, N_DIR * GLA_QK_WIDTH), _F32)
    for n in range(N_DIR):
        wcat = wcat.at[n * GLA_RANK:(n + 1) * GLA_RANK, n * GLA_QK_WIDTH:(n + 1) * GLA_QK_WIDTH].set(w2[n])
    hi = wcat.astype(_BF)
    r1 = wcat - hi.astype(_F32)
    mid = r1.astype(_BF)
    lo = (r1 - mid.astype(_F32)).astype(_BF)
    return jnp.stack([hi, mid, lo]), bias.reshape(1, N_DIR * GLA_QK_WIDTH)


def _pad_rows(a, rows):
    return jnp.pad(a, ((0, rows - a.shape[0]), (0, 0)))


def kernel(x, meta_tokens, ln_in_g, ln_in_b, ffn1_ln_g, ffn1_ln_b, ffn1_w_gate, ffn1_w_up, ffn1_w_down, w_in,
           attn_sink, gla_gate_w2, gla_gate_b, gla_norm_g, w_out, mix_ln_g, mix_ln_b, ffn2_w_gate, ffn2_w_up,
           ffn2_w_down, ffn2_ln_g, ffn2_ln_b):
    batch, seq, d = x.shape
    assert seq % BLOCK == 0 and seq >= _KWIN and seq % GLA_GROUP == 0
    vec = lambda a: a.reshape(1, -1)
    x2d = x.reshape(batch * seq, d)
    wg1, wu1, wd1 = ffn1_w_gate[0].astype(_BF), ffn1_w_up[0].astype(_BF), ffn1_w_down[0].astype(_BF)
    wg2, wu2, wd2 = ffn2_w_gate[0].astype(_BF), ffn2_w_up[0].astype(_BF), ffn2_w_down[0].astype(_BF)
    win = _projection_weight(w_in[0])
    w2s, gbias = _gate_weight(gla_gate_w2[0], gla_gate_b[0])
    sel = jnp.asarray(_sel_matrix(), _BF)

    a_args = (vec(ln_in_g), vec(ln_in_b), wg1, wu1, wd1, vec(ffn1_ln_g[0]), vec(ffn1_ln_b[0]), win)
    h1, aq, ak2, av2, gq, gk, gv, gg, gz = _stage_a(x2d, *a_args)
    _, _, ak2_m, av2_m, _, gk_m, gv_m, _, gz_m = _stage_a(meta_tokens.astype(x.dtype), *a_args)

    o_att = _attention(aq, ak2, av2, _pad_rows(ak2_m, LANES), _pad_rows(av2_m, LANES),
                       attn_sink[0].astype(_F32), batch, seq)

    qk, dec, r = _gla_prep(gq, gk, gz, w2s, gbias, sel)
    s0 = _gla_meta_state(_pad_rows(gk_m, LANES), _pad_rows(gz_m, LANES), _pad_rows(gv_m, LANES), w2s, gbias)
    o_f, o_b = _gla_scan(qk, r, dec, gv, s0, batch, seq)

    out = _stage_c(h1, o_att, o_f, o_b, gg, vec(gla_norm_g[0]), w_out[0].astype(_BF),
                   vec(mix_ln_g[0]), vec(mix_ln_b[0]), wg2, wu2, wd2, vec(ffn2_ln_g[0]), vec(ffn2_ln_b[0]))
    return out.reshape(batch, seq, d)
```

```python
import functools

import numpy as np
import jax
import jax.numpy as jnp
from jax import lax
from jax.experimental import pallas as pl
from jax.experimental.pallas import tpu as pltpu

N_META = 16
ATT_HEADS = 8
ATT_KV_HEADS = 2
ATT_HEAD_DIM = 64
ATT_REP = ATT_HEADS // ATT_KV_HEADS
WINDOW = 128
BLOCK = 128
GLA_HEADS = 4
GLA_DK = 64
GLA_DV = 128
GLA_RANK = 16
GLA_TAU = 16.0
N_DIR = 2
ATT_WIDTH = ATT_HEADS * ATT_HEAD_DIM
ATT_KV_WIDTH = ATT_KV_HEADS * ATT_HEAD_DIM
GLA_QK_WIDTH = GLA_HEADS * GLA_DK
GLA_WIDTH = GLA_HEADS * GLA_DV
LN_EPS = 1e-5
DEPTH = 1
DN_ALPHA = (2.0 * DEPTH) ** 0.25

LANES = 128
GLA_BLK = 16
GLA_GROUP = 128
BLK_PER_GROUP = GLA_GROUP // GLA_BLK
SCAN_SEQS = 4
FF_CHUNK = 256
ROW_TILE = 512
ROW_SUB_TILES = 2
SUB_TILE_SKEW = 3
PM_UNIT = ROW_TILE // ROW_SUB_TILES
PREP_TILE = 1024
VMEM_LIMIT = 56 * 1024 * 1024

_BF = jnp.bfloat16
_F32 = jnp.float32


def _dot(a, b):
    return jnp.dot(a, b, preferred_element_type=_F32)


def _dot_nt(a, b):
    return lax.dot_general(a, b, (((1,), (1,)), ((), ())), preferred_element_type=_F32)


def _layer_norm(x, g, b):
    mu = jnp.mean(x, axis=-1, keepdims=True)
    xc = x - mu
    var = jnp.mean(xc * xc, axis=-1, keepdims=True)
    return xc * lax.rsqrt(var + LN_EPS) * g + b


def _silu(x):
    return x * (1.0 / (1.0 + jnp.exp(-x)))


def _split3(x):
    hi = x.astype(_BF)
    r1 = x - hi.astype(_F32)
    mid = r1.astype(_BF)
    lo = (r1 - mid.astype(_F32)).astype(_BF)
    return hi, mid, lo


def _sub_tiles(rows):
    n = ROW_SUB_TILES if rows % (ROW_SUB_TILES * 16) == 0 else 1
    return [pl.ds(s * (rows // n), rows // n) for s in range(n)]


def _run_skewed(step_gens, skew):
    live = list(step_gens)
    for _ in range(skew if len(live) > 1 else 0):
        next(live[0])
    while live:
        for gen in list(live):
            try:
                next(gen)
            except StopIteration:
                live.remove(gen)


def _swiglu_ln_steps(h, wg_ref, wu_ref, wd_ref, g_ref, b_ref, act_ref):
    hb = h.astype(_BF)
    n_ff = wg_ref.shape[1]
    for c in range(n_ff // FF_CHUNK):
        sl = slice(c * FF_CHUNK, (c + 1) * FF_CHUNK)
        gate = _dot(hb, wg_ref[:, sl])
        up = _dot(hb, wu_ref[:, sl])
        act_ref[:, sl] = (_silu(gate) * up).astype(_BF)
        yield
    y = _dot(act_ref[...], wd_ref[...])
    return _layer_norm(DN_ALPHA * h + 0.5 * y, g_ref[...], b_ref[...])


_P_AQ = (0, 512)
_P_AK = (512, 768)
_P_AV = (768, 1024)
_P_GQ = (1024, 1280)
_P_GK = (1280, 1536)
_P_GV = (1536, 2048)
_P_GG = (2048, 2560)
_P_GZ = (2560, 2688)
_P_WIDTH = 2688


def _position_major_perm(rows):
    nblk = rows // GLA_BLK
    perm = np.zeros((rows, rows), np.float32)
    for n in range(nblk):
        for t in range(GLA_BLK):
            perm[t * nblk + n, n * GLA_BLK + t] = 1.0
    return perm


def _stage_a_kernel(x_ref, g0_ref, b0_ref, wg_ref, wu_ref, wd_ref, g1_ref, b1_ref, win_ref, perm_ref,
                    h1_ref, aq_ref, ak_ref, av_ref, gq_ref, gk_ref, gv_ref, gg_ref, gz_ref, act_ref):
    def steps(rows):
        h0 = _layer_norm(x_ref[rows, :], g0_ref[...], b0_ref[...])
        yield
        h1 = yield from _swiglu_ln_steps(h0, wg_ref, wu_ref, wd_ref, g1_ref, b1_ref, act_ref.at[rows])
        h1_ref[rows, :] = h1
        hb = h1.astype(_BF)
        hb_pm = _dot(perm_ref[...], hb).astype(_BF)
        yield

        def proj(act, cols):
            return _dot(act, win_ref[:, cols[0]:cols[1]])

        aq_ref[rows, :] = (proj(hb, _P_AQ) * (ATT_HEAD_DIM ** -0.5)).astype(_BF)
        ak_ref[rows, :] = proj(hb, _P_AK).astype(_BF)
        av_ref[rows, :] = proj(hb, _P_AV).astype(_BF)
        gq_ref[rows, :] = proj(hb_pm, _P_GQ) * (GLA_DK ** -0.5)
        yield
        gk_ref[rows, :] = proj(hb_pm, _P_GK)
        gv_ref[rows, :] = proj(hb, _P_GV).astype(_BF)
        gg_ref[rows, :] = proj(hb, _P_GG)
        gz_ref[rows, :] = proj(hb_pm, _P_GZ)

    _run_skewed([steps(rows) for rows in _sub_tiles(x_ref.shape[0])], SUB_TILE_SKEW)


def _const_spec(shape):
    nd = len(shape)
    return pl.BlockSpec(shape, lambda *_: (0,) * nd, pipeline_mode=pl.Buffered(1))


def _stage_a(x2d, g0, b0, wg, wu, wd, g1, b1, win):
    m, d = x2d.shape
    tm = min(ROW_TILE, m)
    assert m % tm == 0
    n_ff = wg.shape[1]
    sub = _sub_tiles(tm)[0].size
    perm = jnp.asarray(_position_major_perm(sub), _BF)
    row = lambda w: pl.BlockSpec((tm, w), lambda i: (i, 0))
    out_widths = [(d, _F32), (512, _BF), (256, _BF), (256, _BF), (256, _F32), (256, _F32), (512, _BF), (512, _F32),
                  (128, _F32)]
    return pl.pallas_call(
        _stage_a_kernel,
        grid=(m // tm,),
        in_specs=[row(d), _const_spec((1, d)), _const_spec((1, d)), _const_spec((d, n_ff)), _const_spec((d, n_ff)),
                  _const_spec((n_ff, d)), _const_spec((1, d)), _const_spec((1, d)), _const_spec((d, _P_WIDTH)),
                  _const_spec((sub, sub))],
        out_specs=[row(w) for w, _ in out_widths],
        out_shape=[jax.ShapeDtypeStruct((m, w), dt) for w, dt in out_widths],
        scratch_shapes=[pltpu.VMEM((tm, n_ff), _BF)],
        compiler_params=pltpu.CompilerParams(dimension_semantics=("arbitrary",), vmem_limit_bytes=VMEM_LIMIT),
        name="stage_a",
    )(x2d, g0, b0, wg, wu, wd, g1, b1, win, perm)


_KWIN = 3 * BLOCK


def _alibi_slopes():
    return [2.0 ** (-8.0 * (i + 1) / ATT_HEADS) for i in range(ATT_HEADS)]


_WIN_OFFSETS = (0, -BLOCK, -2 * BLOCK)
_Q_SPLIT = 2


def _attn_kernel(sink_ref, q_ref, k_ref, v_ref, km_ref, vm_ref, o_ref, band_scr, meta_scr):
    i = pl.program_id(1)
    s_len = k_ref.shape[0]
    nq = s_len // BLOCK

    qc = BLOCK // _Q_SPLIT

    @pl.when((pl.program_id(0) == 0) & (i == 0))
    def _():
        rel0 = (lax.broadcasted_iota(jnp.int32, (qc, _KWIN), 1) - lax.broadcasted_iota(jnp.int32, (qc, _KWIN), 0))
        lane = lax.broadcasted_iota(jnp.int32, (qc, LANES), 1)
        slopes = _alibi_slopes()
        for h in range(ATT_HEADS):
            g, hl = divmod(h, ATT_REP)
            rows = slice(hl * qc, (hl + 1) * qc)
            for var, off in enumerate(_WIN_OFFSETS):
                for part in range(_Q_SPLIT):
                    absrel = jnp.abs(rel0 + (off - part * qc)).astype(_F32)
                    band_scr[var, g, part, rows, :] = jnp.where(absrel <= float(WINDOW), -slopes[h] * absrel,
                                                                -jnp.inf)
            meta_scr[g, rows, :] = jnp.where(lane < N_META, 0.0, jnp.where(lane == N_META, sink_ref[h], -jnp.inf))

    start = jnp.clip((i - 1) * BLOCK, 0, s_len - _KWIN)
    start = pl.multiple_of(start, BLOCK)
    var = jnp.where(i == 0, 0, jnp.where(i == nq - 1, 2, 1))
    low_half = lax.broadcasted_iota(jnp.int32, (qc, LANES), 1) < ATT_HEAD_DIM
    zero = jnp.zeros((qc, LANES), _BF)

    chains = [(g, part) for g in range(ATT_KV_HEADS) for part in range(_Q_SPLIT)]

    def scores(g, part):
        lanes_g = slice(g * LANES, (g + 1) * LANES)
        qrows = slice(part * qc, (part + 1) * qc)
        qs = []
        for hl in range(ATT_REP):
            pair, e = divmod(g * ATT_REP + hl, 2)
            qp = q_ref[qrows, pair * LANES:(pair + 1) * LANES]
            qs.append(jnp.where(low_half if e == 0 else jnp.logical_not(low_half), qp, zero))
        qs = jnp.concatenate(qs, axis=0)
        kg = k_ref[pl.ds(start, _KWIN), lanes_g]
        s_band = _dot_nt(qs, kg) + band_scr[var, g, part]
        s_meta = _dot_nt(qs, km_ref[:, lanes_g]) + meta_scr[g]
        return s_band, s_meta

    def softmax(s_band, s_meta):
        mx = jnp.maximum(jnp.max(s_band, axis=-1, keepdims=True), jnp.max(s_meta, axis=-1, keepdims=True))
        p_band = jnp.exp(s_band - mx)
        p_meta = jnp.exp(s_meta - mx)
        den = jnp.sum(p_band, axis=-1, keepdims=True) + jnp.sum(p_meta, axis=-1, keepdims=True)
        return p_band.astype(_BF), p_meta.astype(_BF), den

    def values(g, part, p_band, p_meta, den):
        lanes_g = slice(g * LANES, (g + 1) * LANES)
        vg = v_ref[pl.ds(start, _KWIN), lanes_g]
        o = (_dot(p_band, vg) + _dot(p_meta, vm_ref[:, lanes_g])) * (1.0 / den)
        for pl_ in range(ATT_REP // 2):
            even = o[(2 * pl_) * qc:(2 * pl_ + 1) * qc, :]
            odd = o[(2 * pl_ + 1) * qc:(2 * pl_ + 2) * qc, :]
            pair = g * (ATT_REP // 2) + pl_
            o_ref[part * qc:(part + 1) * qc, pair * LANES:(pair + 1) * LANES] = (
                jnp.where(low_half, even, odd).astype(_BF))

    s_vals, p_vals = {}, {}
    for t in range(len(chains) + 2):
        if t < len(chains):
            s_vals[t] = scores(*chains[t])
        if 0 <= t - 1 < len(chains):
            p_vals[t - 1] = softmax(*s_vals.pop(t - 1))
        if 0 <= t - 2 < len(chains):
            values(*chains[t - 2], *p_vals.pop(t - 2))


def _attention(aq, ak2, av2, km2, vm2, sink, batch, seq):
    m = aq.shape[0]
    nq = seq // BLOCK
    return pl.pallas_call(
        _attn_kernel,
        grid=(batch, nq),
        in_specs=[pl.BlockSpec(memory_space=pltpu.SMEM),
                  pl.BlockSpec((BLOCK, ATT_WIDTH), lambda b, i: (b * nq + i, 0)),
                  pl.BlockSpec((seq, 2 * LANES), lambda b, i: (b, 0)),
                  pl.BlockSpec((seq, 2 * LANES), lambda b, i: (b, 0)),
                  pl.BlockSpec((LANES, 2 * LANES), lambda b, i: (0, 0)),
                  pl.BlockSpec((LANES, 2 * LANES), lambda b, i: (0, 0))],
        out_specs=pl.BlockSpec((BLOCK, ATT_WIDTH), lambda b, i: (b * nq + i, 0)),
        out_shape=jax.ShapeDtypeStruct((m, ATT_WIDTH), _BF),
        scratch_shapes=[pltpu.VMEM((len(_WIN_OFFSETS), ATT_KV_HEADS, _Q_SPLIT, ATT_REP * BLOCK // _Q_SPLIT, _KWIN),
                                   _F32),
                        pltpu.VMEM((ATT_KV_HEADS, ATT_REP * BLOCK // _Q_SPLIT, LANES), _F32)],
        compiler_params=pltpu.CompilerParams(dimension_semantics=("arbitrary", "arbitrary"),
                                             vmem_limit_bytes=VMEM_LIMIT),
        name="attention",
    )(sink, aq, ak2, av2, km2, vm2)


_LOG2E = 1.4426950408889634


def _gla_log2_gates(z, w2_ref, bias_ref):
    zh = z.astype(_BF)
    zm = (z - zh.astype(_F32)).astype(_BF)
    lane = lax.broadcasted_iota(jnp.int32, z.shape, 1)
    mid_lanes = (lane >= N_DIR * GLA_RANK) & (lane < 2 * N_DIR * GLA_RANK)
    logits = _dot(jnp.where(mid_lanes, zm, zh), w2_ref[...]) + bias_ref[...]
    y = logits * _LOG2E
    log2_sig = jnp.minimum(y, 0.0) - jnp.log2(1.0 + jnp.exp2(-jnp.abs(y)))
    return log2_sig * (1.0 / GLA_TAU)


def _sel_matrix():
    sel = np.zeros((N_DIR, 2, GLA_BLK, LANES, LANES), np.float32)
    for d in range(N_DIR):
        for lg in range(2):
            for s in range(GLA_BLK):
                for hl in range(2):
                    col = d * 64 + (2 * lg + hl) * GLA_BLK + s
                    sel[d, lg, s, hl * GLA_DK:(hl + 1) * GLA_DK, col] = 1.0
    return sel.reshape(N_DIR, 2 * GLA_BLK * LANES, LANES)


def _gla_prep_kernel(q0_ref, q1_ref, k0_ref, k1_ref, z_ref, w2_ref, bias_ref, sel_ref, perm_ref,
                     qk_ref, dec_ref, r_ref,
                     g_scr, b_scr, kbf_scr, qk_scr, dec_scr):
    tt_rows = z_ref.shape[0]
    units = tt_rows // PM_UNIT
    ub = PM_UNIT // GLA_BLK
    rc = min(512, tt_rows)
    for c in range(tt_rows // rc):
        rows = pl.ds(c * rc, rc)
        g = _gla_log2_gates(z_ref[rows, :], w2_ref, bias_ref)
        for gi in range(4):
            g_scr[gi, rows, :] = g[:, gi * LANES:(gi + 1) * LANES]

    def slab(ref, tt, lead=()):
        return jnp.concatenate([ref[lead + (pl.ds(u * PM_UNIT + tt * ub, ub), slice(None))] for u in range(units)],
                               axis=0)

    def store_slab(ref, lead, tt, val):
        for u in range(units):
            ref[lead + (pl.ds(u * PM_UNIT + tt * ub, ub), slice(None))] = val[u * ub:(u + 1) * ub, :]

    qk_refs = ((q0_ref, k0_ref), (q1_ref, k1_ref))
    last = {}
    for lg, (_, k_ref) in enumerate(qk_refs):
        for tt in range(GLA_BLK):
            kbf_scr[lg, tt] = slab(k_ref, tt).astype(_BF)
        for d in range(N_DIR):
            gi = d * 2 + lg
            acc = None
            for tt in (range(GLA_BLK) if d == 0 else reversed(range(GLA_BLK))):
                gs = slab(g_scr, tt, (gi,))
                acc = gs if acc is None else acc + gs
                b_scr[gi, tt] = acc
            last[gi] = acc
            dec_scr[gi] = jnp.exp2(acc)
    for tt in range(GLA_BLK):
        prods, sel_rows = [], []
        for d in range(N_DIR):
            positions = list(range(0, tt + 1)) if d == 0 else list(range(tt + 1, GLA_BLK))
            for lg, (q_ref, k_ref) in enumerate(qk_refs):
                gi = d * 2 + lg
                q = slab(q_ref, tt)
                b = b_scr[gi, tt]
                store_slab(qk_scr, (d * 4 + lg,), tt, q * jnp.exp2(b))
                store_slab(qk_scr, (d * 4 + 2 + lg,), tt, slab(k_ref, tt) * jnp.exp2(last[gi] - b))
                if not positions:
                    continue
                qb = q.astype(_BF)
                for s in positions:
                    p = qb * kbf_scr[lg, s]
                    if s != tt:
                        p = p * jnp.exp2(b - b_scr[gi, s]).astype(_BF)
                    prods.append(p)
                r0 = lg * GLA_BLK * LANES + positions[0] * LANES
                sel_rows.append(sel_ref[d, r0:r0 + len(positions) * LANES, :])
        a_tt = _dot(jnp.concatenate(prods, axis=1), jnp.concatenate(sel_rows, axis=0))
        store_slab(qk_scr, (8,), tt, a_tt)
    for u in range(units):
        rows = pl.ds(u * PM_UNIT, PM_UNIT)
        x = jnp.concatenate([qk_scr[c, rows, :] for c in range(9)], axis=1).astype(_BF)
        y = _dot(perm_ref[...], x).astype(_BF)
        qk_ref[rows, :] = y[:, :8 * LANES]
        r_ref[rows, :] = y[:, 8 * LANES:]
    for gi in range(4):
        dec_ref[:, gi * LANES:(gi + 1) * LANES] = dec_scr[gi]


def _gla_prep(gq, gk, gz, w2s, bias, sel):
    m = gq.shape[0]
    tt = min(PREP_TILE, m)
    assert m % tt == 0 and tt % PM_UNIT == 0
    nb = tt // GLA_BLK
    perm = jnp.asarray(_position_major_perm(PM_UNIT), _BF)
    col = lambda j: pl.BlockSpec((tt, LANES), lambda i: (i, j))
    return pl.pallas_call(
        _gla_prep_kernel,
        grid=(m // tt,),
        in_specs=[col(0), col(1), col(0), col(1), col(0),
                  _const_spec(w2s.shape), _const_spec(bias.shape), _const_spec(sel.shape),
                  _const_spec((PM_UNIT, PM_UNIT))],
        out_specs=[pl.BlockSpec((tt, 8 * LANES), lambda i: (i, 0)),
                   pl.BlockSpec((nb, 4 * LANES), lambda i: (i, 0)),
                   pl.BlockSpec((tt, LANES), lambda i: (i, 0))],
        out_shape=[jax.ShapeDtypeStruct((m, 8 * LANES), _BF),
                   jax.ShapeDtypeStruct((m // GLA_BLK, 4 * LANES), _F32),
                   jax.ShapeDtypeStruct((m, LANES), _BF)],
        scratch_shapes=[pltpu.VMEM((4, tt, LANES), _F32),
                        pltpu.VMEM((4, GLA_BLK, nb, LANES), _F32),
                        pltpu.VMEM((2, GLA_BLK, nb, LANES), _BF),
                        pltpu.VMEM((9, tt, LANES), _F32),
                        pltpu.VMEM((4, nb, LANES), _F32)],
        compiler_params=pltpu.CompilerParams(dimension_semantics=("arbitrary",), vmem_limit_bytes=VMEM_LIMIT),
        name="gla_prep",
    )(gq, gq, gk, gk, gz, w2s, bias, sel, perm)


def _gla_meta_state_kernel(k_ref, z_ref, v_ref, w2_ref, bias_ref, s0_ref):
    g = _gla_log2_gates(z_ref[...], w2_ref, bias_ref)[:, :GLA_QK_WIDTH]
    r = lax.broadcasted_iota(jnp.int32, (LANES, LANES), 0)
    c = lax.broadcasted_iota(jnp.int32, (LANES, LANES), 1)
    tail = jnp.where((c > r) & (c < N_META), 1.0, 0.0).astype(_BF)
    gh, gm, gl = _split3(g)
    rest = _dot(tail, gh) + _dot(tail, gm) + _dot(tail, gl)
    is_meta = lax.broadcasted_iota(jnp.int32, (LANES, GLA_QK_WIDTH), 0) < N_META
    ke = jnp.where(is_meta, k_ref[...] * jnp.exp2(jnp.where(is_meta, rest, 0.0)), 0.0).astype(_BF)
    lane = lax.broadcasted_iota(jnp.int32, (GLA_DV, GLA_QK_WIDTH), 1)
    v = v_ref[...].astype(_F32)
    s0 = jnp.zeros((GLA_DV, GLA_QK_WIDTH), _F32)
    for h in range(GLA_HEADS):
        vt = v[:, h * GLA_DV:(h + 1) * GLA_DV].T.astype(_BF)
        in_head = (lane >= h * GLA_DK) & (lane < (h + 1) * GLA_DK)
        s0 = s0 + jnp.where(in_head, _dot(vt, ke), 0.0)
    s0_ref[...] = s0


def _gla_meta_state(gk_m, gz_m, gv_m, w2s, bias):
    return pl.pallas_call(
        _gla_meta_state_kernel,
        out_shape=jax.ShapeDtypeStruct((GLA_DV, GLA_QK_WIDTH), _F32),
        name="gla_meta_state",
    )(gk_m, gz_m, gv_m, w2s, bias)


def _gla_scan_kernel(qkf_ref, qkb_ref, rf_ref, rb_ref, df_ref, db_ref, vf_ref, vb_ref, s0_ref,
                     of_ref, ob_ref, sf_scr, sb_scr):
    n_seq = sf_scr.shape[0]

    @pl.when(pl.program_id(1) == 0)
    def _():
        for b in range(n_seq):
            sf_scr[b] = s0_ref[...]
        sb_scr[...] = jnp.zeros_like(sb_scr)

    def lane_band(shape, width, n):
        lane = lax.broadcasted_iota(jnp.int32, shape, 1)
        return [(lane >= c * width) & (lane < (c + 1) * width) for c in range(n)]

    head_lanes = lane_band((GLA_BLK, GLA_QK_WIDTH), GLA_DK, GLA_HEADS)
    a_lanes = lane_band((GLA_BLK, LANES), GLA_BLK, N_DIR * GLA_HEADS)

    def head_stack(x, bands):
        return jnp.concatenate([jnp.where(bands[h], x, jnp.zeros_like(x)) for h in range(GLA_HEADS)], axis=0)

    dirs = ((qkf_ref, rf_ref, df_ref, vf_ref, of_ref, sf_scr),
            (qkb_ref, rb_ref, db_ref, vb_ref, ob_ref, sb_scr))
    for step in range(BLK_PER_GROUP):
        for b in range(n_seq):
            for d, (qk_ref, r_ref, d_ref, v_ref, o_ref, s_scr) in enumerate(dirs):
                j = step if d == 0 else BLK_PER_GROUP - 1 - step
                rows = slice(j * GLA_BLK, (j + 1) * GLA_BLK)
                state_t = s_scr[b]
                lhs1 = head_stack(qk_ref[b, rows, :GLA_QK_WIDTH], head_lanes)
                lhs2 = head_stack(r_ref[b, rows, :], a_lanes[d * GLA_HEADS:(d + 1) * GLA_HEADS])
                vblk = v_ref[b, rows, :]
                vstack = jnp.concatenate([vblk[:, h * GLA_DV:(h + 1) * GLA_DV] for h in range(GLA_HEADS)], axis=0)
                vv = jnp.concatenate([vstack, vstack], axis=0)
                o = _dot_nt(lhs1, state_t.astype(_BF)) + _dot(lhs2, vv)
                for h in range(GLA_HEADS):
                    o_ref[h, b, rows, :] = o[h * GLA_BLK:(h + 1) * GLA_BLK, :]
                kbd = head_stack(qk_ref[b, rows, GLA_QK_WIDTH:], head_lanes)
                vst = vstack.astype(_F32).T.astype(_BF)
                s_scr[b] = state_t * d_ref[b, j:j + 1, :] + _dot(vst, kbd)


def _gla_scan(qk, r, dec, gv, s0, batch, seq):
    ng = seq // GLA_GROUP
    nbs = SCAN_SEQS if batch % SCAN_SEQS == 0 else 1
    qk = qk.reshape(batch, seq, N_DIR * 2 * GLA_QK_WIDTH)
    r = r.reshape(batch, seq, LANES)
    dec = dec.reshape(batch, seq // GLA_BLK, N_DIR * GLA_QK_WIDTH)
    gv = gv.reshape(batch, seq, GLA_WIDTH)
    fwd = lambda i: i
    bwd = lambda i: ng - 1 - i
    o_shape = jax.ShapeDtypeStruct((GLA_HEADS, batch, seq, GLA_DV), _F32)

    def spec(rows, width, grp, col):
        return pl.BlockSpec((nbs, rows, width), lambda b, i: (b, grp(i), col))

    o_f, o_b = pl.pallas_call(
        _gla_scan_kernel,
        grid=(batch // nbs, ng),
        in_specs=[spec(GLA_GROUP, 2 * GLA_QK_WIDTH, fwd, 0), spec(GLA_GROUP, 2 * GLA_QK_WIDTH, bwd, 1),
                  spec(GLA_GROUP, LANES, fwd, 0), spec(GLA_GROUP, LANES, bwd, 0),
                  spec(BLK_PER_GROUP, GLA_QK_WIDTH, fwd, 0), spec(BLK_PER_GROUP, GLA_QK_WIDTH, bwd, 1),
                  spec(GLA_GROUP, GLA_WIDTH, fwd, 0), spec(GLA_GROUP, GLA_WIDTH, bwd, 0),
                  pl.BlockSpec((GLA_DV, GLA_QK_WIDTH), lambda b, i: (0, 0))],
        out_specs=[pl.BlockSpec((GLA_HEADS, nbs, GLA_GROUP, GLA_DV), lambda b, i: (0, b, fwd(i), 0)),
                   pl.BlockSpec((GLA_HEADS, nbs, GLA_GROUP, GLA_DV), lambda b, i: (0, b, bwd(i), 0))],
        out_shape=[o_shape, o_shape],
        scratch_shapes=[pltpu.VMEM((nbs, GLA_DV, GLA_QK_WIDTH), _F32), pltpu.VMEM((nbs, GLA_DV, GLA_QK_WIDTH), _F32)],
        compiler_params=pltpu.CompilerParams(dimension_semantics=("arbitrary", "arbitrary"),
                                             vmem_limit_bytes=VMEM_LIMIT),
        name="gla_scan",
    )(qk, qk, r, r, dec, dec, gv, gv, s0)
    return (o_f.reshape(GLA_HEADS, batch * seq, GLA_DV), o_b.reshape(GLA_HEADS, batch * seq, GLA_DV))


def _stage_c_kernel(h1_ref, oatt_ref, of_ref, ob_ref, gg_ref, ng_ref, wout_ref, gm_ref, bm_ref,
                    wg_ref, wu_ref, wd_ref, g2_ref, b2_ref, out_ref, act_ref):
    def steps(rows):
        pieces = []
        for h in range(GLA_HEADS):
            o = of_ref[h, rows, :] + ob_ref[h, rows, :]
            o = o * lax.rsqrt(jnp.mean(o * o, axis=-1, keepdims=True) + LN_EPS) * ng_ref[...]
            pieces.append((o * _silu(gg_ref[rows, h * GLA_DV:(h + 1) * GLA_DV])).astype(_BF))
        mix = jnp.concatenate([oatt_ref[rows, :]] + pieces, axis=1)
        yield
        h2 = _layer_norm(DN_ALPHA * h1_ref[rows, :] + _dot(mix, wout_ref[...]), gm_ref[...], bm_ref[...])
        yield
        out_ref[rows, :] = yield from _swiglu_ln_steps(h2, wg_ref, wu_ref, wd_ref, g2_ref, b2_ref,
                                                       act_ref.at[rows])

    _run_skewed([steps(rows) for rows in _sub_tiles(h1_ref.shape[0])], SUB_TILE_SKEW)


def _stage_c(h1, oatt, o_f, o_b, gg, ng, wout, gm, bm, wg, wu, wd, g2, b2):
    m, d = h1.shape
    tm = min(ROW_TILE, m)
    assert m % tm == 0
    n_ff = wg.shape[1]
    row = lambda w: pl.BlockSpec((tm, w), lambda i: (i, 0))
    hrow = pl.BlockSpec((GLA_HEADS, tm, GLA_DV), lambda i: (0, i, 0))
    return pl.pallas_call(
        _stage_c_kernel,
        grid=(m // tm,),
        in_specs=[row(d), row(ATT_WIDTH), hrow, hrow, row(GLA_WIDTH), _const_spec((1, GLA_DV)),
                  _const_spec(wout.shape), _const_spec((1, d)), _const_spec((1, d)),
                  _const_spec((d, n_ff)), _const_spec((d, n_ff)), _const_spec((n_ff, d)),
                  _const_spec((1, d)), _const_spec((1, d))],
        out_specs=row(d),
        out_shape=jax.ShapeDtypeStruct((m, d), _F32),
        scratch_shapes=[pltpu.VMEM((tm, n_ff), _BF)],
        compiler_params=pltpu.CompilerParams(dimension_semantics=("arbitrary",), vmem_limit_bytes=VMEM_LIMIT),
        name="stage_c",
    )(h1, oatt, o_f, o_b, gg, ng, wout, gm, bm, wg, wu, wd, g2, b2)


def _projection_weight(w_in):
    aq = w_in[:, 0:512]
    ak = w_in[:, 512:640]
    av = w_in[:, 640:768]
    rest = w_in[:, 768:2304]
    gz = w_in[:, 2304:2336]

    def dup(w):
        return jnp.concatenate([w[:, 0:64], w[:, 0:64], w[:, 64:128], w[:, 64:128]], axis=1)

    gz3 = jnp.pad(jnp.concatenate([gz, gz, gz], axis=1), ((0, 0), (0, LANES - 3 * gz.shape[1])))
    return jnp.concatenate([aq, dup(ak), dup(av), rest, gz3], axis=1).astype(_BF)


def _gate_weight(w2, bias):
    k = N_DIR * GLA_RANK
    wcat = jnp.zeros((k, N_DIR * GLA_QK_WIDTH), _F32)
    for n in range(N_DIR):
        wcat = wcat.at[n * GLA_RANK:(n + 1) * GLA_RANK, n * GLA_QK_WIDTH:(n + 1) * GLA_QK_WIDTH].set(w2[n])
    hi = wcat.astype(_BF)
    mid = (wcat - hi.astype(_F32)).astype(_BF)
    pad = jnp.zeros((LANES - 3 * k, N_DIR * GLA_QK_WIDTH), _BF)
    return jnp.concatenate([hi, hi, mid, pad], axis=0), bias.reshape(1, N_DIR * GLA_QK_WIDTH)


def _pad_rows(a, rows):
    return jnp.pad(a, ((0, rows - a.shape[0]), (0, 0)))


def kernel(x, meta_tokens, ln_in_g, ln_in_b, ffn1_ln_g, ffn1_ln_b, ffn1_w_gate, ffn1_w_up, ffn1_w_down, w_in,
           attn_sink, gla_gate_w2, gla_gate_b, gla_norm_g, w_out, mix_ln_g, mix_ln_b, ffn2_w_gate, ffn2_w_up,
           ffn2_w_down, ffn2_ln_g, ffn2_ln_b):
    batch, seq, d = x.shape
    assert seq % BLOCK == 0 and seq >= _KWIN and seq % GLA_GROUP == 0
    vec = lambda a: a.reshape(1, -1)
    x2d = x.reshape(batch * seq, d)
    wg1, wu1, wd1 = ffn1_w_gate[0].astype(_BF), ffn1_w_up[0].astype(_BF), ffn1_w_down[0].astype(_BF)
    wg2, wu2, wd2 = ffn2_w_gate[0].astype(_BF), ffn2_w_up[0].astype(_BF), ffn2_w_down[0].astype(_BF)
    win = _projection_weight(w_in[0])
    w2s, gbias = _gate_weight(gla_gate_w2[0], gla_gate_b[0])
    sel = jnp.asarray(_sel_matrix(), _BF)

    a_args = (vec(ln_in_g), vec(ln_in_b), wg1, wu1, wd1, vec(ffn1_ln_g[0]), vec(ffn1_ln_b[0]), win)
    h1, aq, ak2, av2, gq, gk, gv, gg, gz = _stage_a(x2d, *a_args)
    _, _, ak2_m, av2_m, _, gk_m, gv_m, _, gz_m = _stage_a(meta_tokens.astype(x.dtype), *a_args)

    o_att = _attention(aq, ak2, av2, _pad_rows(ak2_m, LANES), _pad_rows(av2_m, LANES),
                       attn_sink[0].astype(_F32), batch, seq)

    qk, dec, r = _gla_prep(gq, gk, gz, w2s, gbias, sel)
    s0 = _gla_meta_state(_pad_rows(gk_m, LANES), _pad_rows(gz_m, LANES), _pad_rows(gv_m, LANES), w2s, gbias)
    o_f, o_b = _gla_scan(qk, r, dec, gv, s0, batch, seq)

    out = _stage_c(h1, o_att, o_f, o_b, gg, vec(gla_norm_g[0]), w_out[0].astype(_BF),
                   vec(mix_ln_g[0]), vec(mix_ln_b[0]), wg2, wu2, wd2, vec(ffn2_ln_g[0]), vec(ffn2_ln_b[0]))
    return out.reshape(batch, seq, d)
```

```python
import functools

import numpy as np
import jax
import jax.numpy as jnp
from jax import lax
from jax.experimental import pallas as pl
from jax.experimental.pallas import tpu as pltpu

N_META = 16
ATT_HEADS = 8
ATT_KV_HEADS = 2
ATT_HEAD_DIM = 64
ATT_REP = ATT_HEADS // ATT_KV_HEADS
WINDOW = 128
BLOCK = 128
GLA_HEADS = 4
GLA_DK = 64
GLA_DV = 128
GLA_RANK = 16
GLA_TAU = 16.0
N_DIR = 2
ATT_WIDTH = ATT_HEADS * ATT_HEAD_DIM
ATT_KV_WIDTH = ATT_KV_HEADS * ATT_HEAD_DIM
GLA_QK_WIDTH = GLA_HEADS * GLA_DK
GLA_WIDTH = GLA_HEADS * GLA_DV
LN_EPS = 1e-5
DEPTH = 1
DN_ALPHA = (2.0 * DEPTH) ** 0.25

LANES = 128
GLA_BLK = 16
GLA_GROUP = 128
BLK_PER_GROUP = GLA_GROUP // GLA_BLK
SCAN_SEQS = 4
FF_CHUNK = 256
ROW_TILE = 512
ROW_SUB_TILES = 2
SUB_TILE_SKEW = 3
PM_UNIT = ROW_TILE // ROW_SUB_TILES
PREP_TILE = 1024
VMEM_LIMIT = 56 * 1024 * 1024

_BF = jnp.bfloat16
_F32 = jnp.float32


def _dot(a, b):
    return jnp.dot(a, b, preferred_element_type=_F32)


def _dot_nt(a, b):
    return lax.dot_general(a, b, (((1,), (1,)), ((), ())), preferred_element_type=_F32)


def _layer_norm(x, g, b):
    mu = jnp.mean(x, axis=-1, keepdims=True)
    xc = x - mu
    var = jnp.mean(xc * xc, axis=-1, keepdims=True)
    return xc * lax.rsqrt(var + LN_EPS) * g + b


def _silu(x):
    return x * (1.0 / (1.0 + jnp.exp(-x)))


def _split3(x):
    hi = x.astype(_BF)
    r1 = x - hi.astype(_F32)
    mid = r1.astype(_BF)
    lo = (r1 - mid.astype(_F32)).astype(_BF)
    return hi, mid, lo


def _sub_tiles(rows):
    n = ROW_SUB_TILES if rows % (ROW_SUB_TILES * 16) == 0 else 1
    return [pl.ds(s * (rows // n), rows // n) for s in range(n)]


def _run_skewed(step_gens, skew):
    live = list(step_gens)
    for _ in range(skew if len(live) > 1 else 0):
        next(live[0])
    while live:
        for gen in list(live):
            try:
                next(gen)
            except StopIteration:
                live.remove(gen)


def _swiglu_ln_steps(h, wg_ref, wu_ref, wd_ref, g_ref, b_ref, act_ref):
    hb = h.astype(_BF)
    n_ff = wg_ref.shape[1]
    for c in range(n_ff // FF_CHUNK):
        sl = slice(c * FF_CHUNK, (c + 1) * FF_CHUNK)
        gate = _dot(hb, wg_ref[:, sl])
        up = _dot(hb, wu_ref[:, sl])
        act_ref[:, sl] = (_silu(gate) * up).astype(_BF)
        yield
    y = _dot(act_ref[...], wd_ref[...])
    return _layer_norm(DN_ALPHA * h + 0.5 * y, g_ref[...], b_ref[...])


_P_AQ = (0, 512)
_P_AK = (512, 768)
_P_AV = (768, 896)
_P_GQ = (896, 1152)
_P_GK = (1152, 1408)
_P_GV = (1408, 1920)
_P_GG = (1920, 2432)
_P_GZ = (2432, 2560)
_P_WIDTH = 2560


def _position_major_perm(rows):
    nblk = rows // GLA_BLK
    perm = np.zeros((rows, rows), np.float32)
    for n in range(nblk):
        for t in range(GLA_BLK):
            perm[t * nblk + n, n * GLA_BLK + t] = 1.0
    return perm


def _stage_a_kernel(x_ref, g0_ref, b0_ref, wg_ref, wu_ref, wd_ref, g1_ref, b1_ref, win_ref, perm_ref,
                    h1_ref, aq_ref, ak_ref, avt_ref, gq_ref, gk_ref, gv_ref, gg_ref, gz_ref, act_ref):
    def steps(rows):
        h0 = _layer_norm(x_ref[rows, :], g0_ref[...], b0_ref[...])
        yield
        h1 = yield from _swiglu_ln_steps(h0, wg_ref, wu_ref, wd_ref, g1_ref, b1_ref, act_ref.at[rows])
        h1_ref[rows, :] = h1
        hb = h1.astype(_BF)
        hb_pm = _dot(perm_ref[...], hb).astype(_BF)
        yield

        def proj(act, cols):
            return _dot(act, win_ref[:, cols[0]:cols[1]])

        aq_ref[rows, :] = (proj(hb, _P_AQ) * (ATT_HEAD_DIM ** -0.5)).astype(_BF)
        ak_ref[rows, :] = proj(hb, _P_AK).astype(_BF)
        avt_ref[:, rows] = proj(hb, _P_AV).T.astype(_BF)
        gq_ref[rows, :] = proj(hb_pm, _P_GQ) * (GLA_DK ** -0.5)
        yield
        gk_ref[rows, :] = proj(hb_pm, _P_GK)
        gv_ref[rows, :] = proj(hb, _P_GV).astype(_BF)
        gg_ref[rows, :] = proj(hb, _P_GG)
        gz_ref[rows, :] = proj(hb_pm, _P_GZ)

    _run_skewed([steps(rows) for rows in _sub_tiles(x_ref.shape[0])], SUB_TILE_SKEW)


def _const_spec(shape):
    nd = len(shape)
    return pl.BlockSpec(shape, lambda *_: (0,) * nd, pipeline_mode=pl.Buffered(1))


def _stage_a(x2d, g0, b0, wg, wu, wd, g1, b1, win):
    m, d = x2d.shape
    tm = min(ROW_TILE, m)
    assert m % tm == 0
    n_ff = wg.shape[1]
    sub = _sub_tiles(tm)[0].size
    perm = jnp.asarray(_position_major_perm(sub), _BF)
    row = lambda w: pl.BlockSpec((tm, w), lambda i: (i, 0))
    out_widths = [(d, _F32), (512, _BF), (256, _BF), None, (256, _F32), (256, _F32), (512, _BF), (512, _F32),
                  (128, _F32)]
    out_specs = [row(o[0]) if o else pl.BlockSpec((ATT_KV_WIDTH, tm), lambda i: (0, i)) for o in out_widths]
    out_shape = [jax.ShapeDtypeStruct((m, o[0]), o[1]) if o else jax.ShapeDtypeStruct((ATT_KV_WIDTH, m), _BF)
                 for o in out_widths]
    return pl.pallas_call(
        _stage_a_kernel,
        grid=(m // tm,),
        in_specs=[row(d), _const_spec((1, d)), _const_spec((1, d)), _const_spec((d, n_ff)), _const_spec((d, n_ff)),
                  _const_spec((n_ff, d)), _const_spec((1, d)), _const_spec((1, d)), _const_spec((d, _P_WIDTH)),
                  _const_spec((sub, sub))],
        out_specs=out_specs,
        out_shape=out_shape,
        scratch_shapes=[pltpu.VMEM((tm, n_ff), _BF)],
        compiler_params=pltpu.CompilerParams(dimension_semantics=("arbitrary",), vmem_limit_bytes=VMEM_LIMIT),
        name="stage_a",
    )(x2d, g0, b0, wg, wu, wd, g1, b1, win, perm)


_KWIN = 3 * BLOCK


def _alibi_slopes():
    return [2.0 ** (-8.0 * (i + 1) / ATT_HEADS) for i in range(ATT_HEADS)]


_WIN_OFFSETS = (0, -BLOCK, -2 * BLOCK)
ATT_QBLOCKS = 4


def _attn_kernel(sink_ref, q_ref, k_ref, vt_ref, km_ref, vmt_ref, o_ref, band_scr, meta_scr):
    i = pl.program_id(1)
    s_len = k_ref.shape[0]
    nq = s_len // BLOCK
    pairs_per_group = ATT_REP // 2

    @pl.when((pl.program_id(0) == 0) & (i == 0))
    def _():
        col = lax.broadcasted_iota(jnp.int32, (_KWIN, 2 * BLOCK), 1)
        rel0 = lax.broadcasted_iota(jnp.int32, (_KWIN, 2 * BLOCK), 0) - jnp.where(col < BLOCK, col, col - BLOCK)
        mrow = lax.broadcasted_iota(jnp.int32, (LANES, 2 * BLOCK), 0)
        mcol = lax.broadcasted_iota(jnp.int32, (LANES, 2 * BLOCK), 1)
        slopes = _alibi_slopes()
        for g in range(ATT_KV_HEADS):
            for pl_ in range(pairs_per_group):
                h0 = g * ATT_REP + 2 * pl_
                slope = jnp.where(col < BLOCK, slopes[h0], slopes[h0 + 1])
                for var, off in enumerate(_WIN_OFFSETS):
                    absrel = jnp.abs(rel0 + off).astype(_F32)
                    band_scr[var, g, pl_] = jnp.where(absrel <= float(WINDOW), -slope * absrel, -jnp.inf)
                sink = jnp.where(mcol < BLOCK, sink_ref[h0], sink_ref[h0 + 1])
                meta_scr[g, pl_] = jnp.where(mrow < N_META, 0.0, jnp.where(mrow == N_META, sink, -jnp.inf))

    n_blk = q_ref.shape[0] // BLOCK
    starts, variants = [], []
    for blk in range(n_blk):
        ib = i * n_blk + blk
        starts.append(pl.multiple_of(jnp.clip((ib - 1) * BLOCK, 0, s_len - _KWIN), BLOCK))
        variants.append(jnp.where(ib == 0, 0, jnp.where(ib == nq - 1, 2, 1)))
    low_half = lax.broadcasted_iota(jnp.int32, (BLOCK, LANES), 1) < ATT_HEAD_DIM
    zero = jnp.zeros((BLOCK, LANES), _BF)

    chains = [(blk, g, pl_) for blk in range(n_blk) for g in range(ATT_KV_HEADS) for pl_ in range(pairs_per_group)]

    def scores(blk, g, pl_):
        lanes_g = slice(g * LANES, (g + 1) * LANES)
        pair = g * pairs_per_group + pl_
        qp = q_ref[blk * BLOCK:(blk + 1) * BLOCK, pair * LANES:(pair + 1) * LANES]
        qs = jnp.concatenate([jnp.where(low_half, qp, zero), jnp.where(low_half, zero, qp)], axis=0)
        kg = k_ref[pl.ds(starts[blk], _KWIN), lanes_g]
        s_band = _dot_nt(kg, qs) + band_scr[variants[blk], g, pl_]
        s_meta = _dot_nt(km_ref[:, lanes_g], qs) + meta_scr[g, pl_]
        return s_band, s_meta

    def rows_reduce(op, arrays):
        parts = [a[r:r + 8, :] for a in arrays for r in range(0, a.shape[0], 8)]
        while len(parts) > 1:
            parts = [op(parts[j], parts[j + 1]) if j + 1 < len(parts) else parts[j]
                     for j in range(0, len(parts), 2)]
        return parts[0]

    def softmax(s_band, s_meta):
        mx = jnp.max(rows_reduce(jnp.maximum, [s_band, s_meta]), axis=0, keepdims=True)
        p_band = jnp.exp(s_band - mx)
        p_meta = jnp.exp(s_meta - mx)
        den = jnp.sum(rows_reduce(jnp.add, [p_band, p_meta]), axis=0, keepdims=True)
        return p_band.astype(_BF), p_meta.astype(_BF), den

    def values(blk, g, pl_, p_band, p_meta, den):
        drows = slice(g * ATT_HEAD_DIM, (g + 1) * ATT_HEAD_DIM)
        vt = vt_ref[drows, pl.ds(starts[blk], _KWIN)]
        ot = (_dot(vt, p_band) + _dot(vmt_ref[drows, :], p_meta)) * (1.0 / den)
        o2 = jnp.concatenate([ot[:, :BLOCK], ot[:, BLOCK:]], axis=0)
        pair = g * pairs_per_group + pl_
        o_ref[blk * BLOCK:(blk + 1) * BLOCK, pair * LANES:(pair + 1) * LANES] = o2.T.astype(_BF)

    s_vals, p_vals = {}, {}
    for t in range(len(chains) + 2):
        if t < len(chains):
            s_vals[t] = scores(*chains[t])
        if 0 <= t - 1 < len(chains):
            p_vals[t - 1] = softmax(*s_vals.pop(t - 1))
        if 0 <= t - 2 < len(chains):
            values(*chains[t - 2], *p_vals.pop(t - 2))


def _attention(aq, ak2, avt, km2, vmt, sink, batch, seq):
    m = aq.shape[0]
    nblk = ATT_QBLOCKS if (seq // BLOCK) % ATT_QBLOCKS == 0 else 1
    nq = seq // (BLOCK * nblk)
    return pl.pallas_call(
        _attn_kernel,
        grid=(batch, nq),
        in_specs=[pl.BlockSpec(memory_space=pltpu.SMEM),
                  pl.BlockSpec((nblk * BLOCK, ATT_WIDTH), lambda b, i: (b * nq + i, 0)),
                  pl.BlockSpec((seq, 2 * LANES), lambda b, i: (b, 0)),
                  pl.BlockSpec((ATT_KV_WIDTH, seq), lambda b, i: (0, b)),
                  pl.BlockSpec((LANES, 2 * LANES), lambda b, i: (0, 0)),
                  pl.BlockSpec((ATT_KV_WIDTH, LANES), lambda b, i: (0, 0))],
        out_specs=pl.BlockSpec((nblk * BLOCK, ATT_WIDTH), lambda b, i: (b * nq + i, 0)),
        out_shape=jax.ShapeDtypeStruct((m, ATT_WIDTH), _BF),
        scratch_shapes=[pltpu.VMEM((len(_WIN_OFFSETS), ATT_KV_HEADS, ATT_REP // 2, _KWIN, 2 * BLOCK), _F32),
                        pltpu.VMEM((ATT_KV_HEADS, ATT_REP // 2, LANES, 2 * BLOCK), _F32)],
        compiler_params=pltpu.CompilerParams(dimension_semantics=("arbitrary", "arbitrary"),
                                             vmem_limit_bytes=VMEM_LIMIT),
        name="attention",
    )(sink, aq, ak2, avt, km2, vmt)


_LOG2E = 1.4426950408889634


def _gla_log2_gates(z, w2_ref, bias_ref):
    zh = z.astype(_BF)
    zm = (z - zh.astype(_F32)).astype(_BF)
    lane = lax.broadcasted_iota(jnp.int32, z.shape, 1)
    mid_lanes = (lane >= N_DIR * GLA_RANK) & (lane < 2 * N_DIR * GLA_RANK)
    logits = _dot(jnp.where(mid_lanes, zm, zh), w2_ref[...]) + bias_ref[...]
    y = logits * _LOG2E
    log2_sig = jnp.minimum(y, 0.0) - jnp.log2(1.0 + jnp.exp2(-jnp.abs(y)))
    return log2_sig * (1.0 / GLA_TAU)


def _sel_matrix():
    sel = np.zeros((N_DIR, 2, GLA_BLK, LANES, LANES), np.float32)
    for d in range(N_DIR):
        for lg in range(2):
            for s in range(GLA_BLK):
                for hl in range(2):
                    col = d * 64 + (2 * lg + hl) * GLA_BLK + s
                    sel[d, lg, s, hl * GLA_DK:(hl + 1) * GLA_DK, col] = 1.0
    return sel.reshape(N_DIR, 2 * GLA_BLK * LANES, LANES)


def _gla_prep_kernel(q0_ref, q1_ref, k0_ref, k1_ref, z_ref, w2_ref, bias_ref, sel_ref, perm_ref,
                     qk_ref, dec_ref, r_ref,
                     g_scr, b_scr, kbf_scr, qk_scr, dec_scr):
    tt_rows = z_ref.shape[0]
    units = tt_rows // PM_UNIT
    ub = PM_UNIT // GLA_BLK
    rc = min(512, tt_rows)
    for c in range(tt_rows // rc):
        rows = pl.ds(c * rc, rc)
        g = _gla_log2_gates(z_ref[rows, :], w2_ref, bias_ref)
        for gi in range(4):
            g_scr[gi, rows, :] = g[:, gi * LANES:(gi + 1) * LANES]

    def slab(ref, tt, lead=()):
        return jnp.concatenate([ref[lead + (pl.ds(u * PM_UNIT + tt * ub, ub), slice(None))] for u in range(units)],
                               axis=0)

    def store_slab(ref, lead, tt, val):
        for u in range(units):
            ref[lead + (pl.ds(u * PM_UNIT + tt * ub, ub), slice(None))] = val[u * ub:(u + 1) * ub, :]

    qk_refs = ((q0_ref, k0_ref), (q1_ref, k1_ref))
    last = {}
    for lg, (_, k_ref) in enumerate(qk_refs):
        for tt in range(GLA_BLK):
            kbf_scr[lg, tt] = slab(k_ref, tt).astype(_BF)
        for d in range(N_DIR):
            gi = d * 2 + lg
            acc = None
            for tt in (range(GLA_BLK) if d == 0 else reversed(range(GLA_BLK))):
                gs = slab(g_scr, tt, (gi,))
                acc = gs if acc is None else acc + gs
                b_scr[gi, tt] = acc
            last[gi] = acc
            dec_scr[gi] = jnp.exp2(acc)
    for tt in range(GLA_BLK):
        prods, sel_rows = [], []
        for d in range(N_DIR):
            positions = list(range(0, tt + 1)) if d == 0 else list(range(tt + 1, GLA_BLK))
            for lg, (q_ref, k_ref) in enumerate(qk_refs):
                gi = d * 2 + lg
                q = slab(q_ref, tt)
                b = b_scr[gi, tt]
                store_slab(qk_scr, (d * 4 + lg,), tt, q * jnp.exp2(b))
                store_slab(qk_scr, (d * 4 + 2 + lg,), tt, slab(k_ref, tt) * jnp.exp2(last[gi] - b))
                if not positions:
                    continue
                qb = q.astype(_BF)
                for s in positions:
                    p = qb * kbf_scr[lg, s]
                    if s != tt:
                        p = p * jnp.exp2(b - b_scr[gi, s]).astype(_BF)
                    prods.append(p)
                r0 = lg * GLA_BLK * LANES + positions[0] * LANES
                sel_rows.append(sel_ref[d, r0:r0 + len(positions) * LANES, :])
        a_tt = _dot(jnp.concatenate(prods, axis=1), jnp.concatenate(sel_rows, axis=0))
        store_slab(qk_scr, (8,), tt, a_tt)
    for u in range(units):
        rows = pl.ds(u * PM_UNIT, PM_UNIT)
        x = jnp.concatenate([qk_scr[c, rows, :] for c in range(9)], axis=1).astype(_BF)
        y = _dot(perm_ref[...], x).astype(_BF)
        qk_ref[rows, :] = y[:, :8 * LANES]
        r_ref[rows, :] = y[:, 8 * LANES:]
    for gi in range(4):
        dec_ref[:, gi * LANES:(gi + 1) * LANES] = dec_scr[gi]


def _gla_prep(gq, gk, gz, w2s, bias, sel):
    m = gq.shape[0]
    tt = min(PREP_TILE, m)
    assert m % tt == 0 and tt % PM_UNIT == 0
    nb = tt // GLA_BLK
    perm = jnp.asarray(_position_major_perm(PM_UNIT), _BF)
    col = lambda j: pl.BlockSpec((tt, LANES), lambda i: (i, j))
    return pl.pallas_call(
        _gla_prep_kernel,
        grid=(m // tt,),
        in_specs=[col(0), col(1), col(0), col(1), col(0),
                  _const_spec(w2s.shape), _const_spec(bias.shape), _const_spec(sel.shape),
                  _const_spec((PM_UNIT, PM_UNIT))],
        out_specs=[pl.BlockSpec((tt, 8 * LANES), lambda i: (i, 0)),
                   pl.BlockSpec((nb, 4 * LANES), lambda i: (i, 0)),
                   pl.BlockSpec((tt, LANES), lambda i: (i, 0))],
        out_shape=[jax.ShapeDtypeStruct((m, 8 * LANES), _BF),
                   jax.ShapeDtypeStruct((m // GLA_BLK, 4 * LANES), _F32),
                   jax.ShapeDtypeStruct((m, LANES), _BF)],
        scratch_shapes=[pltpu.VMEM((4, tt, LANES), _F32),
                        pltpu.VMEM((4, GLA_BLK, nb, LANES), _F32),
                        pltpu.VMEM((2, GLA_BLK, nb, LANES), _BF),
                        pltpu.VMEM((9, tt, LANES), _F32),
                        pltpu.VMEM((4, nb, LANES), _F32)],
        compiler_params=pltpu.CompilerParams(dimension_semantics=("arbitrary",), vmem_limit_bytes=VMEM_LIMIT),
        name="gla_prep",
    )(gq, gq, gk, gk, gz, w2s, bias, sel, perm)


def _gla_meta_state_kernel(k_ref, z_ref, v_ref, w2_ref, bias_ref, s0_ref):
    g = _gla_log2_gates(z_ref[...], w2_ref, bias_ref)[:, :GLA_QK_WIDTH]
    r = lax.broadcasted_iota(jnp.int32, (LANES, LANES), 0)
    c = lax.broadcasted_iota(jnp.int32, (LANES, LANES), 1)
    tail = jnp.where((c > r) & (c < N_META), 1.0, 0.0).astype(_BF)
    gh, gm, gl = _split3(g)
    rest = _dot(tail, gh) + _dot(tail, gm) + _dot(tail, gl)
    is_meta = lax.broadcasted_iota(jnp.int32, (LANES, GLA_QK_WIDTH), 0) < N_META
    ke = jnp.where(is_meta, k_ref[...] * jnp.exp2(jnp.where(is_meta, rest, 0.0)), 0.0).astype(_BF)
    lane = lax.broadcasted_iota(jnp.int32, (GLA_DV, GLA_QK_WIDTH), 1)
    v = v_ref[...].astype(_F32)
    s0 = jnp.zeros((GLA_DV, GLA_QK_WIDTH), _F32)
    for h in range(GLA_HEADS):
        vt = v[:, h * GLA_DV:(h + 1) * GLA_DV].T.astype(_BF)
        in_head = (lane >= h * GLA_DK) & (lane < (h + 1) * GLA_DK)
        s0 = s0 + jnp.where(in_head, _dot(vt, ke), 0.0)
    s0_ref[...] = s0


def _gla_meta_state(gk_m, gz_m, gv_m, w2s, bias):
    return pl.pallas_call(
        _gla_meta_state_kernel,
        out_shape=jax.ShapeDtypeStruct((GLA_DV, GLA_QK_WIDTH), _F32),
        name="gla_meta_state",
    )(gk_m, gz_m, gv_m, w2s, bias)


def _gla_scan_kernel(qkf_ref, qkb_ref, rf_ref, rb_ref, df_ref, db_ref, vf_ref, vb_ref, s0_ref,
                     of_ref, ob_ref, sf_scr, sb_scr):
    n_seq = sf_scr.shape[0]

    @pl.when(pl.program_id(1) == 0)
    def _():
        for b in range(n_seq):
            sf_scr[b] = s0_ref[...]
        sb_scr[...] = jnp.zeros_like(sb_scr)

    def lane_band(shape, width, n):
        lane = lax.broadcasted_iota(jnp.int32, shape, 1)
        return [(lane >= c * width) & (lane < (c + 1) * width) for c in range(n)]

    head_lanes = lane_band((GLA_BLK, GLA_QK_WIDTH), GLA_DK, GLA_HEADS)
    a_lanes = lane_band((GLA_BLK, LANES), GLA_BLK, N_DIR * GLA_HEADS)

    def head_stack(x, bands):
        return jnp.concatenate([jnp.where(bands[h], x, jnp.zeros_like(x)) for h in range(GLA_HEADS)], axis=0)

    dirs = ((qkf_ref, rf_ref, df_ref, vf_ref, of_ref, sf_scr),
            (qkb_ref, rb_ref, db_ref, vb_ref, ob_ref, sb_scr))
    for step in range(BLK_PER_GROUP):
        for b in range(n_seq):
            for d, (qk_ref, r_ref, d_ref, v_ref, o_ref, s_scr) in enumerate(dirs):
                j = step if d == 0 else BLK_PER_GROUP - 1 - step
                rows = slice(j * GLA_BLK, (j + 1) * GLA_BLK)
                state_t = s_scr[b]
                lhs1 = head_stack(qk_ref[b, rows, :GLA_QK_WIDTH], head_lanes)
                lhs2 = head_stack(r_ref[b, rows, :], a_lanes[d * GLA_HEADS:(d + 1) * GLA_HEADS])
                vblk = v_ref[b, rows, :]
                vstack = jnp.concatenate([vblk[:, h * GLA_DV:(h + 1) * GLA_DV] for h in range(GLA_HEADS)], axis=0)
                vv = jnp.concatenate([vstack, vstack], axis=0)
                o = _dot_nt(lhs1, state_t.astype(_BF)) + _dot(lhs2, vv)
                for h in range(GLA_HEADS):
                    o_ref[h, b, rows, :] = o[h * GLA_BLK:(h + 1) * GLA_BLK, :]
                kbd = head_stack(qk_ref[b, rows, GLA_QK_WIDTH:], head_lanes)
                vst = vstack.astype(_F32).T.astype(_BF)
                s_scr[b] = state_t * d_ref[b, j:j + 1, :] + _dot(vst, kbd)


def _gla_scan(qk, r, dec, gv, s0, batch, seq):
    ng = seq // GLA_GROUP
    nbs = SCAN_SEQS if batch % SCAN_SEQS == 0 else 1
    qk = qk.reshape(batch, seq, N_DIR * 2 * GLA_QK_WIDTH)
    r = r.reshape(batch, seq, LANES)
    dec = dec.reshape(batch, seq // GLA_BLK, N_DIR * GLA_QK_WIDTH)
    gv = gv.reshape(batch, seq, GLA_WIDTH)
    fwd = lambda i: i
    bwd = lambda i: ng - 1 - i
    o_shape = jax.ShapeDtypeStruct((GLA_HEADS, batch, seq, GLA_DV), _F32)

    def spec(rows, width, grp, col):
        return pl.BlockSpec((nbs, rows, width), lambda b, i: (b, grp(i), col))

    o_f, o_b = pl.pallas_call(
        _gla_scan_kernel,
        grid=(batch // nbs, ng),
        in_specs=[spec(GLA_GROUP, 2 * GLA_QK_WIDTH, fwd, 0), spec(GLA_GROUP, 2 * GLA_QK_WIDTH, bwd, 1),
                  spec(GLA_GROUP, LANES, fwd, 0), spec(GLA_GROUP, LANES, bwd, 0),
                  spec(BLK_PER_GROUP, GLA_QK_WIDTH, fwd, 0), spec(BLK_PER_GROUP, GLA_QK_WIDTH, bwd, 1),
                  spec(GLA_GROUP, GLA_WIDTH, fwd, 0), spec(GLA_GROUP, GLA_WIDTH, bwd, 0),
                  pl.BlockSpec((GLA_DV, GLA_QK_WIDTH), lambda b, i: (0, 0))],
        out_specs=[pl.BlockSpec((GLA_HEADS, nbs, GLA_GROUP, GLA_DV), lambda b, i: (0, b, fwd(i), 0)),
                   pl.BlockSpec((GLA_HEADS, nbs, GLA_GROUP, GLA_DV), lambda b, i: (0, b, bwd(i), 0))],
        out_shape=[o_shape, o_shape],
        scratch_shapes=[pltpu.VMEM((nbs, GLA_DV, GLA_QK_WIDTH), _F32), pltpu.VMEM((nbs, GLA_DV, GLA_QK_WIDTH), _F32)],
        compiler_params=pltpu.CompilerParams(dimension_semantics=("arbitrary", "arbitrary"),
                                             vmem_limit_bytes=VMEM_LIMIT),
        name="gla_scan",
    )(qk, qk, r, r, dec, dec, gv, gv, s0)
    return (o_f.reshape(GLA_HEADS, batch * seq, GLA_DV), o_b.reshape(GLA_HEADS, batch * seq, GLA_DV))


def _stage_c_kernel(h1_ref, oatt_ref, of_ref, ob_ref, gg_ref, ng_ref, wout_ref, gm_ref, bm_ref,
                    wg_ref, wu_ref, wd_ref, g2_ref, b2_ref, out_ref, act_ref):
    def steps(rows):
        pieces = []
        for h in range(GLA_HEADS):
            o = of_ref[h, rows, :] + ob_ref[h, rows, :]
            o = o * lax.rsqrt(jnp.mean(o * o, axis=-1, keepdims=True) + LN_EPS) * ng_ref[...]
            pieces.append((o * _silu(gg_ref[rows, h * GLA_DV:(h + 1) * GLA_DV])).astype(_BF))
        mix = jnp.concatenate([oatt_ref[rows, :]] + pieces, axis=1)
        yield
        h2 = _layer_norm(DN_ALPHA * h1_ref[rows, :] + _dot(mix, wout_ref[...]), gm_ref[...], bm_ref[...])
        yield
        out_ref[rows, :] = yield from _swiglu_ln_steps(h2, wg_ref, wu_ref, wd_ref, g2_ref, b2_ref,
                                                       act_ref.at[rows])

    _run_skewed([steps(rows) for rows in _sub_tiles(h1_ref.shape[0])], SUB_TILE_SKEW)


def _stage_c(h1, oatt, o_f, o_b, gg, ng, wout, gm, bm, wg, wu, wd, g2, b2):
    m, d = h1.shape
    tm = min(ROW_TILE, m)
    assert m % tm == 0
    n_ff = wg.shape[1]
    row = lambda w: pl.BlockSpec((tm, w), lambda i: (i, 0))
    hrow = pl.BlockSpec((GLA_HEADS, tm, GLA_DV), lambda i: (0, i, 0))
    return pl.pallas_call(
        _stage_c_kernel,
        grid=(m // tm,),
        in_specs=[row(d), row(ATT_WIDTH), hrow, hrow, row(GLA_WIDTH), _const_spec((1, GLA_DV)),
                  _const_spec(wout.shape), _const_spec((1, d)), _const_spec((1, d)),
                  _const_spec((d, n_ff)), _const_spec((d, n_ff)), _const_spec((n_ff, d)),
                  _const_spec((1, d)), _const_spec((1, d))],
        out_specs=row(d),
        out_shape=jax.ShapeDtypeStruct((m, d), _F32),
        scratch_shapes=[pltpu.VMEM((tm, n_ff), _BF)],
        compiler_params=pltpu.CompilerParams(dimension_semantics=("arbitrary",), vmem_limit_bytes=VMEM_LIMIT),
        name="stage_c",
    )(h1, oatt, o_f, o_b, gg, ng, wout, gm, bm, wg, wu, wd, g2, b2)


def _projection_weight(w_in):
    aq = w_in[:, 0:512]
    ak = w_in[:, 512:640]
    rest = w_in[:, 640:2304]
    gz = w_in[:, 2304:2336]
    ak_dup = jnp.concatenate([ak[:, 0:64], ak[:, 0:64], ak[:, 64:128], ak[:, 64:128]], axis=1)
    gz3 = jnp.pad(jnp.concatenate([gz, gz, gz], axis=1), ((0, 0), (0, LANES - 3 * gz.shape[1])))
    return jnp.concatenate([aq, ak_dup, rest, gz3], axis=1).astype(_BF)


def _gate_weight(w2, bias):
    k = N_DIR * GLA_RANK
    wcat = jnp.zeros((k, N_DIR * GLA_QK_WIDTH), _F32)
    for n in range(N_DIR):
        wcat = wcat.at[n * GLA_RANK:(n + 1) * GLA_RANK, n * GLA_QK_WIDTH:(n + 1) * GLA_QK_WIDTH].set(w2[n])
    hi = wcat.astype(_BF)
    mid = (wcat - hi.astype(_F32)).astype(_BF)
    pad = jnp.zeros((LANES - 3 * k, N_DIR * GLA_QK_WIDTH), _BF)
    return jnp.concatenate([hi, hi, mid, pad], axis=0), bias.reshape(1, N_DIR * GLA_QK_WIDTH)


def _pad_rows(a, rows):
    return jnp.pad(a, ((0, rows - a.shape[0]), (0, 0)))


def kernel(x, meta_tokens, ln_in_g, ln_in_b, ffn1_ln_g, ffn1_ln_b, ffn1_w_gate, ffn1_w_up, ffn1_w_down, w_in,
           attn_sink, gla_gate_w2, gla_gate_b, gla_norm_g, w_out, mix_ln_g, mix_ln_b, ffn2_w_gate, ffn2_w_up,
           ffn2_w_down, ffn2_ln_g, ffn2_ln_b):
    batch, seq, d = x.shape
    assert seq % BLOCK == 0 and seq >= _KWIN and seq % GLA_GROUP == 0
    vec = lambda a: a.reshape(1, -1)
    x2d = x.reshape(batch * seq, d)
    wg1, wu1, wd1 = ffn1_w_gate[0].astype(_BF), ffn1_w_up[0].astype(_BF), ffn1_w_down[0].astype(_BF)
    wg2, wu2, wd2 = ffn2_w_gate[0].astype(_BF), ffn2_w_up[0].astype(_BF), ffn2_w_down[0].astype(_BF)
    win = _projection_weight(w_in[0])
    w2s, gbias = _gate_weight(gla_gate_w2[0], gla_gate_b[0])
    sel = jnp.asarray(_sel_matrix(), _BF)

    a_args = (vec(ln_in_g), vec(ln_in_b), wg1, wu1, wd1, vec(ffn1_ln_g[0]), vec(ffn1_ln_b[0]), win)
    h1, aq, ak2, avt, gq, gk, gv, gg, gz = _stage_a(x2d, *a_args)
    _, _, ak2_m, avt_m, _, gk_m, gv_m, _, gz_m = _stage_a(meta_tokens.astype(x.dtype), *a_args)

    vmt = jnp.pad(avt_m, ((0, 0), (0, LANES - avt_m.shape[1])))
    o_att = _attention(aq, ak2, avt, _pad_rows(ak2_m, LANES), vmt, attn_sink[0].astype(_F32), batch, seq)

    qk, dec, r = _gla_prep(gq, gk, gz, w2s, gbias, sel)
    s0 = _gla_meta_state(_pad_rows(gk_m, LANES), _pad_rows(gz_m, LANES), _pad_rows(gv_m, LANES), w2s, gbias)
    o_f, o_b = _gla_scan(qk, r, dec, gv, s0, batch, seq)

    out = _stage_c(h1, o_att, o_f, o_b, gg, vec(gla_norm_g[0]), w_out[0].astype(_BF),
                   vec(mix_ln_g[0]), vec(mix_ln_b[0]), wg2, wu2, wd2, vec(ffn2_ln_g[0]), vec(ffn2_ln_b[0]))
    return out.reshape(batch, seq, d)
```

```python
import functools

import numpy as np
import jax
import jax.numpy as jnp
from jax import lax
from jax.experimental import pallas as pl
from jax.experimental.pallas import tpu as pltpu

N_META = 16
ATT_HEADS = 8
ATT_KV_HEADS = 2
ATT_HEAD_DIM = 64
ATT_REP = ATT_HEADS // ATT_KV_HEADS
WINDOW = 128
BLOCK = 128
GLA_HEADS = 4
GLA_DK = 64
GLA_DV = 128
GLA_RANK = 16
GLA_TAU = 16.0
N_DIR = 2
ATT_WIDTH = ATT_HEADS * ATT_HEAD_DIM
ATT_KV_WIDTH = ATT_KV_HEADS * ATT_HEAD_DIM
GLA_QK_WIDTH = GLA_HEADS * GLA_DK
GLA_WIDTH = GLA_HEADS * GLA_DV
LN_EPS = 1e-5
DEPTH = 1
DN_ALPHA = (2.0 * DEPTH) ** 0.25

LANES = 128
GLA_BLK = 16
GLA_GROUP = 128
BLK_PER_GROUP = GLA_GROUP // GLA_BLK
SCAN_SEQS = 4
FF_CHUNK = 256
ROW_TILE = 512
ROW_SUB_TILES = 2
SUB_TILE_SKEW = 3
LN_PIECES = 4
PM_UNIT = ROW_TILE // ROW_SUB_TILES
PREP_TILE = 1024
VMEM_LIMIT = 56 * 1024 * 1024

_BF = jnp.bfloat16
_F32 = jnp.float32


def _dot(a, b):
    return jnp.dot(a, b, preferred_element_type=_F32)


def _dot_nt(a, b):
    return lax.dot_general(a, b, (((1,), (1,)), ((), ())), preferred_element_type=_F32)


def _layer_norm(x, g, b):
    mu = jnp.mean(x, axis=-1, keepdims=True)
    xc = x - mu
    var = jnp.mean(xc * xc, axis=-1, keepdims=True)
    return xc * lax.rsqrt(var + LN_EPS) * g + b


def _silu(x):
    return x * (1.0 / (1.0 + jnp.exp(-x)))


def _split3(x):
    hi = x.astype(_BF)
    r1 = x - hi.astype(_F32)
    mid = r1.astype(_BF)
    lo = (r1 - mid.astype(_F32)).astype(_BF)
    return hi, mid, lo


def _sub_tiles(rows):
    n = ROW_SUB_TILES if rows % (ROW_SUB_TILES * 16) == 0 else 1
    return [pl.ds(s * (rows // n), rows // n) for s in range(n)]


def _run_skewed(step_gens, skew):
    pending = list(enumerate(step_gens))
    tick = 0
    while pending:
        for k, gen in list(pending):
            if tick < k * skew:
                continue
            try:
                next(gen)
            except StopIteration:
                pending.remove((k, gen))
        tick += 1


def _swiglu_ln_steps(h, wg_ref, wu_ref, wd_ref, g_ref, b_ref, act_ref):
    hb = h.astype(_BF)
    n_ff = wg_ref.shape[1]
    for c in range(n_ff // FF_CHUNK):
        sl = slice(c * FF_CHUNK, (c + 1) * FF_CHUNK)
        gate = _dot(hb, wg_ref[:, sl])
        up = _dot(hb, wu_ref[:, sl])
        act_ref[:, sl] = (_silu(gate) * up).astype(_BF)
        yield
    y = _dot(act_ref[...], wd_ref[...])
    return (yield from _residual_ln_steps(h, 0.5 * y, g_ref, b_ref))


def _residual_ln_steps(h, branch, g_ref, b_ref):
    rows = h.shape[0]
    piece = rows // LN_PIECES if rows % (LN_PIECES * 8) == 0 else rows
    out = []
    for r in range(0, rows, piece):
        out.append(_layer_norm(DN_ALPHA * h[r:r + piece] + branch[r:r + piece], g_ref[...], b_ref[...]))
        if r + piece < rows:
            yield
    return out[0] if len(out) == 1 else jnp.concatenate(out, axis=0)


_P_AQ = (0, 512)
_P_AK = (512, 768)
_P_AV = (768, 896)
_P_GQ = (896, 1152)
_P_GK = (1152, 1408)
_P_GV = (1408, 1920)
_P_GG = (1920, 2432)
_P_GZ = (2432, 2560)
_P_WIDTH = 2560


def _position_major_perm(rows):
    nblk = rows // GLA_BLK
    perm = np.zeros((rows, rows), np.float32)
    for n in range(nblk):
        for t in range(GLA_BLK):
            perm[t * nblk + n, n * GLA_BLK + t] = 1.0
    return perm


def _stage_a_kernel(x_ref, g0_ref, b0_ref, wg_ref, wu_ref, wd_ref, g1_ref, b1_ref, win_ref, perm_ref,
                    h1_ref, aq_ref, ak_ref, avt_ref, gq_ref, gk_ref, gv_ref, gg_ref, gz_ref, act_ref):
    def steps(rows):
        h0 = _layer_norm(x_ref[rows, :], g0_ref[...], b0_ref[...])
        yield
        h1 = yield from _swiglu_ln_steps(h0, wg_ref, wu_ref, wd_ref, g1_ref, b1_ref, act_ref.at[rows])
        h1_ref[rows, :] = h1
        hb = h1.astype(_BF)
        hb_pm = _dot(perm_ref[...], hb).astype(_BF)
        yield

        def proj(act, cols):
            return _dot(act, win_ref[:, cols[0]:cols[1]])

        aq_ref[rows, :] = (proj(hb, _P_AQ) * (ATT_HEAD_DIM ** -0.5)).astype(_BF)
        ak_ref[rows, :] = proj(hb, _P_AK).astype(_BF)
        avt_ref[:, rows] = proj(hb, _P_AV).T.astype(_BF)
        gq_ref[rows, :] = proj(hb_pm, _P_GQ) * (GLA_DK ** -0.5)
        yield
        gk_ref[rows, :] = proj(hb_pm, _P_GK)
        gv_ref[rows, :] = proj(hb, _P_GV).astype(_BF)
        gg_ref[rows, :] = proj(hb, _P_GG)
        gz_ref[rows, :] = proj(hb_pm, _P_GZ)

    _run_skewed([steps(rows) for rows in _sub_tiles(x_ref.shape[0])], SUB_TILE_SKEW)


def _const_spec(shape):
    nd = len(shape)
    return pl.BlockSpec(shape, lambda *_: (0,) * nd, pipeline_mode=pl.Buffered(1))


def _stage_a(x2d, g0, b0, wg, wu, wd, g1, b1, win):
    m, d = x2d.shape
    tm = min(ROW_TILE, m)
    assert m % tm == 0
    n_ff = wg.shape[1]
    sub = _sub_tiles(tm)[0].size
    perm = jnp.asarray(_position_major_perm(sub), _BF)
    row = lambda w: pl.BlockSpec((tm, w), lambda i: (i, 0))
    out_widths = [(d, _F32), (512, _BF), (256, _BF), None, (256, _F32), (256, _F32), (512, _BF), (512, _F32),
                  (128, _F32)]
    out_specs = [row(o[0]) if o else pl.BlockSpec((ATT_KV_WIDTH, tm), lambda i: (0, i)) for o in out_widths]
    out_shape = [jax.ShapeDtypeStruct((m, o[0]), o[1]) if o else jax.ShapeDtypeStruct((ATT_KV_WIDTH, m), _BF)
                 for o in out_widths]
    return pl.pallas_call(
        _stage_a_kernel,
        grid=(m // tm,),
        in_specs=[row(d), _const_spec((1, d)), _const_spec((1, d)), _const_spec((d, n_ff)), _const_spec((d, n_ff)),
                  _const_spec((n_ff, d)), _const_spec((1, d)), _const_spec((1, d)), _const_spec((d, _P_WIDTH)),
                  _const_spec((sub, sub))],
        out_specs=out_specs,
        out_shape=out_shape,
        scratch_shapes=[pltpu.VMEM((tm, n_ff), _BF)],
        compiler_params=pltpu.CompilerParams(dimension_semantics=("arbitrary",), vmem_limit_bytes=VMEM_LIMIT),
        name="stage_a",
    )(x2d, g0, b0, wg, wu, wd, g1, b1, win, perm)


_KWIN = 3 * BLOCK


def _alibi_slopes():
    return [2.0 ** (-8.0 * (i + 1) / ATT_HEADS) for i in range(ATT_HEADS)]


_WIN_OFFSETS = (0, -BLOCK, -2 * BLOCK)
ATT_QBLOCKS = 4


def _attn_kernel(sink_ref, q_ref, k_ref, vt_ref, km_ref, vmt_ref, o_ref, band_scr, meta_scr):
    i = pl.program_id(1)
    s_len = k_ref.shape[0]
    nq = s_len // BLOCK
    pairs_per_group = ATT_REP // 2

    @pl.when((pl.program_id(0) == 0) & (i == 0))
    def _():
        col = lax.broadcasted_iota(jnp.int32, (_KWIN, 2 * BLOCK), 1)
        rel0 = lax.broadcasted_iota(jnp.int32, (_KWIN, 2 * BLOCK), 0) - jnp.where(col < BLOCK, col, col - BLOCK)
        mrow = lax.broadcasted_iota(jnp.int32, (LANES, 2 * BLOCK), 0)
        mcol = lax.broadcasted_iota(jnp.int32, (LANES, 2 * BLOCK), 1)
        slopes = _alibi_slopes()
        for g in range(ATT_KV_HEADS):
            for pl_ in range(pairs_per_group):
                h0 = g * ATT_REP + 2 * pl_
                slope = jnp.where(col < BLOCK, slopes[h0], slopes[h0 + 1])
                for var, off in enumerate(_WIN_OFFSETS):
                    absrel = jnp.abs(rel0 + off).astype(_F32)
                    band_scr[var, g, pl_] = jnp.where(absrel <= float(WINDOW), -slope * absrel, -jnp.inf)
                sink = jnp.where(mcol < BLOCK, sink_ref[h0], sink_ref[h0 + 1])
                meta_scr[g, pl_] = jnp.where(mrow < N_META, 0.0, jnp.where(mrow == N_META, sink, -jnp.inf))

    n_blk = q_ref.shape[0] // BLOCK
    starts, variants = [], []
    for blk in range(n_blk):
        ib = i * n_blk + blk
        starts.append(pl.multiple_of(jnp.clip((ib - 1) * BLOCK, 0, s_len - _KWIN), BLOCK))
        variants.append(jnp.where(ib == 0, 0, jnp.where(ib == nq - 1, 2, 1)))
    low_half = lax.broadcasted_iota(jnp.int32, (BLOCK, LANES), 1) < ATT_HEAD_DIM
    zero = jnp.zeros((BLOCK, LANES), _BF)

    chains = [(blk, g, pl_) for blk in range(n_blk) for g in range(ATT_KV_HEADS) for pl_ in range(pairs_per_group)]

    def scores(blk, g, pl_):
        lanes_g = slice(g * LANES, (g + 1) * LANES)
        pair = g * pairs_per_group + pl_
        qp = q_ref[blk * BLOCK:(blk + 1) * BLOCK, pair * LANES:(pair + 1) * LANES]
        qs = jnp.concatenate([jnp.where(low_half, qp, zero), jnp.where(low_half, zero, qp)], axis=0)
        kg = k_ref[pl.ds(starts[blk], _KWIN), lanes_g]
        s_band = _dot_nt(kg, qs) + band_scr[variants[blk], g, pl_]
        s_meta = _dot_nt(km_ref[:, lanes_g], qs) + meta_scr[g, pl_]
        return s_band, s_meta

    def rows_reduce(op, arrays):
        parts = [a[r:r + 8, :] for a in arrays for r in range(0, a.shape[0], 8)]
        while len(parts) > 1:
            parts = [op(parts[j], parts[j + 1]) if j + 1 < len(parts) else parts[j]
                     for j in range(0, len(parts), 2)]
        return parts[0]

    def softmax(s_band, s_meta):
        mx = jnp.max(rows_reduce(jnp.maximum, [s_band, s_meta]), axis=0, keepdims=True)
        p_band = jnp.exp(s_band - mx)
        p_meta = jnp.exp(s_meta - mx)
        den = jnp.sum(rows_reduce(jnp.add, [p_band, p_meta]), axis=0, keepdims=True)
        return p_band.astype(_BF), p_meta.astype(_BF), den

    def values(blk, g, pl_, p_band, p_meta, den):
        drows = slice(g * ATT_HEAD_DIM, (g + 1) * ATT_HEAD_DIM)
        vt = vt_ref[drows, pl.ds(starts[blk], _KWIN)]
        ot = (_dot(vt, p_band) + _dot(vmt_ref[drows, :], p_meta)) * (1.0 / den)
        o2 = jnp.concatenate([ot[:, :BLOCK], ot[:, BLOCK:]], axis=0)
        pair = g * pairs_per_group + pl_
        o_ref[blk * BLOCK:(blk + 1) * BLOCK, pair * LANES:(pair + 1) * LANES] = o2.T.astype(_BF)

    s_vals, p_vals = {}, {}
    for t in range(len(chains) + 2):
        if t < len(chains):
            s_vals[t] = scores(*chains[t])
        if 0 <= t - 1 < len(chains):
            p_vals[t - 1] = softmax(*s_vals.pop(t - 1))
        if 0 <= t - 2 < len(chains):
            values(*chains[t - 2], *p_vals.pop(t - 2))


def _attention(aq, ak2, avt, km2, vmt, sink, batch, seq):
    m = aq.shape[0]
    nblk = ATT_QBLOCKS if (seq // BLOCK) % ATT_QBLOCKS == 0 else 1
    nq = seq // (BLOCK * nblk)
    return pl.pallas_call(
        _attn_kernel,
        grid=(batch, nq),
        in_specs=[pl.BlockSpec(memory_space=pltpu.SMEM),
                  pl.BlockSpec((nblk * BLOCK, ATT_WIDTH), lambda b, i: (b * nq + i, 0)),
                  pl.BlockSpec((seq, 2 * LANES), lambda b, i: (b, 0)),
                  pl.BlockSpec((ATT_KV_WIDTH, seq), lambda b, i: (0, b)),
                  pl.BlockSpec((LANES, 2 * LANES), lambda b, i: (0, 0)),
                  pl.BlockSpec((ATT_KV_WIDTH, LANES), lambda b, i: (0, 0))],
        out_specs=pl.BlockSpec((nblk * BLOCK, ATT_WIDTH), lambda b, i: (b * nq + i, 0)),
        out_shape=jax.ShapeDtypeStruct((m, ATT_WIDTH), _BF),
        scratch_shapes=[pltpu.VMEM((len(_WIN_OFFSETS), ATT_KV_HEADS, ATT_REP // 2, _KWIN, 2 * BLOCK), _F32),
                        pltpu.VMEM((ATT_KV_HEADS, ATT_REP // 2, LANES, 2 * BLOCK), _F32)],
        compiler_params=pltpu.CompilerParams(dimension_semantics=("arbitrary", "arbitrary"),
                                             vmem_limit_bytes=VMEM_LIMIT),
        name="attention",
    )(sink, aq, ak2, avt, km2, vmt)


_LOG2E = 1.4426950408889634


def _gla_log2_gates(z, w2_ref, bias_ref):
    zh = z.astype(_BF)
    zm = (z - zh.astype(_F32)).astype(_BF)
    lane = lax.broadcasted_iota(jnp.int32, z.shape, 1)
    mid_lanes = (lane >= N_DIR * GLA_RANK) & (lane < 2 * N_DIR * GLA_RANK)
    logits = _dot(jnp.where(mid_lanes, zm, zh), w2_ref[...]) + bias_ref[...]
    y = logits * _LOG2E
    log2_sig = jnp.minimum(y, 0.0) - jnp.log2(1.0 + jnp.exp2(-jnp.abs(y)))
    return log2_sig * (1.0 / GLA_TAU)


def _sel_matrix():
    sel = np.zeros((N_DIR, 2, GLA_BLK, LANES, LANES), np.float32)
    for d in range(N_DIR):
        for lg in range(2):
            for s in range(GLA_BLK):
                for hl in range(2):
                    col = d * 64 + (2 * lg + hl) * GLA_BLK + s
                    sel[d, lg, s, hl * GLA_DK:(hl + 1) * GLA_DK, col] = 1.0
    return sel.reshape(N_DIR, 2 * GLA_BLK * LANES, LANES)


def _gla_prep_kernel(q0_ref, q1_ref, k0_ref, k1_ref, z_ref, w2_ref, bias_ref, sel_ref, perm_ref,
                     qk_ref, dec_ref, r_ref,
                     g_scr, b_scr, kbf_scr, qk_scr, dec_scr):
    tt_rows = z_ref.shape[0]
    units = tt_rows // PM_UNIT
    ub = PM_UNIT // GLA_BLK
    rc = min(512, tt_rows)
    for c in range(tt_rows // rc):
        rows = pl.ds(c * rc, rc)
        g = _gla_log2_gates(z_ref[rows, :], w2_ref, bias_ref)
        for gi in range(4):
            g_scr[gi, rows, :] = g[:, gi * LANES:(gi + 1) * LANES]

    def slab(ref, tt, lead=()):
        return jnp.concatenate([ref[lead + (pl.ds(u * PM_UNIT + tt * ub, ub), slice(None))] for u in range(units)],
                               axis=0)

    def store_slab(ref, lead, tt, val):
        for u in range(units):
            ref[lead + (pl.ds(u * PM_UNIT + tt * ub, ub), slice(None))] = val[u * ub:(u + 1) * ub, :]

    qk_refs = ((q0_ref, k0_ref), (q1_ref, k1_ref))
    last = {}
    for lg, (_, k_ref) in enumerate(qk_refs):
        for tt in range(GLA_BLK):
            kbf_scr[lg, tt] = slab(k_ref, tt).astype(_BF)
        for d in range(N_DIR):
            gi = d * 2 + lg
            acc = None
            for tt in (range(GLA_BLK) if d == 0 else reversed(range(GLA_BLK))):
                gs = slab(g_scr, tt, (gi,))
                acc = gs if acc is None else acc + gs
                b_scr[gi, tt] = acc
            last[gi] = acc
            dec_scr[gi] = jnp.exp2(acc)
    for tt in range(GLA_BLK):
        prods, sel_rows = [], []
        for d in range(N_DIR):
            positions = list(range(0, tt + 1)) if d == 0 else list(range(tt + 1, GLA_BLK))
            for lg, (q_ref, k_ref) in enumerate(qk_refs):
                gi = d * 2 + lg
                q = slab(q_ref, tt)
                b = b_scr[gi, tt]
                store_slab(qk_scr, (d * 4 + lg,), tt, q * jnp.exp2(b))
                store_slab(qk_scr, (d * 4 + 2 + lg,), tt, slab(k_ref, tt) * jnp.exp2(last[gi] - b))
                if not positions:
                    continue
                qb = q.astype(_BF)
                for s in positions:
                    p = qb * kbf_scr[lg, s]
                    if s != tt:
                        p = p * jnp.exp2(b - b_scr[gi, s]).astype(_BF)
                    prods.append(p)
                r0 = lg * GLA_BLK * LANES + positions[0] * LANES
                sel_rows.append(sel_ref[d, r0:r0 + len(positions) * LANES, :])
        a_tt = _dot(jnp.concatenate(prods, axis=1), jnp.concatenate(sel_rows, axis=0))
        store_slab(qk_scr, (8,), tt, a_tt)
    for u in range(units):
        rows = pl.ds(u * PM_UNIT, PM_UNIT)
        x = jnp.concatenate([qk_scr[c, rows, :] for c in range(9)], axis=1).astype(_BF)
        y = _dot(perm_ref[...], x).astype(_BF)
        qk_ref[rows, :] = y[:, :8 * LANES]
        r_ref[rows, :] = y[:, 8 * LANES:]
    for gi in range(4):
        dec_ref[:, gi * LANES:(gi + 1) * LANES] = dec_scr[gi]


def _gla_prep(gq, gk, gz, w2s, bias, sel):
    m = gq.shape[0]
    tt = min(PREP_TILE, m)
    assert m % tt == 0 and tt % PM_UNIT == 0
    nb = tt // GLA_BLK
    perm = jnp.asarray(_position_major_perm(PM_UNIT).T, _BF)
    col = lambda j: pl.BlockSpec((tt, LANES), lambda i: (i, j))
    return pl.pallas_call(
        _gla_prep_kernel,
        grid=(m // tt,),
        in_specs=[col(0), col(1), col(0), col(1), col(0),
                  _const_spec(w2s.shape), _const_spec(bias.shape), _const_spec(sel.shape),
                  _const_spec((PM_UNIT, PM_UNIT))],
        out_specs=[pl.BlockSpec((tt, 8 * LANES), lambda i: (i, 0)),
                   pl.BlockSpec((nb, 4 * LANES), lambda i: (i, 0)),
                   pl.BlockSpec((tt, LANES), lambda i: (i, 0))],
        out_shape=[jax.ShapeDtypeStruct((m, 8 * LANES), _BF),
                   jax.ShapeDtypeStruct((m // GLA_BLK, 4 * LANES), _F32),
                   jax.ShapeDtypeStruct((m, LANES), _BF)],
        scratch_shapes=[pltpu.VMEM((4, tt, LANES), _F32),
                        pltpu.VMEM((4, GLA_BLK, nb, LANES), _F32),
                        pltpu.VMEM((2, GLA_BLK, nb, LANES), _BF),
                        pltpu.VMEM((9, tt, LANES), _F32),
                        pltpu.VMEM((4, nb, LANES), _F32)],
        compiler_params=pltpu.CompilerParams(dimension_semantics=("arbitrary",), vmem_limit_bytes=VMEM_LIMIT),
        name="gla_prep",
    )(gq, gq, gk, gk, gz, w2s, bias, sel, perm)


def _gla_meta_state_kernel(k_ref, z_ref, v_ref, w2_ref, bias_ref, s0_ref):
    g = _gla_log2_gates(z_ref[...], w2_ref, bias_ref)[:, :GLA_QK_WIDTH]
    r = lax.broadcasted_iota(jnp.int32, (LANES, LANES), 0)
    c = lax.broadcasted_iota(jnp.int32, (LANES, LANES), 1)
    tail = jnp.where((c > r) & (c < N_META), 1.0, 0.0).astype(_BF)
    gh, gm, gl = _split3(g)
    rest = _dot(tail, gh) + _dot(tail, gm) + _dot(tail, gl)
    is_meta = lax.broadcasted_iota(jnp.int32, (LANES, GLA_QK_WIDTH), 0) < N_META
    ke = jnp.where(is_meta, k_ref[...] * jnp.exp2(jnp.where(is_meta, rest, 0.0)), 0.0).astype(_BF)
    lane = lax.broadcasted_iota(jnp.int32, (GLA_DV, GLA_QK_WIDTH), 1)
    v = v_ref[...].astype(_F32)
    s0 = jnp.zeros((GLA_DV, GLA_QK_WIDTH), _F32)
    for h in range(GLA_HEADS):
        vt = v[:, h * GLA_DV:(h + 1) * GLA_DV].T.astype(_BF)
        in_head = (lane >= h * GLA_DK) & (lane < (h + 1) * GLA_DK)
        s0 = s0 + jnp.where(in_head, _dot(vt, ke), 0.0)
    s0_ref[...] = s0


def _gla_meta_state(gk_m, gz_m, gv_m, w2s, bias):
    return pl.pallas_call(
        _gla_meta_state_kernel,
        out_shape=jax.ShapeDtypeStruct((GLA_DV, GLA_QK_WIDTH), _F32),
        name="gla_meta_state",
    )(gk_m, gz_m, gv_m, w2s, bias)


def _gla_scan_kernel(qkf_ref, qkb_ref, rf_ref, rb_ref, df_ref, db_ref, vf_ref, vb_ref, s0_ref,
                     of_ref, ob_ref, sf_scr, sb_scr):
    n_seq = sf_scr.shape[0]

    @pl.when(pl.program_id(1) == 0)
    def _():
        for b in range(n_seq):
            sf_scr[b] = s0_ref[...]
        sb_scr[...] = jnp.zeros_like(sb_scr)

    def lane_band(shape, width, n):
        lane = lax.broadcasted_iota(jnp.int32, shape, 1)
        return [(lane >= c * width) & (lane < (c + 1) * width) for c in range(n)]

    head_lanes = lane_band((GLA_BLK, GLA_QK_WIDTH), GLA_DK, GLA_HEADS)
    a_lanes = lane_band((GLA_BLK, LANES), GLA_BLK, N_DIR * GLA_HEADS)

    def head_stack(x, bands):
        return jnp.concatenate([jnp.where(bands[h], x, jnp.zeros_like(x)) for h in range(GLA_HEADS)], axis=0)

    dirs = ((qkf_ref, rf_ref, df_ref, vf_ref, of_ref, sf_scr),
            (qkb_ref, rb_ref, db_ref, vb_ref, ob_ref, sb_scr))
    for step in range(BLK_PER_GROUP):
        for b in range(n_seq):
            for d, (qk_ref, r_ref, d_ref, v_ref, o_ref, s_scr) in enumerate(dirs):
                j = step if d == 0 else BLK_PER_GROUP - 1 - step
                rows = slice(j * GLA_BLK, (j + 1) * GLA_BLK)
                state_t = s_scr[b]
                lhs1 = head_stack(qk_ref[b, rows, :GLA_QK_WIDTH], head_lanes)
                lhs2 = head_stack(r_ref[b, rows, :], a_lanes[d * GLA_HEADS:(d + 1) * GLA_HEADS])
                vblk = v_ref[b, rows, :]
                vstack = jnp.concatenate([vblk[:, h * GLA_DV:(h + 1) * GLA_DV] for h in range(GLA_HEADS)], axis=0)
                vv = jnp.concatenate([vstack, vstack], axis=0)
                o = _dot_nt(lhs1, state_t.astype(_BF)) + _dot(lhs2, vv)
                for h in range(GLA_HEADS):
                    o_ref[h, b, rows, :] = o[h * GLA_BLK:(h + 1) * GLA_BLK, :]
                kbd = head_stack(qk_ref[b, rows, GLA_QK_WIDTH:], head_lanes)
                vst = vstack.astype(_F32).T.astype(_BF)
                s_scr[b] = state_t * d_ref[b, j:j + 1, :] + _dot(vst, kbd)


def _gla_scan(qk, r, dec, gv, s0, batch, seq):
    ng = seq // GLA_GROUP
    nbs = SCAN_SEQS if batch % SCAN_SEQS == 0 else 1
    qk = qk.reshape(batch, seq, N_DIR * 2 * GLA_QK_WIDTH)
    r = r.reshape(batch, seq, LANES)
    dec = dec.reshape(batch, seq // GLA_BLK, N_DIR * GLA_QK_WIDTH)
    gv = gv.reshape(batch, seq, GLA_WIDTH)
    fwd = lambda i: i
    bwd = lambda i: ng - 1 - i
    o_shape = jax.ShapeDtypeStruct((GLA_HEADS, batch, seq, GLA_DV), _F32)

    def spec(rows, width, grp, col):
        return pl.BlockSpec((nbs, rows, width), lambda b, i: (b, grp(i), col))

    o_f, o_b = pl.pallas_call(
        _gla_scan_kernel,
        grid=(batch // nbs, ng),
        in_specs=[spec(GLA_GROUP, 2 * GLA_QK_WIDTH, fwd, 0), spec(GLA_GROUP, 2 * GLA_QK_WIDTH, bwd, 1),
                  spec(GLA_GROUP, LANES, fwd, 0), spec(GLA_GROUP, LANES, bwd, 0),
                  spec(BLK_PER_GROUP, GLA_QK_WIDTH, fwd, 0), spec(BLK_PER_GROUP, GLA_QK_WIDTH, bwd, 1),
                  spec(GLA_GROUP, GLA_WIDTH, fwd, 0), spec(GLA_GROUP, GLA_WIDTH, bwd, 0),
                  pl.BlockSpec((GLA_DV, GLA_QK_WIDTH), lambda b, i: (0, 0))],
        out_specs=[pl.BlockSpec((GLA_HEADS, nbs, GLA_GROUP, GLA_DV), lambda b, i: (0, b, fwd(i), 0)),
                   pl.BlockSpec((GLA_HEADS, nbs, GLA_GROUP, GLA_DV), lambda b, i: (0, b, bwd(i), 0))],
        out_shape=[o_shape, o_shape],
        scratch_shapes=[pltpu.VMEM((nbs, GLA_DV, GLA_QK_WIDTH), _F32), pltpu.VMEM((nbs, GLA_DV, GLA_QK_WIDTH), _F32)],
        compiler_params=pltpu.CompilerParams(dimension_semantics=("arbitrary", "arbitrary"),
                                             vmem_limit_bytes=VMEM_LIMIT),
        name="gla_scan",
    )(qk, qk, r, r, dec, dec, gv, gv, s0)
    return (o_f.reshape(GLA_HEADS, batch * seq, GLA_DV), o_b.reshape(GLA_HEADS, batch * seq, GLA_DV))


def _stage_c_kernel(h1_ref, oatt_ref, of_ref, ob_ref, gg_ref, ng_ref, wout_ref, gm_ref, bm_ref,
                    wg_ref, wu_ref, wd_ref, g2_ref, b2_ref, out_ref, act_ref):
    def steps(rows):
        pieces = []
        for h in range(GLA_HEADS):
            o = of_ref[h, rows, :] + ob_ref[h, rows, :]
            o = o * lax.rsqrt(jnp.mean(o * o, axis=-1, keepdims=True) + LN_EPS) * ng_ref[...]
            pieces.append((o * _silu(gg_ref[rows, h * GLA_DV:(h + 1) * GLA_DV])).astype(_BF))
        mix = jnp.concatenate([oatt_ref[rows, :]] + pieces, axis=1)
        yield
        h2 = yield from _residual_ln_steps(h1_ref[rows, :], _dot(mix, wout_ref[...]), gm_ref, bm_ref)
        yield
        out_ref[rows, :] = yield from _swiglu_ln_steps(h2, wg_ref, wu_ref, wd_ref, g2_ref, b2_ref,
                                                       act_ref.at[rows])

    _run_skewed([steps(rows) for rows in _sub_tiles(h1_ref.shape[0])], SUB_TILE_SKEW)


def _stage_c(h1, oatt, o_f, o_b, gg, ng, wout, gm, bm, wg, wu, wd, g2, b2):
    m, d = h1.shape
    tm = min(ROW_TILE, m)
    assert m % tm == 0
    n_ff = wg.shape[1]
    row = lambda w: pl.BlockSpec((tm, w), lambda i: (i, 0))
    hrow = pl.BlockSpec((GLA_HEADS, tm, GLA_DV), lambda i: (0, i, 0))
    return pl.pallas_call(
        _stage_c_kernel,
        grid=(m // tm,),
        in_specs=[row(d), row(ATT_WIDTH), hrow, hrow, row(GLA_WIDTH), _const_spec((1, GLA_DV)),
                  _const_spec(wout.shape), _const_spec((1, d)), _const_spec((1, d)),
                  _const_spec((d, n_ff)), _const_spec((d, n_ff)), _const_spec((n_ff, d)),
                  _const_spec((1, d)), _const_spec((1, d))],
        out_specs=row(d),
        out_shape=jax.ShapeDtypeStruct((m, d), _F32),
        scratch_shapes=[pltpu.VMEM((tm, n_ff), _BF)],
        compiler_params=pltpu.CompilerParams(dimension_semantics=("arbitrary",), vmem_limit_bytes=VMEM_LIMIT),
        name="stage_c",
    )(h1, oatt, o_f, o_b, gg, ng, wout, gm, bm, wg, wu, wd, g2, b2)


def _projection_weight(w_in):
    aq = w_in[:, 0:512]
    ak = w_in[:, 512:640]
    rest = w_in[:, 640:2304]
    gz = w_in[:, 2304:2336]
    ak_dup = jnp.concatenate([ak[:, 0:64], ak[:, 0:64], ak[:, 64:128], ak[:, 64:128]], axis=1)
    gz3 = jnp.pad(jnp.concatenate([gz, gz, gz], axis=1), ((0, 0), (0, LANES - 3 * gz.shape[1])))
    return jnp.concatenate([aq, ak_dup, rest, gz3], axis=1).astype(_BF)


def _gate_weight(w2, bias):
    k = N_DIR * GLA_RANK
    wcat = jnp.zeros((k, N_DIR * GLA_QK_WIDTH), _F32)
    for n in range(N_DIR):
        wcat = wcat.at[n * GLA_RANK:(n + 1) * GLA_RANK, n * GLA_QK_WIDTH:(n + 1) * GLA_QK_WIDTH].set(w2[n])
    hi = wcat.astype(_BF)
    mid = (wcat - hi.astype(_F32)).astype(_BF)
    pad = jnp.zeros((LANES - 3 * k, N_DIR * GLA_QK_WIDTH), _BF)
    return jnp.concatenate([hi, hi, mid, pad], axis=0), bias.reshape(1, N_DIR * GLA_QK_WIDTH)


def _pad_rows(a, rows):
    return jnp.pad(a, ((0, rows - a.shape[0]), (0, 0)))


def kernel(x, meta_tokens, ln_in_g, ln_in_b, ffn1_ln_g, ffn1_ln_b, ffn1_w_gate, ffn1_w_up, ffn1_w_down, w_in,
           attn_sink, gla_gate_w2, gla_gate_b, gla_norm_g, w_out, mix_ln_g, mix_ln_b, ffn2_w_gate, ffn2_w_up,
           ffn2_w_down, ffn2_ln_g, ffn2_ln_b):
    batch, seq, d = x.shape
    assert seq % BLOCK == 0 and seq >= _KWIN and seq % GLA_GROUP == 0
    vec = lambda a: a.reshape(1, -1)
    x2d = x.reshape(batch * seq, d)
    wg1, wu1, wd1 = ffn1_w_gate[0].astype(_BF), ffn1_w_up[0].astype(_BF), ffn1_w_down[0].astype(_BF)
    wg2, wu2, wd2 = ffn2_w_gate[0].astype(_BF), ffn2_w_up[0].astype(_BF), ffn2_w_down[0].astype(_BF)
    win = _projection_weight(w_in[0])
    w2s, gbias = _gate_weight(gla_gate_w2[0], gla_gate_b[0])
    sel = jnp.asarray(_sel_matrix(), _BF)

    a_args = (vec(ln_in_g), vec(ln_in_b), wg1, wu1, wd1, vec(ffn1_ln_g[0]), vec(ffn1_ln_b[0]), win)
    h1, aq, ak2, avt, gq, gk, gv, gg, gz = _stage_a(x2d, *a_args)
    _, _, ak2_m, avt_m, _, gk_m, gv_m, _, gz_m = _stage_a(meta_tokens.astype(x.dtype), *a_args)

    vmt = jnp.pad(avt_m, ((0, 0), (0, LANES - avt_m.shape[1])))
    o_att = _attention(aq, ak2, avt, _pad_rows(ak2_m, LANES), vmt, attn_sink[0].astype(_F32), batch, seq)

    qk, dec, r = _gla_prep(gq, gk, gz, w2s, gbias, sel)
    s0 = _gla_meta_state(_pad_rows(gk_m, LANES), _pad_rows(gz_m, LANES), _pad_rows(gv_m, LANES), w2s, gbias)
    o_f, o_b = _gla_scan(qk, r, dec, gv, s0, batch, seq)

    out = _stage_c(h1, o_att, o_f, o_b, gg, vec(gla_norm_g[0]), w_out[0].astype(_BF),
                   vec(mix_ln_g[0]), vec(mix_ln_b[0]), wg2, wu2, wd2, vec(ffn2_ln_g[0]), vec(ffn2_ln_b[0]))
    return out.reshape(batch, seq, d)
```

```python
import functools

import numpy as np
import jax
import jax.numpy as jnp
from jax import lax
from jax.experimental import pallas as pl
from jax.experimental.pallas import tpu as pltpu

N_META = 16
ATT_HEADS = 8
ATT_KV_HEADS = 2
ATT_HEAD_DIM = 64
ATT_REP = ATT_HEADS // ATT_KV_HEADS
WINDOW = 128
BLOCK = 128
GLA_HEADS = 4
GLA_DK = 64
GLA_DV = 128
GLA_RANK = 16
GLA_TAU = 16.0
N_DIR = 2
ATT_WIDTH = ATT_HEADS * ATT_HEAD_DIM
ATT_KV_WIDTH = ATT_KV_HEADS * ATT_HEAD_DIM
GLA_QK_WIDTH = GLA_HEADS * GLA_DK
GLA_WIDTH = GLA_HEADS * GLA_DV
LN_EPS = 1e-5
DEPTH = 1
DN_ALPHA = (2.0 * DEPTH) ** 0.25

LANES = 128
GLA_BLK = 16
GLA_GROUP = 128
BLK_PER_GROUP = GLA_GROUP // GLA_BLK
SCAN_SEQS = 8
FF_CHUNK = 256
ROW_TILE = 512
ROW_SUB_TILES = 2
SUB_TILE_SKEW = 3
LN_PIECES = 4
PM_UNIT = ROW_TILE // ROW_SUB_TILES
PREP_TILE = 2048
VMEM_LIMIT = 56 * 1024 * 1024

_BF = jnp.bfloat16
_F32 = jnp.float32


def _dot(a, b):
    return jnp.dot(a, b, preferred_element_type=_F32)


def _dot_nt(a, b):
    return lax.dot_general(a, b, (((1,), (1,)), ((), ())), preferred_element_type=_F32)


def _layer_norm(x, g, b):
    mu = jnp.mean(x, axis=-1, keepdims=True)
    xc = x - mu
    var = jnp.mean(xc * xc, axis=-1, keepdims=True)
    return xc * lax.rsqrt(var + LN_EPS) * g + b


def _silu(x):
    return x * (1.0 / (1.0 + jnp.exp(-x)))


def _split3(x):
    hi = x.astype(_BF)
    r1 = x - hi.astype(_F32)
    mid = r1.astype(_BF)
    lo = (r1 - mid.astype(_F32)).astype(_BF)
    return hi, mid, lo


def _sub_tiles(rows):
    n = ROW_SUB_TILES if rows % (ROW_SUB_TILES * 16) == 0 else 1
    return [pl.ds(s * (rows // n), rows // n) for s in range(n)]


def _run_skewed(step_gens, skew):
    pending = list(enumerate(step_gens))
    tick = 0
    while pending:
        for k, gen in list(pending):
            if tick < k * skew:
                continue
            try:
                next(gen)
            except StopIteration:
                pending.remove((k, gen))
        tick += 1


def _swiglu_ln_steps(h, wg_ref, wu_ref, wd_ref, g_ref, b_ref, act_ref):
    hb = h.astype(_BF)
    n_ff = wg_ref.shape[1]
    for c in range(n_ff // FF_CHUNK):
        sl = slice(c * FF_CHUNK, (c + 1) * FF_CHUNK)
        gate = _dot(hb, wg_ref[:, sl])
        up = _dot(hb, wu_ref[:, sl])
        act_ref[:, sl] = (_silu(gate) * up).astype(_BF)
        yield
    y = _dot(act_ref[...], wd_ref[...])
    return (yield from _residual_ln_steps(h, 0.5 * y, g_ref, b_ref))


def _residual_ln_steps(h, branch, g_ref, b_ref):
    rows = h.shape[0]
    piece = rows // LN_PIECES if rows % (LN_PIECES * 8) == 0 else rows
    out = []
    for r in range(0, rows, piece):
        out.append(_layer_norm(DN_ALPHA * h[r:r + piece] + branch[r:r + piece], g_ref[...], b_ref[...]))
        if r + piece < rows:
            yield
    return out[0] if len(out) == 1 else jnp.concatenate(out, axis=0)


_P_AQ = (0, 512)
_P_AK = (512, 768)
_P_AV = (768, 896)
_P_GQ = (896, 1152)
_P_GK = (1152, 1408)
_P_GV = (1408, 1920)
_P_GG = (1920, 2432)
_P_GZ = (2432, 2560)
_P_WIDTH = 2560


def _position_major_perm(rows):
    nblk = rows // GLA_BLK
    perm = np.zeros((rows, rows), np.float32)
    for n in range(nblk):
        for t in range(GLA_BLK):
            perm[t * nblk + n, n * GLA_BLK + t] = 1.0
    return perm


def _stage_a_kernel(x_ref, g0_ref, b0_ref, wg_ref, wu_ref, wd_ref, g1_ref, b1_ref, win_ref, perm_ref,
                    h1_ref, aq_ref, ak_ref, avt_ref, gq_ref, gk_ref, gv_ref, gg_ref, gz_ref, act_ref):
    def steps(rows):
        h0 = _layer_norm(x_ref[rows, :], g0_ref[...], b0_ref[...])
        yield
        h1 = yield from _swiglu_ln_steps(h0, wg_ref, wu_ref, wd_ref, g1_ref, b1_ref, act_ref.at[rows])
        h1_ref[rows, :] = h1
        hb = h1.astype(_BF)
        hb_pm = _dot(perm_ref[...], hb).astype(_BF)
        yield

        def proj(act, cols):
            return _dot(act, win_ref[:, cols[0]:cols[1]])

        aq_ref[rows, :] = (proj(hb, _P_AQ) * (ATT_HEAD_DIM ** -0.5)).astype(_BF)
        ak_ref[rows, :] = proj(hb, _P_AK).astype(_BF)
        avt_ref[:, rows] = proj(hb, _P_AV).T.astype(_BF)
        gq_ref[rows, :] = proj(hb_pm, _P_GQ) * (GLA_DK ** -0.5)
        yield
        gk_ref[rows, :] = proj(hb_pm, _P_GK)
        gv_ref[rows, :] = proj(hb, _P_GV).astype(_BF)
        gg_ref[rows, :] = proj(hb, _P_GG)
        gz_ref[rows, :] = proj(hb_pm, _P_GZ)

    _run_skewed([steps(rows) for rows in _sub_tiles(x_ref.shape[0])], SUB_TILE_SKEW)


def _const_spec(shape):
    nd = len(shape)
    return pl.BlockSpec(shape, lambda *_: (0,) * nd, pipeline_mode=pl.Buffered(1))


def _stage_a(x2d, g0, b0, wg, wu, wd, g1, b1, win):
    m, d = x2d.shape
    tm = min(ROW_TILE, m)
    assert m % tm == 0
    n_ff = wg.shape[1]
    sub = _sub_tiles(tm)[0].size
    perm = jnp.asarray(_position_major_perm(sub), _BF)
    row = lambda w: pl.BlockSpec((tm, w), lambda i: (i, 0))
    out_widths = [(d, _F32), (512, _BF), (256, _BF), None, (256, _F32), (256, _F32), (512, _BF), (512, _F32),
                  (128, _F32)]
    out_specs = [row(o[0]) if o else pl.BlockSpec((ATT_KV_WIDTH, tm), lambda i: (0, i)) for o in out_widths]
    out_shape = [jax.ShapeDtypeStruct((m, o[0]), o[1]) if o else jax.ShapeDtypeStruct((ATT_KV_WIDTH, m), _BF)
                 for o in out_widths]
    return pl.pallas_call(
        _stage_a_kernel,
        grid=(m // tm,),
        in_specs=[row(d), _const_spec((1, d)), _const_spec((1, d)), _const_spec((d, n_ff)), _const_spec((d, n_ff)),
                  _const_spec((n_ff, d)), _const_spec((1, d)), _const_spec((1, d)), _const_spec((d, _P_WIDTH)),
                  _const_spec((sub, sub))],
        out_specs=out_specs,
        out_shape=out_shape,
        scratch_shapes=[pltpu.VMEM((tm, n_ff), _BF)],
        compiler_params=pltpu.CompilerParams(dimension_semantics=("arbitrary",), vmem_limit_bytes=VMEM_LIMIT),
        name="stage_a",
    )(x2d, g0, b0, wg, wu, wd, g1, b1, win, perm)


_KWIN = 3 * BLOCK


def _alibi_slopes():
    return [2.0 ** (-8.0 * (i + 1) / ATT_HEADS) for i in range(ATT_HEADS)]


_WIN_OFFSETS = (0, -BLOCK, -2 * BLOCK)
ATT_QBLOCKS = 4
META_ROWS = 32


def _attn_kernel(sink_ref, q_ref, k_ref, vt_ref, km_ref, vmt_ref, o_ref, band_scr, meta_scr):
    i = pl.program_id(1)
    s_len = k_ref.shape[0]
    nq = s_len // BLOCK
    pairs_per_group = ATT_REP // 2

    @pl.when((pl.program_id(0) == 0) & (i == 0))
    def _():
        col = lax.broadcasted_iota(jnp.int32, (_KWIN, 2 * BLOCK), 1)
        rel0 = lax.broadcasted_iota(jnp.int32, (_KWIN, 2 * BLOCK), 0) - jnp.where(col < BLOCK, col, col - BLOCK)
        mrow = lax.broadcasted_iota(jnp.int32, (META_ROWS, 2 * BLOCK), 0)
        mcol = lax.broadcasted_iota(jnp.int32, (META_ROWS, 2 * BLOCK), 1)
        slopes = _alibi_slopes()
        for g in range(ATT_KV_HEADS):
            for pl_ in range(pairs_per_group):
                h0 = g * ATT_REP + 2 * pl_
                slope = jnp.where(col < BLOCK, slopes[h0], slopes[h0 + 1])
                for var, off in enumerate(_WIN_OFFSETS):
                    absrel = jnp.abs(rel0 + off).astype(_F32)
                    band_scr[var, g, pl_] = jnp.where(absrel <= float(WINDOW), -slope * absrel, -jnp.inf)
                sink = jnp.where(mcol < BLOCK, sink_ref[h0], sink_ref[h0 + 1])
                meta_scr[g, pl_] = jnp.where(mrow < N_META, 0.0, jnp.where(mrow == N_META, sink, -jnp.inf))

    n_blk = q_ref.shape[0] // BLOCK
    starts, variants = [], []
    for blk in range(n_blk):
        ib = i * n_blk + blk
        starts.append(pl.multiple_of(jnp.clip((ib - 1) * BLOCK, 0, s_len - _KWIN), BLOCK))
        variants.append(jnp.where(ib == 0, 0, jnp.where(ib == nq - 1, 2, 1)))
    low_half = lax.broadcasted_iota(jnp.int32, (BLOCK, LANES), 1) < ATT_HEAD_DIM
    zero = jnp.zeros((BLOCK, LANES), _BF)

    chains = [(blk, g, pl_) for blk in range(n_blk) for g in range(ATT_KV_HEADS) for pl_ in range(pairs_per_group)]

    def scores(blk, g, pl_):
        lanes_g = slice(g * LANES, (g + 1) * LANES)
        pair = g * pairs_per_group + pl_
        qp = q_ref[blk * BLOCK:(blk + 1) * BLOCK, pair * LANES:(pair + 1) * LANES]
        qs = jnp.concatenate([jnp.where(low_half, qp, zero), jnp.where(low_half, zero, qp)], axis=0)
        kg = k_ref[pl.ds(starts[blk], _KWIN), lanes_g]
        s_band = _dot_nt(kg, qs) + band_scr[variants[blk], g, pl_]
        s_meta = _dot_nt(km_ref[:, lanes_g], qs) + meta_scr[g, pl_]
        return s_band, s_meta

    def rows_reduce(op, arrays):
        parts = [a[r:r + 8, :] for a in arrays for r in range(0, a.shape[0], 8)]
        while len(parts) > 1:
            parts = [op(parts[j], parts[j + 1]) if j + 1 < len(parts) else parts[j]
                     for j in range(0, len(parts), 2)]
        return parts[0]

    def softmax(s_band, s_meta):
        mx = jnp.max(rows_reduce(jnp.maximum, [s_band, s_meta]), axis=0, keepdims=True)
        p_band = jnp.exp(s_band - mx)
        p_meta = jnp.exp(s_meta - mx)
        den = jnp.sum(rows_reduce(jnp.add, [p_band, p_meta]), axis=0, keepdims=True)
        return p_band.astype(_BF), p_meta.astype(_BF), den

    def values(blk, g, pl_, p_band, p_meta, den):
        drows = slice(g * ATT_HEAD_DIM, (g + 1) * ATT_HEAD_DIM)
        vt = vt_ref[drows, pl.ds(starts[blk], _KWIN)]
        ot = (_dot(vt, p_band) + _dot(vmt_ref[drows, :], p_meta)) * (1.0 / den)
        o2 = jnp.concatenate([ot[:, :BLOCK], ot[:, BLOCK:]], axis=0)
        pair = g * pairs_per_group + pl_
        o_ref[blk * BLOCK:(blk + 1) * BLOCK, pair * LANES:(pair + 1) * LANES] = o2.T.astype(_BF)

    s_vals, p_vals = {}, {}
    for t in range(len(chains) + 2):
        if t < len(chains):
            s_vals[t] = scores(*chains[t])
        if 0 <= t - 1 < len(chains):
            p_vals[t - 1] = softmax(*s_vals.pop(t - 1))
        if 0 <= t - 2 < len(chains):
            values(*chains[t - 2], *p_vals.pop(t - 2))


def _attention(aq, ak2, avt, km2, vmt, sink, batch, seq):
    m = aq.shape[0]
    nblk = ATT_QBLOCKS if (seq // BLOCK) % ATT_QBLOCKS == 0 else 1
    nq = seq // (BLOCK * nblk)
    return pl.pallas_call(
        _attn_kernel,
        grid=(batch, nq),
        in_specs=[pl.BlockSpec(memory_space=pltpu.SMEM),
                  pl.BlockSpec((nblk * BLOCK, ATT_WIDTH), lambda b, i: (b * nq + i, 0)),
                  pl.BlockSpec((seq, 2 * LANES), lambda b, i: (b, 0)),
                  pl.BlockSpec((ATT_KV_WIDTH, seq), lambda b, i: (0, b)),
                  pl.BlockSpec((META_ROWS, 2 * LANES), lambda b, i: (0, 0)),
                  pl.BlockSpec((ATT_KV_WIDTH, META_ROWS), lambda b, i: (0, 0))],
        out_specs=pl.BlockSpec((nblk * BLOCK, ATT_WIDTH), lambda b, i: (b * nq + i, 0)),
        out_shape=jax.ShapeDtypeStruct((m, ATT_WIDTH), _BF),
        scratch_shapes=[pltpu.VMEM((len(_WIN_OFFSETS), ATT_KV_HEADS, ATT_REP // 2, _KWIN, 2 * BLOCK), _F32),
                        pltpu.VMEM((ATT_KV_HEADS, ATT_REP // 2, META_ROWS, 2 * BLOCK), _F32)],
        compiler_params=pltpu.CompilerParams(dimension_semantics=("arbitrary", "arbitrary"),
                                             vmem_limit_bytes=VMEM_LIMIT),
        name="attention",
    )(sink, aq, ak2, avt, km2, vmt)


_LOG2E = 1.4426950408889634


def _gla_log2_gates(z, w2_ref, bias_ref):
    zh = z.astype(_BF)
    zm = (z - zh.astype(_F32)).astype(_BF)
    lane = lax.broadcasted_iota(jnp.int32, z.shape, 1)
    mid_lanes = (lane >= N_DIR * GLA_RANK) & (lane < 2 * N_DIR * GLA_RANK)
    logits = _dot(jnp.where(mid_lanes, zm, zh), w2_ref[...]) + bias_ref[...]
    y = logits * _LOG2E
    log2_sig = jnp.minimum(y, 0.0) - jnp.log2(1.0 + jnp.exp2(-jnp.abs(y)))
    return log2_sig * (1.0 / GLA_TAU)


def _sel_matrix():
    sel = np.zeros((N_DIR, 2, GLA_BLK, LANES, LANES), np.float32)
    for d in range(N_DIR):
        for lg in range(2):
            for s in range(GLA_BLK):
                for hl in range(2):
                    col = d * 64 + (2 * lg + hl) * GLA_BLK + s
                    sel[d, lg, s, hl * GLA_DK:(hl + 1) * GLA_DK, col] = 1.0
    return sel.reshape(N_DIR, 2 * GLA_BLK * LANES, LANES)


def _gla_prep_kernel(q0_ref, q1_ref, k0_ref, k1_ref, z_ref, w2_ref, bias_ref, sel_ref, perm_ref,
                     qk_ref, dec_ref, r_ref,
                     g_scr, b_scr, kbf_scr, qk_scr, dec_scr):
    tt_rows = z_ref.shape[0]
    units = tt_rows // PM_UNIT
    ub = PM_UNIT // GLA_BLK
    rc = min(512, tt_rows)
    for c in range(tt_rows // rc):
        rows = pl.ds(c * rc, rc)
        g = _gla_log2_gates(z_ref[rows, :], w2_ref, bias_ref)
        for gi in range(4):
            g_scr[gi, rows, :] = g[:, gi * LANES:(gi + 1) * LANES]

    def slab(ref, tt, lead=()):
        return jnp.concatenate([ref[lead + (pl.ds(u * PM_UNIT + tt * ub, ub), slice(None))] for u in range(units)],
                               axis=0)

    def store_slab(ref, lead, tt, val):
        for u in range(units):
            ref[lead + (pl.ds(u * PM_UNIT + tt * ub, ub), slice(None))] = val[u * ub:(u + 1) * ub, :]

    qk_refs = ((q0_ref, k0_ref), (q1_ref, k1_ref))
    last = {}
    for lg, (_, k_ref) in enumerate(qk_refs):
        for tt in range(GLA_BLK):
            kbf_scr[lg, tt] = slab(k_ref, tt).astype(_BF)
        for d in range(N_DIR):
            gi = d * 2 + lg
            acc = None
            for tt in (range(GLA_BLK) if d == 0 else reversed(range(GLA_BLK))):
                gs = slab(g_scr, tt, (gi,))
                acc = gs if acc is None else acc + gs
                b_scr[gi, tt] = acc
            last[gi] = acc
            dec_scr[gi] = jnp.exp2(acc)
    for tt in range(GLA_BLK):
        prods, sel_rows = [], []
        for d in range(N_DIR):
            positions = list(range(0, tt + 1)) if d == 0 else list(range(tt + 1, GLA_BLK))
            for lg, (q_ref, k_ref) in enumerate(qk_refs):
                gi = d * 2 + lg
                q = slab(q_ref, tt)
                b = b_scr[gi, tt]
                store_slab(qk_scr, (d * 4 + lg,), tt, q * jnp.exp2(b))
                store_slab(qk_scr, (d * 4 + 2 + lg,), tt, slab(k_ref, tt) * jnp.exp2(last[gi] - b))
                if not positions:
                    continue
                qb = q.astype(_BF)
                for s in positions:
                    p = qb * kbf_scr[lg, s]
                    if s != tt:
                        p = p * jnp.exp2(b - b_scr[gi, s]).astype(_BF)
                    prods.append(p)
                r0 = lg * GLA_BLK * LANES + positions[0] * LANES
                sel_rows.append(sel_ref[d, r0:r0 + len(positions) * LANES, :])
        a_tt = _dot(jnp.concatenate(prods, axis=1), jnp.concatenate(sel_rows, axis=0))
        store_slab(qk_scr, (8,), tt, a_tt)
    for u in range(units):
        rows = pl.ds(u * PM_UNIT, PM_UNIT)
        x = jnp.concatenate([qk_scr[c, rows, :] for c in range(9)], axis=1).astype(_BF)
        y = _dot(perm_ref[...], x).astype(_BF)
        qk_ref[rows, :] = y[:, :8 * LANES]
        r_ref[rows, :] = y[:, 8 * LANES:]
    for gi in range(4):
        dec_ref[:, gi * LANES:(gi + 1) * LANES] = dec_scr[gi]


def _gla_prep(gq, gk, gz, w2s, bias, sel):
    m = gq.shape[0]
    tt = min(PREP_TILE, m)
    assert m % tt == 0 and tt % PM_UNIT == 0
    nb = tt // GLA_BLK
    perm = jnp.asarray(_position_major_perm(PM_UNIT).T, _BF)
    col = lambda j: pl.BlockSpec((tt, LANES), lambda i: (i, j))
    return pl.pallas_call(
        _gla_prep_kernel,
        grid=(m // tt,),
        in_specs=[col(0), col(1), col(0), col(1), col(0),
                  _const_spec(w2s.shape), _const_spec(bias.shape), _const_spec(sel.shape),
                  _const_spec((PM_UNIT, PM_UNIT))],
        out_specs=[pl.BlockSpec((tt, 8 * LANES), lambda i: (i, 0)),
                   pl.BlockSpec((nb, 4 * LANES), lambda i: (i, 0)),
                   pl.BlockSpec((tt, LANES), lambda i: (i, 0))],
        out_shape=[jax.ShapeDtypeStruct((m, 8 * LANES), _BF),
                   jax.ShapeDtypeStruct((m // GLA_BLK, 4 * LANES), _F32),
                   jax.ShapeDtypeStruct((m, LANES), _BF)],
        scratch_shapes=[pltpu.VMEM((4, tt, LANES), _F32),
                        pltpu.VMEM((4, GLA_BLK, nb, LANES), _F32),
                        pltpu.VMEM((2, GLA_BLK, nb, LANES), _BF),
                        pltpu.VMEM((9, tt, LANES), _F32),
                        pltpu.VMEM((4, nb, LANES), _F32)],
        compiler_params=pltpu.CompilerParams(dimension_semantics=("arbitrary",), vmem_limit_bytes=VMEM_LIMIT),
        name="gla_prep",
    )(gq, gq, gk, gk, gz, w2s, bias, sel, perm)


def _gla_meta_state_kernel(k_ref, z_ref, v_ref, w2_ref, bias_ref, s0_ref):
    g = _gla_log2_gates(z_ref[...], w2_ref, bias_ref)[:, :GLA_QK_WIDTH]
    r = lax.broadcasted_iota(jnp.int32, (LANES, LANES), 0)
    c = lax.broadcasted_iota(jnp.int32, (LANES, LANES), 1)
    tail = jnp.where((c > r) & (c < N_META), 1.0, 0.0).astype(_BF)
    gh, gm, gl = _split3(g)
    rest = _dot(tail, gh) + _dot(tail, gm) + _dot(tail, gl)
    is_meta = lax.broadcasted_iota(jnp.int32, (LANES, GLA_QK_WIDTH), 0) < N_META
    ke = jnp.where(is_meta, k_ref[...] * jnp.exp2(jnp.where(is_meta, rest, 0.0)), 0.0).astype(_BF)
    lane = lax.broadcasted_iota(jnp.int32, (GLA_DV, GLA_QK_WIDTH), 1)
    v = v_ref[...].astype(_F32)
    s0 = jnp.zeros((GLA_DV, GLA_QK_WIDTH), _F32)
    for h in range(GLA_HEADS):
        vt = v[:, h * GLA_DV:(h + 1) * GLA_DV].T.astype(_BF)
        in_head = (lane >= h * GLA_DK) & (lane < (h + 1) * GLA_DK)
        s0 = s0 + jnp.where(in_head, _dot(vt, ke), 0.0)
    s0_ref[...] = s0


def _gla_meta_state(gk_m, gz_m, gv_m, w2s, bias):
    return pl.pallas_call(
        _gla_meta_state_kernel,
        out_shape=jax.ShapeDtypeStruct((GLA_DV, GLA_QK_WIDTH), _F32),
        name="gla_meta_state",
    )(gk_m, gz_m, gv_m, w2s, bias)


def _gla_scan_kernel(qkf_ref, qkb_ref, rf_ref, rb_ref, df_ref, db_ref, vf_ref, vb_ref, s0_ref,
                     of_ref, ob_ref, sf_scr, sb_scr):
    n_seq = sf_scr.shape[0]

    @pl.when(pl.program_id(1) == 0)
    def _():
        for b in range(n_seq):
            sf_scr[b] = s0_ref[...]
        sb_scr[...] = jnp.zeros_like(sb_scr)

    def lane_band(shape, width, n):
        lane = lax.broadcasted_iota(jnp.int32, shape, 1)
        return [(lane >= c * width) & (lane < (c + 1) * width) for c in range(n)]

    head_lanes = lane_band((GLA_BLK, GLA_QK_WIDTH), GLA_DK, GLA_HEADS)
    a_lanes = lane_band((GLA_BLK, LANES), GLA_BLK, N_DIR * GLA_HEADS)

    def head_stack(x, bands):
        return jnp.concatenate([jnp.where(bands[h], x, jnp.zeros_like(x)) for h in range(GLA_HEADS)], axis=0)

    dirs = ((qkf_ref, rf_ref, df_ref, vf_ref, of_ref, sf_scr),
            (qkb_ref, rb_ref, db_ref, vb_ref, ob_ref, sb_scr))
    for step in range(BLK_PER_GROUP):
        for b in range(n_seq):
            for d, (qk_ref, r_ref, d_ref, v_ref, o_ref, s_scr) in enumerate(dirs):
                j = step if d == 0 else BLK_PER_GROUP - 1 - step
                rows = slice(j * GLA_BLK, (j + 1) * GLA_BLK)
                state_t = s_scr[b]
                lhs1 = head_stack(qk_ref[b, rows, :GLA_QK_WIDTH], head_lanes)
                lhs2 = head_stack(r_ref[b, rows, :], a_lanes[d * GLA_HEADS:(d + 1) * GLA_HEADS])
                vblk = v_ref[b, rows, :]
                vstack = jnp.concatenate([vblk[:, h * GLA_DV:(h + 1) * GLA_DV] for h in range(GLA_HEADS)], axis=0)
                vv = jnp.concatenate([vstack, vstack], axis=0)
                o = _dot_nt(lhs1, state_t.astype(_BF)) + _dot(lhs2, vv)
                for h in range(GLA_HEADS):
                    o_ref[h, b, rows, :] = o[h * GLA_BLK:(h + 1) * GLA_BLK, :]
                kbd = head_stack(qk_ref[b, rows, GLA_QK_WIDTH:], head_lanes)
                vst = vstack.astype(_F32).T.astype(_BF)
                s_scr[b] = state_t * d_ref[b, j:j + 1, :] + _dot(vst, kbd)


def _gla_scan(qk, r, dec, gv, s0, batch, seq):
    ng = seq // GLA_GROUP
    nbs = SCAN_SEQS if batch % SCAN_SEQS == 0 else 1
    qk = qk.reshape(batch, seq, N_DIR * 2 * GLA_QK_WIDTH)
    r = r.reshape(batch, seq, LANES)
    dec = dec.reshape(batch, seq // GLA_BLK, N_DIR * GLA_QK_WIDTH)
    gv = gv.reshape(batch, seq, GLA_WIDTH)
    fwd = lambda i: i
    bwd = lambda i: ng - 1 - i
    o_shape = jax.ShapeDtypeStruct((GLA_HEADS, batch, seq, GLA_DV), _F32)

    def spec(rows, width, grp, col):
        return pl.BlockSpec((nbs, rows, width), lambda b, i: (b, grp(i), col))

    o_f, o_b = pl.pallas_call(
        _gla_scan_kernel,
        grid=(batch // nbs, ng),
        in_specs=[spec(GLA_GROUP, 2 * GLA_QK_WIDTH, fwd, 0), spec(GLA_GROUP, 2 * GLA_QK_WIDTH, bwd, 1),
                  spec(GLA_GROUP, LANES, fwd, 0), spec(GLA_GROUP, LANES, bwd, 0),
                  spec(BLK_PER_GROUP, GLA_QK_WIDTH, fwd, 0), spec(BLK_PER_GROUP, GLA_QK_WIDTH, bwd, 1),
                  spec(GLA_GROUP, GLA_WIDTH, fwd, 0), spec(GLA_GROUP, GLA_WIDTH, bwd, 0),
                  pl.BlockSpec((GLA_DV, GLA_QK_WIDTH), lambda b, i: (0, 0))],
        out_specs=[pl.BlockSpec((GLA_HEADS, nbs, GLA_GROUP, GLA_DV), lambda b, i: (0, b, fwd(i), 0)),
                   pl.BlockSpec((GLA_HEADS, nbs, GLA_GROUP, GLA_DV), lambda b, i: (0, b, bwd(i), 0))],
        out_shape=[o_shape, o_shape],
        scratch_shapes=[pltpu.VMEM((nbs, GLA_DV, GLA_QK_WIDTH), _F32), pltpu.VMEM((nbs, GLA_DV, GLA_QK_WIDTH), _F32)],
        compiler_params=pltpu.CompilerParams(dimension_semantics=("arbitrary", "arbitrary"),
                                             vmem_limit_bytes=VMEM_LIMIT),
        name="gla_scan",
    )(qk, qk, r, r, dec, dec, gv, gv, s0)
    return (o_f.reshape(GLA_HEADS, batch * seq, GLA_DV), o_b.reshape(GLA_HEADS, batch * seq, GLA_DV))


def _stage_c_kernel(h1_ref, oatt_ref, of_ref, ob_ref, gg_ref, ng_ref, wout_ref, gm_ref, bm_ref,
                    wg_ref, wu_ref, wd_ref, g2_ref, b2_ref, out_ref, act_ref):
    def steps(rows):
        pieces = []
        for h in range(GLA_HEADS):
            o = of_ref[h, rows, :] + ob_ref[h, rows, :]
            o = o * lax.rsqrt(jnp.mean(o * o, axis=-1, keepdims=True) + LN_EPS) * ng_ref[...]
            pieces.append((o * _silu(gg_ref[rows, h * GLA_DV:(h + 1) * GLA_DV])).astype(_BF))
        mix = jnp.concatenate([oatt_ref[rows, :]] + pieces, axis=1)
        yield
        h2 = yield from _residual_ln_steps(h1_ref[rows, :], _dot(mix, wout_ref[...]), gm_ref, bm_ref)
        yield
        out_ref[rows, :] = yield from _swiglu_ln_steps(h2, wg_ref, wu_ref, wd_ref, g2_ref, b2_ref,
                                                       act_ref.at[rows])

    _run_skewed([steps(rows) for rows in _sub_tiles(h1_ref.shape[0])], SUB_TILE_SKEW)


def _stage_c(h1, oatt, o_f, o_b, gg, ng, wout, gm, bm, wg, wu, wd, g2, b2):
    m, d = h1.shape
    tm = min(ROW_TILE, m)
    assert m % tm == 0
    n_ff = wg.shape[1]
    row = lambda w: pl.BlockSpec((tm, w), lambda i: (i, 0))
    hrow = pl.BlockSpec((GLA_HEADS, tm, GLA_DV), lambda i: (0, i, 0))
    return pl.pallas_call(
        _stage_c_kernel,
        grid=(m // tm,),
        in_specs=[row(d), row(ATT_WIDTH), hrow, hrow, row(GLA_WIDTH), _const_spec((1, GLA_DV)),
                  _const_spec(wout.shape), _const_spec((1, d)), _const_spec((1, d)),
                  _const_spec((d, n_ff)), _const_spec((d, n_ff)), _const_spec((n_ff, d)),
                  _const_spec((1, d)), _const_spec((1, d))],
        out_specs=row(d),
        out_shape=jax.ShapeDtypeStruct((m, d), _F32),
        scratch_shapes=[pltpu.VMEM((tm, n_ff), _BF)],
        compiler_params=pltpu.CompilerParams(dimension_semantics=("arbitrary",), vmem_limit_bytes=VMEM_LIMIT),
        name="stage_c",
    )(h1, oatt, o_f, o_b, gg, ng, wout, gm, bm, wg, wu, wd, g2, b2)


def _projection_weight(w_in):
    aq = w_in[:, 0:512]
    ak = w_in[:, 512:640]
    rest = w_in[:, 640:2304]
    gz = w_in[:, 2304:2336]
    ak_dup = jnp.concatenate([ak[:, 0:64], ak[:, 0:64], ak[:, 64:128], ak[:, 64:128]], axis=1)
    gz3 = jnp.pad(jnp.concatenate([gz, gz, gz], axis=1), ((0, 0), (0, LANES - 3 * gz.shape[1])))
    return jnp.concatenate([aq, ak_dup, rest, gz3], axis=1).astype(_BF)


def _gate_weight(w2, bias):
    k = N_DIR * GLA_RANK
    wcat = jnp.zeros((k, N_DIR * GLA_QK_WIDTH), _F32)
    for n in range(N_DIR):
        wcat = wcat.at[n * GLA_RANK:(n + 1) * GLA_RANK, n * GLA_QK_WIDTH:(n + 1) * GLA_QK_WIDTH].set(w2[n])
    hi = wcat.astype(_BF)
    mid = (wcat - hi.astype(_F32)).astype(_BF)
    pad = jnp.zeros((LANES - 3 * k, N_DIR * GLA_QK_WIDTH), _BF)
    return jnp.concatenate([hi, hi, mid, pad], axis=0), bias.reshape(1, N_DIR * GLA_QK_WIDTH)


def _pad_rows(a, rows):
    return jnp.pad(a, ((0, rows - a.shape[0]), (0, 0)))


def kernel(x, meta_tokens, ln_in_g, ln_in_b, ffn1_ln_g, ffn1_ln_b, ffn1_w_gate, ffn1_w_up, ffn1_w_down, w_in,
           attn_sink, gla_gate_w2, gla_gate_b, gla_norm_g, w_out, mix_ln_g, mix_ln_b, ffn2_w_gate, ffn2_w_up,
           ffn2_w_down, ffn2_ln_g, ffn2_ln_b):
    batch, seq, d = x.shape
    assert seq % BLOCK == 0 and seq >= _KWIN and seq % GLA_GROUP == 0
    vec = lambda a: a.reshape(1, -1)
    x2d = x.reshape(batch * seq, d)
    wg1, wu1, wd1 = ffn1_w_gate[0].astype(_BF), ffn1_w_up[0].astype(_BF), ffn1_w_down[0].astype(_BF)
    wg2, wu2, wd2 = ffn2_w_gate[0].astype(_BF), ffn2_w_up[0].astype(_BF), ffn2_w_down[0].astype(_BF)
    win = _projection_weight(w_in[0])
    w2s, gbias = _gate_weight(gla_gate_w2[0], gla_gate_b[0])
    sel = jnp.asarray(_sel_matrix(), _BF)

    a_args = (vec(ln_in_g), vec(ln_in_b), wg1, wu1, wd1, vec(ffn1_ln_g[0]), vec(ffn1_ln_b[0]), win)
    h1, aq, ak2, avt, gq, gk, gv, gg, gz = _stage_a(x2d, *a_args)
    _, _, ak2_m, avt_m, _, gk_m, gv_m, _, gz_m = _stage_a(meta_tokens.astype(x.dtype), *a_args)

    vmt = jnp.pad(avt_m, ((0, 0), (0, META_ROWS - avt_m.shape[1])))
    o_att = _attention(aq, ak2, avt, _pad_rows(ak2_m, META_ROWS), vmt, attn_sink[0].astype(_F32), batch, seq)

    qk, dec, r = _gla_prep(gq, gk, gz, w2s, gbias, sel)
    s0 = _gla_meta_state(_pad_rows(gk_m, LANES), _pad_rows(gz_m, LANES), _pad_rows(gv_m, LANES), w2s, gbias)
    o_f, o_b = _gla_scan(qk, r, dec, gv, s0, batch, seq)

    out = _stage_c(h1, o_att, o_f, o_b, gg, vec(gla_norm_g[0]), w_out[0].astype(_BF),
                   vec(mix_ln_g[0]), vec(mix_ln_b[0]), wg2, wu2, wd2, vec(ffn2_ln_g[0]), vec(ffn2_ln_b[0]))
    return out.reshape(batch, seq, d)
```

```python
import functools

import numpy as np
import jax
import jax.numpy as jnp
from jax import lax
from jax.experimental import pallas as pl
from jax.experimental.pallas import tpu as pltpu

N_META = 16
ATT_HEADS = 8
ATT_KV_HEADS = 2
ATT_HEAD_DIM = 64
ATT_REP = ATT_HEADS // ATT_KV_HEADS
WINDOW = 128
BLOCK = 128
GLA_HEADS = 4
GLA_DK = 64
GLA_DV = 128
GLA_RANK = 16
GLA_TAU = 16.0
N_DIR = 2
ATT_WIDTH = ATT_HEADS * ATT_HEAD_DIM
ATT_KV_WIDTH = ATT_KV_HEADS * ATT_HEAD_DIM
GLA_QK_WIDTH = GLA_HEADS * GLA_DK
GLA_WIDTH = GLA_HEADS * GLA_DV
LN_EPS = 1e-5
DEPTH = 1
DN_ALPHA = (2.0 * DEPTH) ** 0.25

LANES = 128
GLA_BLK = 16
GLA_GROUP = 128
BLK_PER_GROUP = GLA_GROUP // GLA_BLK
SCAN_SEQS = 8
FF_CHUNK = 256
ROW_TILE = 512
ROW_SUB_TILES = 2
SUB_TILE_SKEW = 3
LN_PIECES = 4
PM_UNIT = ROW_TILE // ROW_SUB_TILES
PREP_TILE = 2048
VMEM_LIMIT = 56 * 1024 * 1024

_BF = jnp.bfloat16
_F32 = jnp.float32


def _dot(a, b):
    return jnp.dot(a, b, preferred_element_type=_F32)


def _dot_nt(a, b):
    return lax.dot_general(a, b, (((1,), (1,)), ((), ())), preferred_element_type=_F32)


def _layer_norm(x, g, b):
    mu = jnp.mean(x, axis=-1, keepdims=True)
    xc = x - mu
    var = jnp.mean(xc * xc, axis=-1, keepdims=True)
    return xc * lax.rsqrt(var + LN_EPS) * g + b


def _silu(x):
    return x * (1.0 / (1.0 + jnp.exp(-x)))


def _split3(x):
    hi = x.astype(_BF)
    r1 = x - hi.astype(_F32)
    mid = r1.astype(_BF)
    lo = (r1 - mid.astype(_F32)).astype(_BF)
    return hi, mid, lo


def _sub_tiles(rows):
    n = ROW_SUB_TILES if rows % (ROW_SUB_TILES * 16) == 0 else 1
    return [pl.ds(s * (rows // n), rows // n) for s in range(n)]


def _run_skewed(step_gens, skew):
    pending = list(enumerate(step_gens))
    tick = 0
    while pending:
        for k, gen in list(pending):
            if tick < k * skew:
                continue
            try:
                next(gen)
            except StopIteration:
                pending.remove((k, gen))
        tick += 1


def _swiglu_ln_steps(h, wg_ref, wu_ref, wd_ref, g_ref, b_ref, act_ref):
    hb = h.astype(_BF)
    n_ff = wg_ref.shape[1]
    for c in range(n_ff // FF_CHUNK):
        sl = slice(c * FF_CHUNK, (c + 1) * FF_CHUNK)
        gate = _dot(hb, wg_ref[:, sl])
        up = _dot(hb, wu_ref[:, sl])
        act_ref[:, sl] = (_silu(gate) * up).astype(_BF)
        yield
    y = _dot(act_ref[...], wd_ref[...])
    return (yield from _residual_ln_steps(h, 0.5 * y, g_ref, b_ref))


def _residual_ln_steps(h, branch, g_ref, b_ref):
    rows = h.shape[0]
    piece = rows // LN_PIECES if rows % (LN_PIECES * 8) == 0 else rows
    out = []
    for r in range(0, rows, piece):
        out.append(_layer_norm(DN_ALPHA * h[r:r + piece] + branch[r:r + piece], g_ref[...], b_ref[...]))
        if r + piece < rows:
            yield
    return out[0] if len(out) == 1 else jnp.concatenate(out, axis=0)


_P_AQ = (0, 512)
_P_AK = (512, 768)
_P_AV = (768, 896)
_P_GQ = (896, 1152)
_P_GK = (1152, 1408)
_P_GV = (1408, 1920)
_P_GG = (1920, 2432)
_P_GZ = (2432, 2560)
_P_WIDTH = 2560


def _position_major_perm(rows):
    nblk = rows // GLA_BLK
    perm = np.zeros((rows, rows), np.float32)
    for n in range(nblk):
        for t in range(GLA_BLK):
            perm[t * nblk + n, n * GLA_BLK + t] = 1.0
    return perm


def _stage_a_kernel(x_ref, g0_ref, b0_ref, wg_ref, wu_ref, wd_ref, g1_ref, b1_ref, win_ref, perm_ref,
                    h1_ref, aq_ref, ak_ref, avt_ref, gq_ref, gk_ref, gv_ref, gg_ref, gz_ref, act_ref):
    def steps(rows):
        h0 = _layer_norm(x_ref[rows, :], g0_ref[...], b0_ref[...])
        yield
        h1 = yield from _swiglu_ln_steps(h0, wg_ref, wu_ref, wd_ref, g1_ref, b1_ref, act_ref.at[rows])
        h1_ref[rows, :] = h1
        hb = h1.astype(_BF)
        hb_pm = _dot(perm_ref[...], hb).astype(_BF)
        yield

        def proj(act, cols):
            return _dot(act, win_ref[:, cols[0]:cols[1]])

        aq_ref[rows, :] = (proj(hb, _P_AQ) * (ATT_HEAD_DIM ** -0.5)).astype(_BF)
        ak_ref[rows, :] = proj(hb, _P_AK).astype(_BF)
        avt_ref[:, rows] = proj(hb, _P_AV).T.astype(_BF)
        gq_ref[rows, :] = proj(hb_pm, _P_GQ) * (GLA_DK ** -0.5)
        yield
        gk_ref[rows, :] = proj(hb_pm, _P_GK)
        gv_ref[rows, :] = proj(hb, _P_GV).astype(_BF)
        gg_ref[rows, :] = proj(hb, _P_GG)
        gz_ref[rows, :] = proj(hb_pm, _P_GZ)

    _run_skewed([steps(rows) for rows in _sub_tiles(x_ref.shape[0])], SUB_TILE_SKEW)


def _const_spec(shape):
    nd = len(shape)
    return pl.BlockSpec(shape, lambda *_: (0,) * nd, pipeline_mode=pl.Buffered(1))


def _stage_a(x2d, g0, b0, wg, wu, wd, g1, b1, win):
    m, d = x2d.shape
    tm = min(ROW_TILE, m)
    assert m % tm == 0
    n_ff = wg.shape[1]
    sub = _sub_tiles(tm)[0].size
    perm = jnp.asarray(_position_major_perm(sub), _BF)
    row = lambda w: pl.BlockSpec((tm, w), lambda i: (i, 0))
    out_widths = [(d, _F32), (512, _BF), (256, _BF), None, (256, _F32), (256, _F32), (512, _BF), (512, _F32),
                  (128, _F32)]
    out_specs = [row(o[0]) if o else pl.BlockSpec((ATT_KV_WIDTH, tm), lambda i: (0, i)) for o in out_widths]
    out_shape = [jax.ShapeDtypeStruct((m, o[0]), o[1]) if o else jax.ShapeDtypeStruct((ATT_KV_WIDTH, m), _BF)
                 for o in out_widths]
    return pl.pallas_call(
        _stage_a_kernel,
        grid=(m // tm,),
        in_specs=[row(d), _const_spec((1, d)), _const_spec((1, d)), _const_spec((d, n_ff)), _const_spec((d, n_ff)),
                  _const_spec((n_ff, d)), _const_spec((1, d)), _const_spec((1, d)), _const_spec((d, _P_WIDTH)),
                  _const_spec((sub, sub))],
        out_specs=out_specs,
        out_shape=out_shape,
        scratch_shapes=[pltpu.VMEM((tm, n_ff), _BF)],
        compiler_params=pltpu.CompilerParams(dimension_semantics=("arbitrary",), vmem_limit_bytes=VMEM_LIMIT),
        name="stage_a",
    )(x2d, g0, b0, wg, wu, wd, g1, b1, win, perm)


_KWIN = 3 * BLOCK


def _alibi_slopes():
    return [2.0 ** (-8.0 * (i + 1) / ATT_HEADS) for i in range(ATT_HEADS)]


_WIN_OFFSETS = (0, -BLOCK, -2 * BLOCK)
ATT_QBLOCKS = 8
META_ROWS = 32


def _attn_kernel(sink_ref, q_ref, k_ref, vt_ref, km_ref, vmt_ref, o_ref, band_scr, meta_scr):
    i = pl.program_id(1)
    s_len = k_ref.shape[0]
    nq = s_len // BLOCK
    pairs_per_group = ATT_REP // 2

    @pl.when((pl.program_id(0) == 0) & (i == 0))
    def _():
        col = lax.broadcasted_iota(jnp.int32, (_KWIN, 2 * BLOCK), 1)
        rel0 = lax.broadcasted_iota(jnp.int32, (_KWIN, 2 * BLOCK), 0) - jnp.where(col < BLOCK, col, col - BLOCK)
        mrow = lax.broadcasted_iota(jnp.int32, (META_ROWS, 2 * BLOCK), 0)
        mcol = lax.broadcasted_iota(jnp.int32, (META_ROWS, 2 * BLOCK), 1)
        slopes = _alibi_slopes()
        for g in range(ATT_KV_HEADS):
            for pl_ in range(pairs_per_group):
                h0 = g * ATT_REP + 2 * pl_
                slope = jnp.where(col < BLOCK, slopes[h0], slopes[h0 + 1])
                for var, off in enumerate(_WIN_OFFSETS):
                    absrel = jnp.abs(rel0 + off).astype(_F32)
                    band_scr[var, g, pl_] = jnp.where(absrel <= float(WINDOW), -slope * absrel, -jnp.inf)
                sink = jnp.where(mcol < BLOCK, sink_ref[h0], sink_ref[h0 + 1])
                meta_scr[g, pl_] = jnp.where(mrow < N_META, 0.0, jnp.where(mrow == N_META, sink, -jnp.inf))

    n_blk = q_ref.shape[0] // BLOCK
    starts, variants = [], []
    for blk in range(n_blk):
        ib = i * n_blk + blk
        starts.append(pl.multiple_of(jnp.clip((ib - 1) * BLOCK, 0, s_len - _KWIN), BLOCK))
        variants.append(jnp.where(ib == 0, 0, jnp.where(ib == nq - 1, 2, 1)))
    low_half = lax.broadcasted_iota(jnp.int32, (BLOCK, LANES), 1) < ATT_HEAD_DIM
    zero = jnp.zeros((BLOCK, LANES), _BF)

    chains = [(blk, g, pl_) for blk in range(n_blk) for g in range(ATT_KV_HEADS) for pl_ in range(pairs_per_group)]

    def scores(blk, g, pl_):
        lanes_g = slice(g * LANES, (g + 1) * LANES)
        pair = g * pairs_per_group + pl_
        qp = q_ref[blk * BLOCK:(blk + 1) * BLOCK, pair * LANES:(pair + 1) * LANES]
        qs = jnp.concatenate([jnp.where(low_half, qp, zero), jnp.where(low_half, zero, qp)], axis=0)
        kg = k_ref[pl.ds(starts[blk], _KWIN), lanes_g]
        s_band = _dot_nt(kg, qs) + band_scr[variants[blk], g, pl_]
        s_meta = _dot_nt(km_ref[:, lanes_g], qs) + meta_scr[g, pl_]
        return s_band, s_meta

    def rows_reduce(op, arrays):
        parts = [a[r:r + 8, :] for a in arrays for r in range(0, a.shape[0], 8)]
        while len(parts) > 1:
            parts = [op(parts[j], parts[j + 1]) if j + 1 < len(parts) else parts[j]
                     for j in range(0, len(parts), 2)]
        return parts[0]

    def col_max(s_band, s_meta):
        return s_band, s_meta, jnp.max(rows_reduce(jnp.maximum, [s_band, s_meta]), axis=0, keepdims=True)

    def softmax(s_band, s_meta, mx):
        p_band = jnp.exp(s_band - mx)
        p_meta = jnp.exp(s_meta - mx)
        den = jnp.sum(rows_reduce(jnp.add, [p_band, p_meta]), axis=0, keepdims=True)
        return p_band.astype(_BF), p_meta.astype(_BF), den

    def values(blk, g, pl_, p_band, p_meta, den):
        drows = slice(g * ATT_HEAD_DIM, (g + 1) * ATT_HEAD_DIM)
        vt = vt_ref[drows, pl.ds(starts[blk], _KWIN)]
        ot = (_dot(vt, p_band) + _dot(vmt_ref[drows, :], p_meta)) * (1.0 / den)
        o2 = jnp.concatenate([ot[:, :BLOCK], ot[:, BLOCK:]], axis=0)
        pair = g * pairs_per_group + pl_
        o_ref[blk * BLOCK:(blk + 1) * BLOCK, pair * LANES:(pair + 1) * LANES] = o2.T.astype(_BF)

    s_vals, m_vals, p_vals = {}, {}, {}
    for t in range(len(chains) + 3):
        if t < len(chains):
            s_vals[t] = scores(*chains[t])
        if 0 <= t - 1 < len(chains):
            m_vals[t - 1] = col_max(*s_vals.pop(t - 1))
        if 0 <= t - 2 < len(chains):
            p_vals[t - 2] = softmax(*m_vals.pop(t - 2))
        if 0 <= t - 3 < len(chains):
            values(*chains[t - 3], *p_vals.pop(t - 3))


def _attention(aq, ak2, avt, km2, vmt, sink, batch, seq):
    m = aq.shape[0]
    nblk = ATT_QBLOCKS if (seq // BLOCK) % ATT_QBLOCKS == 0 else 1
    nq = seq // (BLOCK * nblk)
    return pl.pallas_call(
        _attn_kernel,
        grid=(batch, nq),
        in_specs=[pl.BlockSpec(memory_space=pltpu.SMEM),
                  pl.BlockSpec((nblk * BLOCK, ATT_WIDTH), lambda b, i: (b * nq + i, 0)),
                  pl.BlockSpec((seq, 2 * LANES), lambda b, i: (b, 0)),
                  pl.BlockSpec((ATT_KV_WIDTH, seq), lambda b, i: (0, b)),
                  pl.BlockSpec((META_ROWS, 2 * LANES), lambda b, i: (0, 0)),
                  pl.BlockSpec((ATT_KV_WIDTH, META_ROWS), lambda b, i: (0, 0))],
        out_specs=pl.BlockSpec((nblk * BLOCK, ATT_WIDTH), lambda b, i: (b * nq + i, 0)),
        out_shape=jax.ShapeDtypeStruct((m, ATT_WIDTH), _BF),
        scratch_shapes=[pltpu.VMEM((len(_WIN_OFFSETS), ATT_KV_HEADS, ATT_REP // 2, _KWIN, 2 * BLOCK), _F32),
                        pltpu.VMEM((ATT_KV_HEADS, ATT_REP // 2, META_ROWS, 2 * BLOCK), _F32)],
        compiler_params=pltpu.CompilerParams(dimension_semantics=("arbitrary", "arbitrary"),
                                             vmem_limit_bytes=VMEM_LIMIT),
        name="attention",
    )(sink, aq, ak2, avt, km2, vmt)


_LOG2E = 1.4426950408889634


def _gla_log2_gates(z, w2_ref, bias_ref):
    zh = z.astype(_BF)
    zm = (z - zh.astype(_F32)).astype(_BF)
    lane = lax.broadcasted_iota(jnp.int32, z.shape, 1)
    mid_lanes = (lane >= N_DIR * GLA_RANK) & (lane < 2 * N_DIR * GLA_RANK)
    logits = _dot(jnp.where(mid_lanes, zm, zh), w2_ref[...]) + bias_ref[...]
    y = logits * _LOG2E
    log2_sig = jnp.minimum(y, 0.0) - jnp.log2(1.0 + jnp.exp2(-jnp.abs(y)))
    return log2_sig * (1.0 / GLA_TAU)


def _sel_matrix():
    sel = np.zeros((N_DIR, 2, GLA_BLK, LANES, LANES), np.float32)
    for d in range(N_DIR):
        for lg in range(2):
            for s in range(GLA_BLK):
                for hl in range(2):
                    col = d * 64 + (2 * lg + hl) * GLA_BLK + s
                    sel[d, lg, s, hl * GLA_DK:(hl + 1) * GLA_DK, col] = 1.0
    return sel.reshape(N_DIR, 2 * GLA_BLK * LANES, LANES)


def _gla_prep_kernel(q0_ref, q1_ref, k0_ref, k1_ref, z_ref, w2_ref, bias_ref, sel_ref, perm_ref,
                     qk_ref, dec_ref, r_ref,
                     g_scr, b_scr, kbf_scr, qk_scr, dec_scr):
    tt_rows = z_ref.shape[0]
    units = tt_rows // PM_UNIT
    ub = PM_UNIT // GLA_BLK
    rc = min(512, tt_rows)
    for c in range(tt_rows // rc):
        rows = pl.ds(c * rc, rc)
        g = _gla_log2_gates(z_ref[rows, :], w2_ref, bias_ref)
        for gi in range(4):
            g_scr[gi, rows, :] = g[:, gi * LANES:(gi + 1) * LANES]

    def slab(ref, tt, lead=()):
        return jnp.concatenate([ref[lead + (pl.ds(u * PM_UNIT + tt * ub, ub), slice(None))] for u in range(units)],
                               axis=0)

    def store_slab(ref, lead, tt, val):
        for u in range(units):
            ref[lead + (pl.ds(u * PM_UNIT + tt * ub, ub), slice(None))] = val[u * ub:(u + 1) * ub, :]

    qk_refs = ((q0_ref, k0_ref), (q1_ref, k1_ref))
    last = {}
    for lg, (_, k_ref) in enumerate(qk_refs):
        for tt in range(GLA_BLK):
            kbf_scr[lg, tt] = slab(k_ref, tt).astype(_BF)
        for d in range(N_DIR):
            gi = d * 2 + lg
            acc = None
            for tt in (range(GLA_BLK) if d == 0 else reversed(range(GLA_BLK))):
                gs = slab(g_scr, tt, (gi,))
                acc = gs if acc is None else acc + gs
                b_scr[gi, tt] = acc
            last[gi] = acc
            dec_scr[gi] = jnp.exp2(acc)
    for tt in range(GLA_BLK):
        prods, sel_rows = [], []
        for d in range(N_DIR):
            positions = list(range(0, tt + 1)) if d == 0 else list(range(tt + 1, GLA_BLK))
            for lg, (q_ref, k_ref) in enumerate(qk_refs):
                gi = d * 2 + lg
                q = slab(q_ref, tt)
                b = b_scr[gi, tt]
                store_slab(qk_scr, (d * 4 + lg,), tt, q * jnp.exp2(b))
                store_slab(qk_scr, (d * 4 + 2 + lg,), tt, slab(k_ref, tt) * jnp.exp2(last[gi] - b))
                if not positions:
                    continue
                qb = q.astype(_BF)
                for s in positions:
                    p = qb * kbf_scr[lg, s]
                    if s != tt:
                        p = p * jnp.exp2(b - b_scr[gi, s]).astype(_BF)
                    prods.append(p)
                r0 = lg * GLA_BLK * LANES + positions[0] * LANES
                sel_rows.append(sel_ref[d, r0:r0 + len(positions) * LANES, :])
        a_tt = _dot(jnp.concatenate(prods, axis=1), jnp.concatenate(sel_rows, axis=0))
        store_slab(qk_scr, (8,), tt, a_tt)
    for u in range(units):
        rows = pl.ds(u * PM_UNIT, PM_UNIT)
        x = jnp.concatenate([qk_scr[c, rows, :] for c in range(9)], axis=1).astype(_BF)
        y = _dot(perm_ref[...], x).astype(_BF)
        qk_ref[rows, :] = y[:, :8 * LANES]
        r_ref[rows, :] = y[:, 8 * LANES:]
    for gi in range(4):
        dec_ref[:, gi * LANES:(gi + 1) * LANES] = dec_scr[gi]


def _gla_prep(gq, gk, gz, w2s, bias, sel):
    m = gq.shape[0]
    tt = min(PREP_TILE, m)
    assert m % tt == 0 and tt % PM_UNIT == 0
    nb = tt // GLA_BLK
    perm = jnp.asarray(_position_major_perm(PM_UNIT).T, _BF)
    col = lambda j: pl.BlockSpec((tt, LANES), lambda i: (i, j))
    return pl.pallas_call(
        _gla_prep_kernel,
        grid=(m // tt,),
        in_specs=[col(0), col(1), col(0), col(1), col(0),
                  _const_spec(w2s.shape), _const_spec(bias.shape), _const_spec(sel.shape),
                  _const_spec((PM_UNIT, PM_UNIT))],
        out_specs=[pl.BlockSpec((tt, 8 * LANES), lambda i: (i, 0)),
                   pl.BlockSpec((nb, 4 * LANES), lambda i: (i, 0)),
                   pl.BlockSpec((tt, LANES), lambda i: (i, 0))],
        out_shape=[jax.ShapeDtypeStruct((m, 8 * LANES), _BF),
                   jax.ShapeDtypeStruct((m // GLA_BLK, 4 * LANES), _F32),
                   jax.ShapeDtypeStruct((m, LANES), _BF)],
        scratch_shapes=[pltpu.VMEM((4, tt, LANES), _F32),
                        pltpu.VMEM((4, GLA_BLK, nb, LANES), _F32),
                        pltpu.VMEM((2, GLA_BLK, nb, LANES), _BF),
                        pltpu.VMEM((9, tt, LANES), _F32),
                        pltpu.VMEM((4, nb, LANES), _F32)],
        compiler_params=pltpu.CompilerParams(dimension_semantics=("arbitrary",), vmem_limit_bytes=VMEM_LIMIT),
        name="gla_prep",
    )(gq, gq, gk, gk, gz, w2s, bias, sel, perm)


def _gla_meta_state_kernel(k_ref, z_ref, v_ref, w2_ref, bias_ref, s0_ref):
    g = _gla_log2_gates(z_ref[...], w2_ref, bias_ref)[:, :GLA_QK_WIDTH]
    r = lax.broadcasted_iota(jnp.int32, (LANES, LANES), 0)
    c = lax.broadcasted_iota(jnp.int32, (LANES, LANES), 1)
    tail = jnp.where((c > r) & (c < N_META), 1.0, 0.0).astype(_BF)
    gh, gm, gl = _split3(g)
    rest = _dot(tail, gh) + _dot(tail, gm) + _dot(tail, gl)
    is_meta = lax.broadcasted_iota(jnp.int32, (LANES, GLA_QK_WIDTH), 0) < N_META
    ke = jnp.where(is_meta, k_ref[...] * jnp.exp2(jnp.where(is_meta, rest, 0.0)), 0.0).astype(_BF)
    lane = lax.broadcasted_iota(jnp.int32, (GLA_DV, GLA_QK_WIDTH), 1)
    v = v_ref[...].astype(_F32)
    s0 = jnp.zeros((GLA_DV, GLA_QK_WIDTH), _F32)
    for h in range(GLA_HEADS):
        vt = v[:, h * GLA_DV:(h + 1) * GLA_DV].T.astype(_BF)
        in_head = (lane >= h * GLA_DK) & (lane < (h + 1) * GLA_DK)
        s0 = s0 + jnp.where(in_head, _dot(vt, ke), 0.0)
    s0_ref[...] = s0


def _gla_meta_state(gk_m, gz_m, gv_m, w2s, bias):
    return pl.pallas_call(
        _gla_meta_state_kernel,
        out_shape=jax.ShapeDtypeStruct((GLA_DV, GLA_QK_WIDTH), _F32),
        name="gla_meta_state",
    )(gk_m, gz_m, gv_m, w2s, bias)


def _gla_scan_kernel(qkf_ref, qkb_ref, rf_ref, rb_ref, df_ref, db_ref, vf_ref, vb_ref, s0_ref,
                     of_ref, ob_ref, sf_scr, sb_scr):
    n_seq = sf_scr.shape[0]

    @pl.when(pl.program_id(1) == 0)
    def _():
        for b in range(n_seq):
            sf_scr[b] = s0_ref[...]
        sb_scr[...] = jnp.zeros_like(sb_scr)

    def lane_band(shape, width, n):
        lane = lax.broadcasted_iota(jnp.int32, shape, 1)
        return [(lane >= c * width) & (lane < (c + 1) * width) for c in range(n)]

    head_lanes = lane_band((GLA_BLK, GLA_QK_WIDTH), GLA_DK, GLA_HEADS)
    a_lanes = lane_band((GLA_BLK, LANES), GLA_BLK, N_DIR * GLA_HEADS)

    def head_stack(x, bands):
        return jnp.concatenate([jnp.where(bands[h], x, jnp.zeros_like(x)) for h in range(GLA_HEADS)], axis=0)

    dirs = ((qkf_ref, rf_ref, df_ref, vf_ref, of_ref, sf_scr),
            (qkb_ref, rb_ref, db_ref, vb_ref, ob_ref, sb_scr))
    for step in range(BLK_PER_GROUP):
        for b in range(n_seq):
            for d, (qk_ref, r_ref, d_ref, v_ref, o_ref, s_scr) in enumerate(dirs):
                j = step if d == 0 else BLK_PER_GROUP - 1 - step
                rows = slice(j * GLA_BLK, (j + 1) * GLA_BLK)
                state_t = s_scr[b]
                lhs1 = head_stack(qk_ref[b, rows, :GLA_QK_WIDTH], head_lanes)
                lhs2 = head_stack(r_ref[b, rows, :], a_lanes[d * GLA_HEADS:(d + 1) * GLA_HEADS])
                vblk = v_ref[b, rows, :]
                vstack = jnp.concatenate([vblk[:, h * GLA_DV:(h + 1) * GLA_DV] for h in range(GLA_HEADS)], axis=0)
                vv = jnp.concatenate([vstack, vstack], axis=0)
                o = _dot_nt(lhs1, state_t.astype(_BF)) + _dot(lhs2, vv)
                for h in range(GLA_HEADS):
                    o_ref[h, b, rows, :] = o[h * GLA_BLK:(h + 1) * GLA_BLK, :]
                kbd = head_stack(qk_ref[b, rows, GLA_QK_WIDTH:], head_lanes)
                vst = vstack.astype(_F32).T.astype(_BF)
                s_scr[b] = state_t * d_ref[b, j:j + 1, :] + _dot(vst, kbd)


def _gla_scan(qk, r, dec, gv, s0, batch, seq):
    ng = seq // GLA_GROUP
    nbs = SCAN_SEQS if batch % SCAN_SEQS == 0 else 1
    qk = qk.reshape(batch, seq, N_DIR * 2 * GLA_QK_WIDTH)
    r = r.reshape(batch, seq, LANES)
    dec = dec.reshape(batch, seq // GLA_BLK, N_DIR * GLA_QK_WIDTH)
    gv = gv.reshape(batch, seq, GLA_WIDTH)
    fwd = lambda i: i
    bwd = lambda i: ng - 1 - i
    o_shape = jax.ShapeDtypeStruct((GLA_HEADS, batch, seq, GLA_DV), _F32)

    def spec(rows, width, grp, col):
        return pl.BlockSpec((nbs, rows, width), lambda b, i: (b, grp(i), col))

    o_f, o_b = pl.pallas_call(
        _gla_scan_kernel,
        grid=(batch // nbs, ng),
        in_specs=[spec(GLA_GROUP, 2 * GLA_QK_WIDTH, fwd, 0), spec(GLA_GROUP, 2 * GLA_QK_WIDTH, bwd, 1),
                  spec(GLA_GROUP, LANES, fwd, 0), spec(GLA_GROUP, LANES, bwd, 0),
                  spec(BLK_PER_GROUP, GLA_QK_WIDTH, fwd, 0), spec(BLK_PER_GROUP, GLA_QK_WIDTH, bwd, 1),
                  spec(GLA_GROUP, GLA_WIDTH, fwd, 0), spec(GLA_GROUP, GLA_WIDTH, bwd, 0),
                  pl.BlockSpec((GLA_DV, GLA_QK_WIDTH), lambda b, i: (0, 0))],
        out_specs=[pl.BlockSpec((GLA_HEADS, nbs, GLA_GROUP, GLA_DV), lambda b, i: (0, b, fwd(i), 0)),
                   pl.BlockSpec((GLA_HEADS, nbs, GLA_GROUP, GLA_DV), lambda b, i: (0, b, bwd(i), 0))],
        out_shape=[o_shape, o_shape],
        scratch_shapes=[pltpu.VMEM((nbs, GLA_DV, GLA_QK_WIDTH), _F32), pltpu.VMEM((nbs, GLA_DV, GLA_QK_WIDTH), _F32)],
        compiler_params=pltpu.CompilerParams(dimension_semantics=("arbitrary", "arbitrary"),
                                             vmem_limit_bytes=VMEM_LIMIT),
        name="gla_scan",
    )(qk, qk, r, r, dec, dec, gv, gv, s0)
    return (o_f.reshape(GLA_HEADS, batch * seq, GLA_DV), o_b.reshape(GLA_HEADS, batch * seq, GLA_DV))


def _stage_c_kernel(h1_ref, oatt_ref, of_ref, ob_ref, gg_ref, ng_ref, wout_ref, gm_ref, bm_ref,
                    wg_ref, wu_ref, wd_ref, g2_ref, b2_ref, out_ref, act_ref):
    def steps(rows):
        pieces = []
        for h in range(GLA_HEADS):
            o = of_ref[h, rows, :] + ob_ref[h, rows, :]
            o = o * lax.rsqrt(jnp.mean(o * o, axis=-1, keepdims=True) + LN_EPS) * ng_ref[...]
            pieces.append((o * _silu(gg_ref[rows, h * GLA_DV:(h + 1) * GLA_DV])).astype(_BF))
        mix = jnp.concatenate([oatt_ref[rows, :]] + pieces, axis=1)
        yield
        h2 = yield from _residual_ln_steps(h1_ref[rows, :], _dot(mix, wout_ref[...]), gm_ref, bm_ref)
        yield
        out_ref[rows, :] = yield from _swiglu_ln_steps(h2, wg_ref, wu_ref, wd_ref, g2_ref, b2_ref,
                                                       act_ref.at[rows])

    _run_skewed([steps(rows) for rows in _sub_tiles(h1_ref.shape[0])], SUB_TILE_SKEW)


def _stage_c(h1, oatt, o_f, o_b, gg, ng, wout, gm, bm, wg, wu, wd, g2, b2):
    m, d = h1.shape
    tm = min(ROW_TILE, m)
    assert m % tm == 0
    n_ff = wg.shape[1]
    row = lambda w: pl.BlockSpec((tm, w), lambda i: (i, 0))
    hrow = pl.BlockSpec((GLA_HEADS, tm, GLA_DV), lambda i: (0, i, 0))
    return pl.pallas_call(
        _stage_c_kernel,
        grid=(m // tm,),
        in_specs=[row(d), row(ATT_WIDTH), hrow, hrow, row(GLA_WIDTH), _const_spec((1, GLA_DV)),
                  _const_spec(wout.shape), _const_spec((1, d)), _const_spec((1, d)),
                  _const_spec((d, n_ff)), _const_spec((d, n_ff)), _const_spec((n_ff, d)),
                  _const_spec((1, d)), _const_spec((1, d))],
        out_specs=row(d),
        out_shape=jax.ShapeDtypeStruct((m, d), _F32),
        scratch_shapes=[pltpu.VMEM((tm, n_ff), _BF)],
        compiler_params=pltpu.CompilerParams(dimension_semantics=("arbitrary",), vmem_limit_bytes=VMEM_LIMIT),
        name="stage_c",
    )(h1, oatt, o_f, o_b, gg, ng, wout, gm, bm, wg, wu, wd, g2, b2)


def _projection_weight(w_in):
    aq = w_in[:, 0:512]
    ak = w_in[:, 512:640]
    rest = w_in[:, 640:2304]
    gz = w_in[:, 2304:2336]
    ak_dup = jnp.concatenate([ak[:, 0:64], ak[:, 0:64], ak[:, 64:128], ak[:, 64:128]], axis=1)
    gz3 = jnp.pad(jnp.concatenate([gz, gz, gz], axis=1), ((0, 0), (0, LANES - 3 * gz.shape[1])))
    return jnp.concatenate([aq, ak_dup, rest, gz3], axis=1).astype(_BF)


def _gate_weight(w2, bias):
    k = N_DIR * GLA_RANK
    wcat = jnp.zeros((k, N_DIR * GLA_QK_WIDTH), _F32)
    for n in range(N_DIR):
        wcat = wcat.at[n * GLA_RANK:(n + 1) * GLA_RANK, n * GLA_QK_WIDTH:(n + 1) * GLA_QK_WIDTH].set(w2[n])
    hi = wcat.astype(_BF)
    mid = (wcat - hi.astype(_F32)).astype(_BF)
    pad = jnp.zeros((LANES - 3 * k, N_DIR * GLA_QK_WIDTH), _BF)
    return jnp.concatenate([hi, hi, mid, pad], axis=0), bias.reshape(1, N_DIR * GLA_QK_WIDTH)


def _pad_rows(a, rows):
    return jnp.pad(a, ((0, rows - a.shape[0]), (0, 0)))


def kernel(x, meta_tokens, ln_in_g, ln_in_b, ffn1_ln_g, ffn1_ln_b, ffn1_w_gate, ffn1_w_up, ffn1_w_down, w_in,
           attn_sink, gla_gate_w2, gla_gate_b, gla_norm_g, w_out, mix_ln_g, mix_ln_b, ffn2_w_gate, ffn2_w_up,
           ffn2_w_down, ffn2_ln_g, ffn2_ln_b):
    batch, seq, d = x.shape
    assert seq % BLOCK == 0 and seq >= _KWIN and seq % GLA_GROUP == 0
    vec = lambda a: a.reshape(1, -1)
    x2d = x.reshape(batch * seq, d)
    wg1, wu1, wd1 = ffn1_w_gate[0].astype(_BF), ffn1_w_up[0].astype(_BF), ffn1_w_down[0].astype(_BF)
    wg2, wu2, wd2 = ffn2_w_gate[0].astype(_BF), ffn2_w_up[0].astype(_BF), ffn2_w_down[0].astype(_BF)
    win = _projection_weight(w_in[0])
    w2s, gbias = _gate_weight(gla_gate_w2[0], gla_gate_b[0])
    sel = jnp.asarray(_sel_matrix(), _BF)

    a_args = (vec(ln_in_g), vec(ln_in_b), wg1, wu1, wd1, vec(ffn1_ln_g[0]), vec(ffn1_ln_b[0]), win)
    h1, aq, ak2, avt, gq, gk, gv, gg, gz = _stage_a(x2d, *a_args)
    _, _, ak2_m, avt_m, _, gk_m, gv_m, _, gz_m = _stage_a(meta_tokens.astype(x.dtype), *a_args)

    vmt = jnp.pad(avt_m, ((0, 0), (0, META_ROWS - avt_m.shape[1])))
    o_att = _attention(aq, ak2, avt, _pad_rows(ak2_m, META_ROWS), vmt, attn_sink[0].astype(_F32), batch, seq)

    qk, dec, r = _gla_prep(gq, gk, gz, w2s, gbias, sel)
    s0 = _gla_meta_state(_pad_rows(gk_m, LANES), _pad_rows(gz_m, LANES), _pad_rows(gv_m, LANES), w2s, gbias)
    o_f, o_b = _gla_scan(qk, r, dec, gv, s0, batch, seq)

    out = _stage_c(h1, o_att, o_f, o_b, gg, vec(gla_norm_g[0]), w_out[0].astype(_BF),
                   vec(mix_ln_g[0]), vec(mix_ln_b[0]), wg2, wu2, wd2, vec(ffn2_ln_g[0]), vec(ffn2_ln_b[0]))
    return out.reshape(batch, seq, d)
```

```python
import numpy as np
import jax
import jax.numpy as jnp
from jax import lax
from jax.experimental import pallas as pl
from jax.experimental.pallas import tpu as pltpu

N_META = 16
ATT_HEADS = 8
ATT_KV_HEADS = 2
ATT_HEAD_DIM = 64
ATT_REP = ATT_HEADS // ATT_KV_HEADS
WINDOW = 128
BLOCK = 128
GLA_HEADS = 4
GLA_DK = 64
GLA_DV = 128
GLA_RANK = 16
GLA_TAU = 16.0
N_DIR = 2
ATT_WIDTH = ATT_HEADS * ATT_HEAD_DIM
ATT_KV_WIDTH = ATT_KV_HEADS * ATT_HEAD_DIM
GLA_QK_WIDTH = GLA_HEADS * GLA_DK
GLA_WIDTH = GLA_HEADS * GLA_DV
LN_EPS = 1e-5
DEPTH = 1
DN_ALPHA = (2.0 * DEPTH) ** 0.25

LANES = 128
GLA_BLK = 16
GLA_GROUP = 128
BLK_PER_GROUP = GLA_GROUP // GLA_BLK
SCAN_SEQS = 8
FF_CHUNK = 256
ROW_TILE = 512
ROW_SUB_TILES = 2
SUB_TILE_SKEW = 3
LN_PIECES = 4
PM_UNIT = ROW_TILE // ROW_SUB_TILES
PREP_TILE = 2048
VMEM_LIMIT = 56 * 1024 * 1024

_BF = jnp.bfloat16
_F32 = jnp.float32


def _dot(a, b):
    return jnp.dot(a, b, preferred_element_type=_F32)


def _dot_nt(a, b):
    return lax.dot_general(a, b, (((1,), (1,)), ((), ())), preferred_element_type=_F32)


def _layer_norm(x, g, b):
    mu = jnp.mean(x, axis=-1, keepdims=True)
    xc = x - mu
    var = jnp.mean(xc * xc, axis=-1, keepdims=True)
    return xc * lax.rsqrt(var + LN_EPS) * g + b


def _silu(x):
    return x * (1.0 / (1.0 + jnp.exp(-x)))


def _split3(x):
    hi = x.astype(_BF)
    r1 = x - hi.astype(_F32)
    mid = r1.astype(_BF)
    lo = (r1 - mid.astype(_F32)).astype(_BF)
    return hi, mid, lo


def _sub_tiles(rows):
    n = ROW_SUB_TILES if rows % (ROW_SUB_TILES * 16) == 0 else 1
    return [pl.ds(s * (rows // n), rows // n) for s in range(n)]


def _run_skewed(step_gens, skew):
    pending = list(enumerate(step_gens))
    tick = 0
    while pending:
        for k, gen in list(pending):
            if tick < k * skew:
                continue
            try:
                next(gen)
            except StopIteration:
                pending.remove((k, gen))
        tick += 1


def _swiglu_ln_steps(h, wg_ref, wu_ref, wd_ref, g_ref, b_ref, act_ref):
    hb = h.astype(_BF)
    n_ff = wg_ref.shape[1]
    assert n_ff % FF_CHUNK == 0
    for c in range(n_ff // FF_CHUNK):
        sl = slice(c * FF_CHUNK, (c + 1) * FF_CHUNK)
        gate = _dot(hb, wg_ref[:, sl])
        up = _dot(hb, wu_ref[:, sl])
        act_ref[:, sl] = (_silu(gate) * up).astype(_BF)
        yield
    y = _dot(act_ref[...], wd_ref[...])
    return (yield from _residual_ln_steps(h, 0.5 * y, g_ref, b_ref))


def _residual_ln_steps(h, branch, g_ref, b_ref):
    rows = h.shape[0]
    piece = rows // LN_PIECES if rows % (LN_PIECES * 8) == 0 else rows
    out = []
    for r in range(0, rows, piece):
        out.append(_layer_norm(DN_ALPHA * h[r:r + piece] + branch[r:r + piece], g_ref[...], b_ref[...]))
        if r + piece < rows:
            yield
    return out[0] if len(out) == 1 else jnp.concatenate(out, axis=0)


_P_AQ = (0, 512)
_P_AK = (512, 768)
_P_AV = (768, 896)
_P_GQ = (896, 1152)
_P_GK = (1152, 1408)
_P_GV = (1408, 1920)
_P_GG = (1920, 2432)
_P_GZ = (2432, 2560)
_P_WIDTH = 2560


def _position_major_perm(rows):
    nblk = rows // GLA_BLK
    perm = np.zeros((rows, rows), np.float32)
    for n in range(nblk):
        for t in range(GLA_BLK):
            perm[t * nblk + n, n * GLA_BLK + t] = 1.0
    return perm


def _stage_a_kernel(x_ref, g0_ref, b0_ref, wg_ref, wu_ref, wd_ref, g1_ref, b1_ref, win_ref, perm_ref,
                    h1_ref, aq_ref, ak_ref, avt_ref, gq_ref, gk_ref, gv_ref, gg_ref, gz_ref, act_ref):
    def steps(rows):
        h0 = _layer_norm(x_ref[rows, :], g0_ref[...], b0_ref[...])
        yield
        h1 = yield from _swiglu_ln_steps(h0, wg_ref, wu_ref, wd_ref, g1_ref, b1_ref, act_ref.at[rows])
        h1_ref[rows, :] = h1
        hb = h1.astype(_BF)
        hb_pm = _dot(perm_ref[...], hb).astype(_BF)
        yield

        def proj(act, cols):
            return _dot(act, win_ref[:, cols[0]:cols[1]])

        aq_ref[rows, :] = (proj(hb, _P_AQ) * (ATT_HEAD_DIM ** -0.5)).astype(_BF)
        ak_ref[rows, :] = proj(hb, _P_AK).astype(_BF)
        avt_ref[:, rows] = proj(hb, _P_AV).T.astype(_BF)
        gq_ref[rows, :] = proj(hb_pm, _P_GQ) * (GLA_DK ** -0.5)
        yield
        gk_ref[rows, :] = proj(hb_pm, _P_GK)
        gv_ref[rows, :] = proj(hb, _P_GV).astype(_BF)
        gg_ref[rows, :] = proj(hb, _P_GG)
        gz_ref[rows, :] = proj(hb_pm, _P_GZ)

    _run_skewed([steps(rows) for rows in _sub_tiles(x_ref.shape[0])], SUB_TILE_SKEW)


def _const_spec(shape):
    nd = len(shape)
    return pl.BlockSpec(shape, lambda *_: (0,) * nd, pipeline_mode=pl.Buffered(1))


def _stage_a(x2d, g0, b0, wg, wu, wd, g1, b1, win):
    m, d = x2d.shape
    tm = min(ROW_TILE, m)
    assert m % tm == 0
    n_ff = wg.shape[1]
    sub = _sub_tiles(tm)[0].size
    perm = jnp.asarray(_position_major_perm(sub), _BF)
    row = lambda w: pl.BlockSpec((tm, w), lambda i: (i, 0))
    out_widths = [(d, _F32), (512, _BF), (256, _BF), None, (256, _F32), (256, _F32), (512, _BF), (512, _F32),
                  (128, _F32)]
    out_specs = [row(o[0]) if o else pl.BlockSpec((ATT_KV_WIDTH, tm), lambda i: (0, i)) for o in out_widths]
    out_shape = [jax.ShapeDtypeStruct((m, o[0]), o[1]) if o else jax.ShapeDtypeStruct((ATT_KV_WIDTH, m), _BF)
                 for o in out_widths]
    return pl.pallas_call(
        _stage_a_kernel,
        grid=(m // tm,),
        in_specs=[row(d), _const_spec((1, d)), _const_spec((1, d)), _const_spec((d, n_ff)), _const_spec((d, n_ff)),
                  _const_spec((n_ff, d)), _const_spec((1, d)), _const_spec((1, d)), _const_spec((d, _P_WIDTH)),
                  _const_spec((sub, sub))],
        out_specs=out_specs,
        out_shape=out_shape,
        scratch_shapes=[pltpu.VMEM((tm, n_ff), _BF)],
        compiler_params=pltpu.CompilerParams(dimension_semantics=("arbitrary",), vmem_limit_bytes=VMEM_LIMIT),
        name="stage_a",
    )(x2d, g0, b0, wg, wu, wd, g1, b1, win, perm)


_KWIN = 3 * BLOCK


def _alibi_slopes():
    return [2.0 ** (-8.0 * (i + 1) / ATT_HEADS) for i in range(ATT_HEADS)]


_WIN_OFFSETS = (0, -BLOCK, -2 * BLOCK)
ATT_QBLOCKS = 8
META_ROWS = 32


def _attn_kernel(sink_ref, q_ref, k_ref, vt_ref, km_ref, vmt_ref, o_ref, band_scr, meta_scr):
    i = pl.program_id(1)
    s_len = k_ref.shape[0]
    nq = s_len // BLOCK
    pairs_per_group = ATT_REP // 2

    @pl.when((pl.program_id(0) == 0) & (i == 0))
    def _():
        col = lax.broadcasted_iota(jnp.int32, (_KWIN, 2 * BLOCK), 1)
        rel0 = lax.broadcasted_iota(jnp.int32, (_KWIN, 2 * BLOCK), 0) - jnp.where(col < BLOCK, col, col - BLOCK)
        mrow = lax.broadcasted_iota(jnp.int32, (META_ROWS, 2 * BLOCK), 0)
        mcol = lax.broadcasted_iota(jnp.int32, (META_ROWS, 2 * BLOCK), 1)
        slopes = _alibi_slopes()
        for g in range(ATT_KV_HEADS):
            for pl_ in range(pairs_per_group):
                h0 = g * ATT_REP + 2 * pl_
                slope = jnp.where(col < BLOCK, slopes[h0], slopes[h0 + 1])
                for var, off in enumerate(_WIN_OFFSETS):
                    absrel = jnp.abs(rel0 + off).astype(_F32)
                    band_scr[var, g, pl_] = jnp.where(absrel <= float(WINDOW), -slope * absrel, -jnp.inf)
                sink = jnp.where(mcol < BLOCK, sink_ref[h0], sink_ref[h0 + 1])
                meta_scr[g, pl_] = jnp.where(mrow < N_META, 0.0, jnp.where(mrow == N_META, sink, -jnp.inf))

    n_blk = q_ref.shape[0] // BLOCK
    starts, variants = [], []
    for blk in range(n_blk):
        ib = i * n_blk + blk
        starts.append(pl.multiple_of(jnp.clip((ib - 1) * BLOCK, 0, s_len - _KWIN), BLOCK))
        variants.append(jnp.where(ib == 0, 0, jnp.where(ib == nq - 1, 2, 1)))
    low_half = lax.broadcasted_iota(jnp.int32, (BLOCK, LANES), 1) < ATT_HEAD_DIM
    zero = jnp.zeros((BLOCK, LANES), _BF)

    chains = [(blk, g, pl_) for blk in range(n_blk) for g in range(ATT_KV_HEADS) for pl_ in range(pairs_per_group)]

    def scores(blk, g, pl_):
        lanes_g = slice(g * LANES, (g + 1) * LANES)
        pair = g * pairs_per_group + pl_
        qp = q_ref[blk * BLOCK:(blk + 1) * BLOCK, pair * LANES:(pair + 1) * LANES]
        qs = jnp.concatenate([jnp.where(low_half, qp, zero), jnp.where(low_half, zero, qp)], axis=0)
        kg = k_ref[pl.ds(starts[blk], _KWIN), lanes_g]
        s_band = _dot_nt(kg, qs) + band_scr[variants[blk], g, pl_]
        s_meta = _dot_nt(km_ref[:, lanes_g], qs) + meta_scr[g, pl_]
        return s_band, s_meta

    def rows_reduce(op, arrays):
        parts = [a[r:r + 8, :] for a in arrays for r in range(0, a.shape[0], 8)]
        while len(parts) > 1:
            parts = [op(parts[j], parts[j + 1]) if j + 1 < len(parts) else parts[j]
                     for j in range(0, len(parts), 2)]
        return parts[0]

    def col_max(s_band, s_meta):
        return s_band, s_meta, jnp.max(rows_reduce(jnp.maximum, [s_band, s_meta]), axis=0, keepdims=True)

    def softmax(s_band, s_meta, mx):
        p_band = jnp.exp(s_band - mx)
        p_meta = jnp.exp(s_meta - mx)
        den = jnp.sum(rows_reduce(jnp.add, [p_band, p_meta]), axis=0, keepdims=True)
        return p_band.astype(_BF), p_meta.astype(_BF), den

    def values(blk, g, pl_, p_band, p_meta, den):
        drows = slice(g * ATT_HEAD_DIM, (g + 1) * ATT_HEAD_DIM)
        vt = vt_ref[drows, pl.ds(starts[blk], _KWIN)]
        ot = (_dot(vt, p_band) + _dot(vmt_ref[drows, :], p_meta)) * (1.0 / den)
        o2 = jnp.concatenate([ot[:, :BLOCK], ot[:, BLOCK:]], axis=0)
        pair = g * pairs_per_group + pl_
        o_ref[blk * BLOCK:(blk + 1) * BLOCK, pair * LANES:(pair + 1) * LANES] = o2.T.astype(_BF)

    s_vals, m_vals, p_vals = {}, {}, {}
    for t in range(len(chains) + 3):
        if t < len(chains):
            s_vals[t] = scores(*chains[t])
        if 0 <= t - 1 < len(chains):
            m_vals[t - 1] = col_max(*s_vals.pop(t - 1))
        if 0 <= t - 2 < len(chains):
            p_vals[t - 2] = softmax(*m_vals.pop(t - 2))
        if 0 <= t - 3 < len(chains):
            values(*chains[t - 3], *p_vals.pop(t - 3))


def _attention(aq, ak2, avt, km2, vmt, sink, batch, seq):
    m = aq.shape[0]
    nblk = ATT_QBLOCKS if (seq // BLOCK) % ATT_QBLOCKS == 0 else 1
    nq = seq // (BLOCK * nblk)
    return pl.pallas_call(
        _attn_kernel,
        grid=(batch, nq),
        in_specs=[pl.BlockSpec(memory_space=pltpu.SMEM),
                  pl.BlockSpec((nblk * BLOCK, ATT_WIDTH), lambda b, i: (b * nq + i, 0)),
                  pl.BlockSpec((seq, 2 * LANES), lambda b, i: (b, 0)),
                  pl.BlockSpec((ATT_KV_WIDTH, seq), lambda b, i: (0, b)),
                  pl.BlockSpec((META_ROWS, 2 * LANES), lambda b, i: (0, 0)),
                  pl.BlockSpec((ATT_KV_WIDTH, META_ROWS), lambda b, i: (0, 0))],
        out_specs=pl.BlockSpec((nblk * BLOCK, ATT_WIDTH), lambda b, i: (b * nq + i, 0)),
        out_shape=jax.ShapeDtypeStruct((m, ATT_WIDTH), _BF),
        scratch_shapes=[pltpu.VMEM((len(_WIN_OFFSETS), ATT_KV_HEADS, ATT_REP // 2, _KWIN, 2 * BLOCK), _F32),
                        pltpu.VMEM((ATT_KV_HEADS, ATT_REP // 2, META_ROWS, 2 * BLOCK), _F32)],
        compiler_params=pltpu.CompilerParams(dimension_semantics=("arbitrary", "arbitrary"),
                                             vmem_limit_bytes=VMEM_LIMIT),
        name="attention",
    )(sink, aq, ak2, avt, km2, vmt)


_LOG2E = 1.4426950408889634


def _gla_log2_gates(z, w2_ref, bias_ref):
    zh = z.astype(_BF)
    zm = (z - zh.astype(_F32)).astype(_BF)
    lane = lax.broadcasted_iota(jnp.int32, z.shape, 1)
    mid_lanes = (lane >= N_DIR * GLA_RANK) & (lane < 2 * N_DIR * GLA_RANK)
    logits = _dot(jnp.where(mid_lanes, zm, zh), w2_ref[...]) + bias_ref[...]
    y = logits * _LOG2E
    log2_sig = jnp.minimum(y, 0.0) - jnp.log2(1.0 + jnp.exp2(-jnp.abs(y)))
    return log2_sig * (1.0 / GLA_TAU)


def _sel_matrix():
    sel = np.zeros((N_DIR, 2, GLA_BLK, LANES, LANES), np.float32)
    for d in range(N_DIR):
        for lg in range(2):
            for s in range(GLA_BLK):
                for hl in range(2):
                    col = d * 64 + (2 * lg + hl) * GLA_BLK + s
                    sel[d, lg, s, hl * GLA_DK:(hl + 1) * GLA_DK, col] = 1.0
    return sel.reshape(N_DIR, 2 * GLA_BLK * LANES, LANES)


def _gla_prep_kernel(q0_ref, q1_ref, k0_ref, k1_ref, z_ref, w2_ref, bias_ref, sel_ref, perm_ref,
                     qk_ref, dec_ref, r_ref,
                     g_scr, b_scr, kbf_scr, qk_scr, dec_scr):
    tt_rows = z_ref.shape[0]
    units = tt_rows // PM_UNIT
    ub = PM_UNIT // GLA_BLK
    rc = min(512, tt_rows)
    for c in range(tt_rows // rc):
        rows = pl.ds(c * rc, rc)
        g = _gla_log2_gates(z_ref[rows, :], w2_ref, bias_ref)
        for gi in range(4):
            g_scr[gi, rows, :] = g[:, gi * LANES:(gi + 1) * LANES]

    def slab(ref, tt, lead=()):
        return jnp.concatenate([ref[lead + (pl.ds(u * PM_UNIT + tt * ub, ub), slice(None))] for u in range(units)],
                               axis=0)

    def store_slab(ref, lead, tt, val):
        for u in range(units):
            ref[lead + (pl.ds(u * PM_UNIT + tt * ub, ub), slice(None))] = val[u * ub:(u + 1) * ub, :]

    qk_refs = ((q0_ref, k0_ref), (q1_ref, k1_ref))
    last = {}
    for lg, (_, k_ref) in enumerate(qk_refs):
        for tt in range(GLA_BLK):
            kbf_scr[lg, tt] = slab(k_ref, tt).astype(_BF)
        for d in range(N_DIR):
            gi = d * 2 + lg
            acc = None
            for tt in (range(GLA_BLK) if d == 0 else reversed(range(GLA_BLK))):
                gs = slab(g_scr, tt, (gi,))
                acc = gs if acc is None else acc + gs
                b_scr[gi, tt] = acc
            last[gi] = acc
            dec_scr[gi] = jnp.exp2(acc)
    for tt in range(GLA_BLK):
        prods, sel_rows = [], []
        for d in range(N_DIR):
            positions = list(range(0, tt + 1)) if d == 0 else list(range(tt + 1, GLA_BLK))
            for lg, (q_ref, k_ref) in enumerate(qk_refs):
                gi = d * 2 + lg
                q = slab(q_ref, tt)
                b = b_scr[gi, tt]
                store_slab(qk_scr, (d * 4 + lg,), tt, q * jnp.exp2(b))
                store_slab(qk_scr, (d * 4 + 2 + lg,), tt, slab(k_ref, tt) * jnp.exp2(last[gi] - b))
                if not positions:
                    continue
                qb = q.astype(_BF)
                for s in positions:
                    p = qb * kbf_scr[lg, s]
                    if s != tt:
                        p = p * jnp.exp2(b - b_scr[gi, s]).astype(_BF)
                    prods.append(p)
                r0 = lg * GLA_BLK * LANES + positions[0] * LANES
                sel_rows.append(sel_ref[d, r0:r0 + len(positions) * LANES, :])
        a_tt = _dot(jnp.concatenate(prods, axis=1), jnp.concatenate(sel_rows, axis=0))
        store_slab(qk_scr, (8,), tt, a_tt)
    for u in range(units):
        rows = pl.ds(u * PM_UNIT, PM_UNIT)
        x = jnp.concatenate([qk_scr[c, rows, :] for c in range(9)], axis=1).astype(_BF)
        y = _dot(perm_ref[...], x).astype(_BF)
        qk_ref[rows, :] = y[:, :8 * LANES]
        r_ref[rows, :] = y[:, 8 * LANES:]
    for gi in range(4):
        dec_ref[:, gi * LANES:(gi + 1) * LANES] = dec_scr[gi]


def _gla_prep(gq, gk, gz, w2s, bias, sel):
    m = gq.shape[0]
    tt = min(PREP_TILE, m)
    assert m % tt == 0 and tt % PM_UNIT == 0
    nb = tt // GLA_BLK
    perm = jnp.asarray(_position_major_perm(PM_UNIT).T, _BF)
    col = lambda j: pl.BlockSpec((tt, LANES), lambda i: (i, j))
    return pl.pallas_call(
        _gla_prep_kernel,
        grid=(m // tt,),
        in_specs=[col(0), col(1), col(0), col(1), col(0),
                  _const_spec(w2s.shape), _const_spec(bias.shape), _const_spec(sel.shape),
                  _const_spec((PM_UNIT, PM_UNIT))],
        out_specs=[pl.BlockSpec((tt, 8 * LANES), lambda i: (i, 0)),
                   pl.BlockSpec((nb, 4 * LANES), lambda i: (i, 0)),
                   pl.BlockSpec((tt, LANES), lambda i: (i, 0))],
        out_shape=[jax.ShapeDtypeStruct((m, 8 * LANES), _BF),
                   jax.ShapeDtypeStruct((m // GLA_BLK, 4 * LANES), _F32),
                   jax.ShapeDtypeStruct((m, LANES), _BF)],
        scratch_shapes=[pltpu.VMEM((4, tt, LANES), _F32),
                        pltpu.VMEM((4, GLA_BLK, nb, LANES), _F32),
                        pltpu.VMEM((2, GLA_BLK, nb, LANES), _BF),
                        pltpu.VMEM((9, tt, LANES), _F32),
                        pltpu.VMEM((4, nb, LANES), _F32)],
        compiler_params=pltpu.CompilerParams(dimension_semantics=("arbitrary",), vmem_limit_bytes=VMEM_LIMIT),
        name="gla_prep",
    )(gq, gq, gk, gk, gz, w2s, bias, sel, perm)


def _gla_meta_state_kernel(k_ref, z_ref, v_ref, w2_ref, bias_ref, s0_ref):
    g = _gla_log2_gates(z_ref[...], w2_ref, bias_ref)[:, :GLA_QK_WIDTH]
    r = lax.broadcasted_iota(jnp.int32, (LANES, LANES), 0)
    c = lax.broadcasted_iota(jnp.int32, (LANES, LANES), 1)
    tail = jnp.where((c > r) & (c < N_META), 1.0, 0.0).astype(_BF)
    gh, gm, gl = _split3(g)
    rest = _dot(tail, gh) + _dot(tail, gm) + _dot(tail, gl)
    is_meta = lax.broadcasted_iota(jnp.int32, (LANES, GLA_QK_WIDTH), 0) < N_META
    ke = jnp.where(is_meta, k_ref[...] * jnp.exp2(jnp.where(is_meta, rest, 0.0)), 0.0).astype(_BF)
    lane = lax.broadcasted_iota(jnp.int32, (GLA_DV, GLA_QK_WIDTH), 1)
    v = v_ref[...].astype(_F32)
    s0 = jnp.zeros((GLA_DV, GLA_QK_WIDTH), _F32)
    for h in range(GLA_HEADS):
        vt = v[:, h * GLA_DV:(h + 1) * GLA_DV].T.astype(_BF)
        in_head = (lane >= h * GLA_DK) & (lane < (h + 1) * GLA_DK)
        s0 = s0 + jnp.where(in_head, _dot(vt, ke), 0.0)
    s0_ref[...] = s0


def _gla_meta_state(gk_m, gz_m, gv_m, w2s, bias):
    return pl.pallas_call(
        _gla_meta_state_kernel,
        out_shape=jax.ShapeDtypeStruct((GLA_DV, GLA_QK_WIDTH), _F32),
        name="gla_meta_state",
    )(gk_m, gz_m, gv_m, w2s, bias)


def _gla_scan_kernel(qkf_ref, qkb_ref, rf_ref, rb_ref, df_ref, db_ref, vf_ref, vb_ref, s0_ref,
                     of_ref, ob_ref, sf_scr, sb_scr):
    n_seq = sf_scr.shape[0]

    @pl.when(pl.program_id(1) == 0)
    def _():
        for b in range(n_seq):
            sf_scr[b] = s0_ref[...]
        sb_scr[...] = jnp.zeros_like(sb_scr)

    def lane_band(shape, width, n):
        lane = lax.broadcasted_iota(jnp.int32, shape, 1)
        return [(lane >= c * width) & (lane < (c + 1) * width) for c in range(n)]

    head_lanes = lane_band((GLA_BLK, GLA_QK_WIDTH), GLA_DK, GLA_HEADS)
    a_lanes = lane_band((GLA_BLK, LANES), GLA_BLK, N_DIR * GLA_HEADS)

    def head_stack(x, bands):
        return jnp.concatenate([jnp.where(bands[h], x, jnp.zeros_like(x)) for h in range(GLA_HEADS)], axis=0)

    dirs = ((qkf_ref, rf_ref, df_ref, vf_ref, of_ref, sf_scr),
            (qkb_ref, rb_ref, db_ref, vb_ref, ob_ref, sb_scr))
    for step in range(BLK_PER_GROUP):
        for b in range(n_seq):
            for d, (qk_ref, r_ref, d_ref, v_ref, o_ref, s_scr) in enumerate(dirs):
                j = step if d == 0 else BLK_PER_GROUP - 1 - step
                rows = slice(j * GLA_BLK, (j + 1) * GLA_BLK)
                state_t = s_scr[b]
                lhs1 = head_stack(qk_ref[b, rows, :GLA_QK_WIDTH], head_lanes)
                lhs2 = head_stack(r_ref[b, rows, :], a_lanes[d * GLA_HEADS:(d + 1) * GLA_HEADS])
                vblk = v_ref[b, rows, :]
                vstack = jnp.concatenate([vblk[:, h * GLA_DV:(h + 1) * GLA_DV] for h in range(GLA_HEADS)], axis=0)
                vv = jnp.concatenate([vstack, vstack], axis=0)
                o = _dot_nt(lhs1, state_t.astype(_BF)) + _dot(lhs2, vv)
                for h in range(GLA_HEADS):
                    o_ref[h, b, rows, :] = o[h * GLA_BLK:(h + 1) * GLA_BLK, :]
                kbd = head_stack(qk_ref[b, rows, GLA_QK_WIDTH:], head_lanes)
                vst = vstack.astype(_F32).T.astype(_BF)
                s_scr[b] = state_t * d_ref[b, j:j + 1, :] + _dot(vst, kbd)


def _gla_scan(qk, r, dec, gv, s0, batch, seq):
    ng = seq // GLA_GROUP
    nbs = SCAN_SEQS if batch % SCAN_SEQS == 0 else 1
    qk = qk.reshape(batch, seq, N_DIR * 2 * GLA_QK_WIDTH)
    r = r.reshape(batch, seq, LANES)
    dec = dec.reshape(batch, seq // GLA_BLK, N_DIR * GLA_QK_WIDTH)
    gv = gv.reshape(batch, seq, GLA_WIDTH)
    fwd = lambda i: i
    bwd = lambda i: ng - 1 - i
    o_shape = jax.ShapeDtypeStruct((GLA_HEADS, batch, seq, GLA_DV), _F32)

    def spec(rows, width, grp, col):
        return pl.BlockSpec((nbs, rows, width), lambda b, i: (b, grp(i), col))

    o_f, o_b = pl.pallas_call(
        _gla_scan_kernel,
        grid=(batch // nbs, ng),
        in_specs=[spec(GLA_GROUP, 2 * GLA_QK_WIDTH, fwd, 0), spec(GLA_GROUP, 2 * GLA_QK_WIDTH, bwd, 1),
                  spec(GLA_GROUP, LANES, fwd, 0), spec(GLA_GROUP, LANES, bwd, 0),
                  spec(BLK_PER_GROUP, GLA_QK_WIDTH, fwd, 0), spec(BLK_PER_GROUP, GLA_QK_WIDTH, bwd, 1),
                  spec(GLA_GROUP, GLA_WIDTH, fwd, 0), spec(GLA_GROUP, GLA_WIDTH, bwd, 0),
                  pl.BlockSpec((GLA_DV, GLA_QK_WIDTH), lambda b, i: (0, 0))],
        out_specs=[pl.BlockSpec((GLA_HEADS, nbs, GLA_GROUP, GLA_DV), lambda b, i: (0, b, fwd(i), 0)),
                   pl.BlockSpec((GLA_HEADS, nbs, GLA_GROUP, GLA_DV), lambda b, i: (0, b, bwd(i), 0))],
        out_shape=[o_shape, o_shape],
        scratch_shapes=[pltpu.VMEM((nbs, GLA_DV, GLA_QK_WIDTH), _F32), pltpu.VMEM((nbs, GLA_DV, GLA_QK_WIDTH), _F32)],
        compiler_params=pltpu.CompilerParams(dimension_semantics=("arbitrary", "arbitrary"),
                                             vmem_limit_bytes=VMEM_LIMIT),
        name="gla_scan",
    )(qk, qk, r, r, dec, dec, gv, gv, s0)
    return (o_f.reshape(GLA_HEADS, batch * seq, GLA_DV), o_b.reshape(GLA_HEADS, batch * seq, GLA_DV))


def _stage_c_kernel(h1_ref, oatt_ref, of_ref, ob_ref, gg_ref, ng_ref, wout_ref, gm_ref, bm_ref,
                    wg_ref, wu_ref, wd_ref, g2_ref, b2_ref, out_ref, act_ref):
    def steps(rows):
        pieces = []
        for h in range(GLA_HEADS):
            o = of_ref[h, rows, :] + ob_ref[h, rows, :]
            o = o * lax.rsqrt(jnp.mean(o * o, axis=-1, keepdims=True) + LN_EPS) * ng_ref[...]
            pieces.append((o * _silu(gg_ref[rows, h * GLA_DV:(h + 1) * GLA_DV])).astype(_BF))
        mix = jnp.concatenate([oatt_ref[rows, :]] + pieces, axis=1)
        yield
        h2 = yield from _residual_ln_steps(h1_ref[rows, :], _dot(mix, wout_ref[...]), gm_ref, bm_ref)
        yield
        out_ref[rows, :] = yield from _swiglu_ln_steps(h2, wg_ref, wu_ref, wd_ref, g2_ref, b2_ref,
                                                       act_ref.at[rows])

    _run_skewed([steps(rows) for rows in _sub_tiles(h1_ref.shape[0])], SUB_TILE_SKEW)


def _stage_c(h1, oatt, o_f, o_b, gg, ng, wout, gm, bm, wg, wu, wd, g2, b2):
    m, d = h1.shape
    tm = min(ROW_TILE, m)
    assert m % tm == 0
    n_ff = wg.shape[1]
    row = lambda w: pl.BlockSpec((tm, w), lambda i: (i, 0))
    hrow = pl.BlockSpec((GLA_HEADS, tm, GLA_DV), lambda i: (0, i, 0))
    return pl.pallas_call(
        _stage_c_kernel,
        grid=(m // tm,),
        in_specs=[row(d), row(ATT_WIDTH), hrow, hrow, row(GLA_WIDTH), _const_spec((1, GLA_DV)),
                  _const_spec(wout.shape), _const_spec((1, d)), _const_spec((1, d)),
                  _const_spec((d, n_ff)), _const_spec((d, n_ff)), _const_spec((n_ff, d)),
                  _const_spec((1, d)), _const_spec((1, d))],
        out_specs=row(d),
        out_shape=jax.ShapeDtypeStruct((m, d), _F32),
        scratch_shapes=[pltpu.VMEM((tm, n_ff), _BF)],
        compiler_params=pltpu.CompilerParams(dimension_semantics=("arbitrary",), vmem_limit_bytes=VMEM_LIMIT),
        name="stage_c",
    )(h1, oatt, o_f, o_b, gg, ng, wout, gm, bm, wg, wu, wd, g2, b2)


def _projection_weight(w_in):
    aq = w_in[:, 0:512]
    ak = w_in[:, 512:640]
    rest = w_in[:, 640:2304]
    gz = w_in[:, 2304:2336]
    ak_dup = jnp.concatenate([ak[:, 0:64], ak[:, 0:64], ak[:, 64:128], ak[:, 64:128]], axis=1)
    gz3 = jnp.pad(jnp.concatenate([gz, gz, gz], axis=1), ((0, 0), (0, LANES - 3 * gz.shape[1])))
    return jnp.concatenate([aq, ak_dup, rest, gz3], axis=1).astype(_BF)


def _gate_weight(w2, bias):
    k = N_DIR * GLA_RANK
    wcat = jnp.zeros((k, N_DIR * GLA_QK_WIDTH), _F32)
    for n in range(N_DIR):
        wcat = wcat.at[n * GLA_RANK:(n + 1) * GLA_RANK, n * GLA_QK_WIDTH:(n + 1) * GLA_QK_WIDTH].set(w2[n])
    hi = wcat.astype(_BF)
    mid = (wcat - hi.astype(_F32)).astype(_BF)
    pad = jnp.zeros((LANES - 3 * k, N_DIR * GLA_QK_WIDTH), _BF)
    return jnp.concatenate([hi, hi, mid, pad], axis=0), bias.reshape(1, N_DIR * GLA_QK_WIDTH)


def _pad_rows(a, rows):
    return jnp.pad(a, ((0, rows - a.shape[0]), (0, 0)))


def kernel(x, meta_tokens, ln_in_g, ln_in_b, ffn1_ln_g, ffn1_ln_b, ffn1_w_gate, ffn1_w_up, ffn1_w_down, w_in,
           attn_sink, gla_gate_w2, gla_gate_b, gla_norm_g, w_out, mix_ln_g, mix_ln_b, ffn2_w_gate, ffn2_w_up,
           ffn2_w_down, ffn2_ln_g, ffn2_ln_b):
    batch, seq, d = x.shape
    assert seq % BLOCK == 0 and seq >= _KWIN and seq % GLA_GROUP == 0
    vec = lambda a: a.reshape(1, -1)
    x2d = x.reshape(batch * seq, d)
    wg1, wu1, wd1 = ffn1_w_gate[0].astype(_BF), ffn1_w_up[0].astype(_BF), ffn1_w_down[0].astype(_BF)
    wg2, wu2, wd2 = ffn2_w_gate[0].astype(_BF), ffn2_w_up[0].astype(_BF), ffn2_w_down[0].astype(_BF)
    win = _projection_weight(w_in[0])
    w2s, gbias = _gate_weight(gla_gate_w2[0], gla_gate_b[0])
    sel = jnp.asarray(_sel_matrix(), _BF)

    a_args = (vec(ln_in_g), vec(ln_in_b), wg1, wu1, wd1, vec(ffn1_ln_g[0]), vec(ffn1_ln_b[0]), win)
    h1, aq, ak2, avt, gq, gk, gv, gg, gz = _stage_a(x2d, *a_args)
    _, _, ak2_m, avt_m, _, gk_m, gv_m, _, gz_m = _stage_a(meta_tokens.astype(x.dtype), *a_args)

    vmt = jnp.pad(avt_m, ((0, 0), (0, META_ROWS - avt_m.shape[1])))
    o_att = _attention(aq, ak2, avt, _pad_rows(ak2_m, META_ROWS), vmt, attn_sink[0].astype(_F32), batch, seq)

    qk, dec, r = _gla_prep(gq, gk, gz, w2s, gbias, sel)
    s0 = _gla_meta_state(_pad_rows(gk_m, LANES), _pad_rows(gz_m, LANES), _pad_rows(gv_m, LANES), w2s, gbias)
    o_f, o_b = _gla_scan(qk, r, dec, gv, s0, batch, seq)

    out = _stage_c(h1, o_att, o_f, o_b, gg, vec(gla_norm_g[0]), w_out[0].astype(_BF),
                   vec(mix_ln_g[0]), vec(mix_ln_b[0]), wg2, wu2, wd2, vec(ffn2_ln_g[0]), vec(ffn2_ln_b[0]))
    return out.reshape(batch, seq, d)
```

```python
import numpy as np
import jax
import jax.numpy as jnp
from jax import lax
from jax.experimental import pallas as pl
from jax.experimental.pallas import tpu as pltpu

N_META = 16
ATT_HEADS = 8
ATT_KV_HEADS = 2
ATT_HEAD_DIM = 64
ATT_REP = ATT_HEADS // ATT_KV_HEADS
WINDOW = 128
BLOCK = 128
GLA_HEADS = 4
GLA_DK = 64
GLA_DV = 128
GLA_RANK = 16
GLA_TAU = 16.0
N_DIR = 2
ATT_WIDTH = ATT_HEADS * ATT_HEAD_DIM
ATT_KV_WIDTH = ATT_KV_HEADS * ATT_HEAD_DIM
GLA_QK_WIDTH = GLA_HEADS * GLA_DK
GLA_WIDTH = GLA_HEADS * GLA_DV
LN_EPS = 1e-5
DEPTH = 1
DN_ALPHA = (2.0 * DEPTH) ** 0.25

LANES = 128
GLA_BLK = 16
GLA_GROUP = 128
BLK_PER_GROUP = GLA_GROUP // GLA_BLK
SCAN_SEQS = 8
FF_CHUNK = 256
ROW_TILE = 512
ROW_SUB_TILES = 2
SUB_TILE_SKEW = 3
LN_PIECES = 4
PM_UNIT = ROW_TILE // ROW_SUB_TILES
PREP_TILE = 2048
VMEM_LIMIT = 56 * 1024 * 1024

_BF = jnp.bfloat16
_F32 = jnp.float32


def _dot(a, b):
    return jnp.dot(a, b, preferred_element_type=_F32)


def _dot_nt(a, b):
    return lax.dot_general(a, b, (((1,), (1,)), ((), ())), preferred_element_type=_F32)


def _layer_norm(x, g, b):
    mu = jnp.mean(x, axis=-1, keepdims=True)
    xc = x - mu
    var = jnp.mean(xc * xc, axis=-1, keepdims=True)
    return xc * lax.rsqrt(var + LN_EPS) * g + b


def _silu(x):
    return x * (1.0 / (1.0 + jnp.exp(-x)))


def _split3(x):
    hi = x.astype(_BF)
    r1 = x - hi.astype(_F32)
    mid = r1.astype(_BF)
    lo = (r1 - mid.astype(_F32)).astype(_BF)
    return hi, mid, lo


def _sub_tiles(rows):
    n = ROW_SUB_TILES if rows % (ROW_SUB_TILES * 16) == 0 else 1
    return [pl.ds(s * (rows // n), rows // n) for s in range(n)]


def _run_skewed(step_gens, skew):
    pending = list(enumerate(step_gens))
    tick = 0
    while pending:
        for k, gen in list(pending):
            if tick < k * skew:
                continue
            try:
                next(gen)
            except StopIteration:
                pending.remove((k, gen))
        tick += 1


def _swiglu_ln_steps(h, wg_ref, wu_ref, wd_ref, g_ref, b_ref, act_ref):
    hb = h.astype(_BF)
    n_ff = wg_ref.shape[1]
    assert n_ff % FF_CHUNK == 0
    for c in range(n_ff // FF_CHUNK):
        sl = slice(c * FF_CHUNK, (c + 1) * FF_CHUNK)
        gate = _dot(hb, wg_ref[:, sl])
        up = _dot(hb, wu_ref[:, sl])
        act_ref[:, sl] = (_silu(gate) * up).astype(_BF)
        yield
    y = _dot(act_ref[...], wd_ref[...])
    return (yield from _residual_ln_steps(h, 0.5 * y, g_ref, b_ref))


def _residual_ln_steps(h, branch, g_ref, b_ref):
    rows = h.shape[0]
    piece = rows // LN_PIECES if rows % (LN_PIECES * 8) == 0 else rows
    out = []
    for r in range(0, rows, piece):
        out.append(_layer_norm(DN_ALPHA * h[r:r + piece] + branch[r:r + piece], g_ref[...], b_ref[...]))
        if r + piece < rows:
            yield
    return out[0] if len(out) == 1 else jnp.concatenate(out, axis=0)


_P_AQ = (0, 512)
_P_AK = (512, 768)
_P_AV = (768, 896)
_P_GQ = (896, 1152)
_P_GK = (1152, 1408)
_P_GV = (1408, 1920)
_P_GG = (1920, 2432)
_P_GZ = (2432, 2560)
_P_WIDTH = 2560


def _position_major_perm(rows):
    nblk = rows // GLA_BLK
    perm = np.zeros((rows, rows), np.float32)
    for n in range(nblk):
        for t in range(GLA_BLK):
            perm[t * nblk + n, n * GLA_BLK + t] = 1.0
    return perm


def _stage_a_kernel(x_ref, g0_ref, b0_ref, wg_ref, wu_ref, wd_ref, g1_ref, b1_ref, win_ref, perm_ref,
                    h1_ref, aq_ref, ak_ref, avt_ref, gq_ref, gk_ref, gv_ref, gg_ref, gz_ref, act_ref):
    def steps(rows):
        h0 = _layer_norm(x_ref[rows, :], g0_ref[...], b0_ref[...])
        yield
        h1 = yield from _swiglu_ln_steps(h0, wg_ref, wu_ref, wd_ref, g1_ref, b1_ref, act_ref.at[rows])
        h1_ref[rows, :] = h1
        hb = h1.astype(_BF)
        hb_pm = _dot(perm_ref[...], hb).astype(_BF)
        yield

        def proj(act, cols):
            return _dot(act, win_ref[:, cols[0]:cols[1]])

        aq_ref[rows, :] = (proj(hb, _P_AQ) * (ATT_HEAD_DIM ** -0.5)).astype(_BF)
        ak_ref[rows, :] = proj(hb, _P_AK).astype(_BF)
        avt_ref[:, rows] = proj(hb, _P_AV).T.astype(_BF)
        gq_ref[rows, :] = proj(hb_pm, _P_GQ) * (GLA_DK ** -0.5)
        yield
        gk_ref[rows, :] = proj(hb_pm, _P_GK)
        gv_ref[rows, :] = proj(hb, _P_GV).astype(_BF)
        gg_ref[rows, :] = proj(hb, _P_GG)
        gz_ref[rows, :] = proj(hb_pm, _P_GZ)

    _run_skewed([steps(rows) for rows in _sub_tiles(x_ref.shape[0])], SUB_TILE_SKEW)


def _const_spec(shape):
    nd = len(shape)
    return pl.BlockSpec(shape, lambda *_: (0,) * nd, pipeline_mode=pl.Buffered(1))


def _stage_a(x2d, g0, b0, wg, wu, wd, g1, b1, win):
    m, d = x2d.shape
    tm = min(ROW_TILE, m)
    assert m % tm == 0
    n_ff = wg.shape[1]
    sub = _sub_tiles(tm)[0].size
    perm = jnp.asarray(_position_major_perm(sub), _BF)
    row = lambda w: pl.BlockSpec((tm, w), lambda i: (i, 0))
    out_widths = [(d, _F32), (512, _BF), (256, _BF), None, (256, _F32), (256, _F32), (512, _BF), (512, _F32),
                  (128, _F32)]
    out_specs = [row(o[0]) if o else pl.BlockSpec((ATT_KV_WIDTH, tm), lambda i: (0, i)) for o in out_widths]
    out_shape = [jax.ShapeDtypeStruct((m, o[0]), o[1]) if o else jax.ShapeDtypeStruct((ATT_KV_WIDTH, m), _BF)
                 for o in out_widths]
    return pl.pallas_call(
        _stage_a_kernel,
        grid=(m // tm,),
        in_specs=[row(d), _const_spec((1, d)), _const_spec((1, d)), _const_spec((d, n_ff)), _const_spec((d, n_ff)),
                  _const_spec((n_ff, d)), _const_spec((1, d)), _const_spec((1, d)), _const_spec((d, _P_WIDTH)),
                  _const_spec((sub, sub))],
        out_specs=out_specs,
        out_shape=out_shape,
        scratch_shapes=[pltpu.VMEM((tm, n_ff), _BF)],
        compiler_params=pltpu.CompilerParams(dimension_semantics=("arbitrary",), vmem_limit_bytes=VMEM_LIMIT),
        name="stage_a",
    )(x2d, g0, b0, wg, wu, wd, g1, b1, win, perm)


_KWIN = 3 * BLOCK


def _alibi_slopes():
    return [2.0 ** (-8.0 * (i + 1) / ATT_HEADS) for i in range(ATT_HEADS)]


_WIN_OFFSETS = (0, -BLOCK, -2 * BLOCK)
ATT_QBLOCKS = 8
META_ROWS = 32


def _attn_kernel(sink_ref, q_ref, k_ref, vt_ref, km_ref, vmt_ref, o_ref, band_scr, meta_scr):
    i = pl.program_id(1)
    s_len = k_ref.shape[0]
    nq = s_len // BLOCK
    pairs_per_group = ATT_REP // 2

    @pl.when((pl.program_id(0) == 0) & (i == 0))
    def _():
        col = lax.broadcasted_iota(jnp.int32, (_KWIN, 2 * BLOCK), 1)
        rel0 = lax.broadcasted_iota(jnp.int32, (_KWIN, 2 * BLOCK), 0) - jnp.where(col < BLOCK, col, col - BLOCK)
        mrow = lax.broadcasted_iota(jnp.int32, (META_ROWS, 2 * BLOCK), 0)
        mcol = lax.broadcasted_iota(jnp.int32, (META_ROWS, 2 * BLOCK), 1)
        slopes = _alibi_slopes()
        for g in range(ATT_KV_HEADS):
            for pl_ in range(pairs_per_group):
                h0 = g * ATT_REP + 2 * pl_
                slope = jnp.where(col < BLOCK, slopes[h0], slopes[h0 + 1])
                for var, off in enumerate(_WIN_OFFSETS):
                    absrel = jnp.abs(rel0 + off).astype(_F32)
                    band_scr[var, g, pl_] = jnp.where(absrel <= float(WINDOW), -slope * absrel, -jnp.inf)
                sink = jnp.where(mcol < BLOCK, sink_ref[h0], sink_ref[h0 + 1])
                meta_scr[g, pl_] = jnp.where(mrow < N_META, 0.0, jnp.where(mrow == N_META, sink, -jnp.inf))

    n_blk = q_ref.shape[0] // BLOCK
    starts, variants = [], []
    for blk in range(n_blk):
        ib = i * n_blk + blk
        starts.append(pl.multiple_of(jnp.clip((ib - 1) * BLOCK, 0, s_len - _KWIN), BLOCK))
        variants.append(jnp.where(ib == 0, 0, jnp.where(ib == nq - 1, 2, 1)))
    low_half = lax.broadcasted_iota(jnp.int32, (BLOCK, LANES), 1) < ATT_HEAD_DIM
    zero = jnp.zeros((BLOCK, LANES), _BF)

    chains = [(blk, g, pl_) for blk in range(n_blk) for g in range(ATT_KV_HEADS) for pl_ in range(pairs_per_group)]

    def scores(blk, g, pl_):
        lanes_g = slice(g * LANES, (g + 1) * LANES)
        pair = g * pairs_per_group + pl_
        qp = q_ref[blk * BLOCK:(blk + 1) * BLOCK, pair * LANES:(pair + 1) * LANES]
        qs = jnp.concatenate([jnp.where(low_half, qp, zero), jnp.where(low_half, zero, qp)], axis=0)
        kg = k_ref[pl.ds(starts[blk], _KWIN), lanes_g]
        s_band = _dot_nt(kg, qs) + band_scr[variants[blk], g, pl_]
        s_meta = _dot_nt(km_ref[:, lanes_g], qs) + meta_scr[g, pl_]
        return s_band, s_meta

    def rows_reduce(op, arrays):
        parts = [a[r:r + 8, :] for a in arrays for r in range(0, a.shape[0], 8)]
        while len(parts) > 1:
            parts = [op(parts[j], parts[j + 1]) if j + 1 < len(parts) else parts[j]
                     for j in range(0, len(parts), 2)]
        return parts[0]

    def col_max(s_band, s_meta):
        return s_band, s_meta, jnp.max(rows_reduce(jnp.maximum, [s_band, s_meta]), axis=0, keepdims=True)

    def softmax(s_band, s_meta, mx):
        return jnp.exp(s_band - mx).astype(_BF), jnp.exp(s_meta - mx).astype(_BF)

    ones_band = jnp.ones((16, _KWIN), _BF)
    ones_meta = jnp.where(lax.broadcasted_iota(jnp.int32, (16, META_ROWS), 1) <= N_META, 1.0, 0.0).astype(_BF)

    def values(blk, g, pl_, p_band, p_meta):
        drows = slice(g * ATT_HEAD_DIM, (g + 1) * ATT_HEAD_DIM)
        vt = jnp.concatenate([vt_ref[drows, pl.ds(starts[blk], _KWIN)], ones_band], axis=0)
        vmt = jnp.concatenate([vmt_ref[drows, :], ones_meta], axis=0)
        acc = _dot(vt, p_band) + _dot(vmt, p_meta)
        ot = acc[:ATT_HEAD_DIM, :] * (1.0 / acc[ATT_HEAD_DIM:ATT_HEAD_DIM + 1, :])
        o2 = jnp.concatenate([ot[:, :BLOCK], ot[:, BLOCK:]], axis=0)
        pair = g * pairs_per_group + pl_
        o_ref[blk * BLOCK:(blk + 1) * BLOCK, pair * LANES:(pair + 1) * LANES] = o2.T.astype(_BF)

    s_vals, m_vals, p_vals = {}, {}, {}
    for t in range(len(chains) + 3):
        if t < len(chains):
            s_vals[t] = scores(*chains[t])
        if 0 <= t - 1 < len(chains):
            m_vals[t - 1] = col_max(*s_vals.pop(t - 1))
        if 0 <= t - 2 < len(chains):
            p_vals[t - 2] = softmax(*m_vals.pop(t - 2))
        if 0 <= t - 3 < len(chains):
            values(*chains[t - 3], *p_vals.pop(t - 3))


def _attention(aq, ak2, avt, km2, vmt, sink, batch, seq):
    m = aq.shape[0]
    nblk = ATT_QBLOCKS if (seq // BLOCK) % ATT_QBLOCKS == 0 else 1
    nq = seq // (BLOCK * nblk)
    return pl.pallas_call(
        _attn_kernel,
        grid=(batch, nq),
        in_specs=[pl.BlockSpec(memory_space=pltpu.SMEM),
                  pl.BlockSpec((nblk * BLOCK, ATT_WIDTH), lambda b, i: (b * nq + i, 0)),
                  pl.BlockSpec((seq, 2 * LANES), lambda b, i: (b, 0)),
                  pl.BlockSpec((ATT_KV_WIDTH, seq), lambda b, i: (0, b)),
                  pl.BlockSpec((META_ROWS, 2 * LANES), lambda b, i: (0, 0)),
                  pl.BlockSpec((ATT_KV_WIDTH, META_ROWS), lambda b, i: (0, 0))],
        out_specs=pl.BlockSpec((nblk * BLOCK, ATT_WIDTH), lambda b, i: (b * nq + i, 0)),
        out_shape=jax.ShapeDtypeStruct((m, ATT_WIDTH), _BF),
        scratch_shapes=[pltpu.VMEM((len(_WIN_OFFSETS), ATT_KV_HEADS, ATT_REP // 2, _KWIN, 2 * BLOCK), _F32),
                        pltpu.VMEM((ATT_KV_HEADS, ATT_REP // 2, META_ROWS, 2 * BLOCK), _F32)],
        compiler_params=pltpu.CompilerParams(dimension_semantics=("arbitrary", "arbitrary"),
                                             vmem_limit_bytes=VMEM_LIMIT),
        name="attention",
    )(sink, aq, ak2, avt, km2, vmt)


_LOG2E = 1.4426950408889634


def _gla_log2_gates(z, w2_ref, bias_ref):
    zh = z.astype(_BF)
    zm = (z - zh.astype(_F32)).astype(_BF)
    lane = lax.broadcasted_iota(jnp.int32, z.shape, 1)
    mid_lanes = (lane >= N_DIR * GLA_RANK) & (lane < 2 * N_DIR * GLA_RANK)
    logits = _dot(jnp.where(mid_lanes, zm, zh), w2_ref[...]) + bias_ref[...]
    y = logits * _LOG2E
    log2_sig = jnp.minimum(y, 0.0) - jnp.log2(1.0 + jnp.exp2(-jnp.abs(y)))
    return log2_sig * (1.0 / GLA_TAU)


def _sel_matrix():
    sel = np.zeros((N_DIR, 2, GLA_BLK, LANES, LANES), np.float32)
    for d in range(N_DIR):
        for lg in range(2):
            for s in range(GLA_BLK):
                for hl in range(2):
                    col = d * 64 + (2 * lg + hl) * GLA_BLK + s
                    sel[d, lg, s, hl * GLA_DK:(hl + 1) * GLA_DK, col] = 1.0
    return sel.reshape(N_DIR, 2 * GLA_BLK * LANES, LANES)


def _gla_prep_kernel(q0_ref, q1_ref, k0_ref, k1_ref, z_ref, w2_ref, bias_ref, sel_ref, perm_ref,
                     qk_ref, dec_ref, r_ref,
                     g_scr, b_scr, kbf_scr, qk_scr, dec_scr):
    tt_rows = z_ref.shape[0]
    units = tt_rows // PM_UNIT
    ub = PM_UNIT // GLA_BLK
    rc = min(512, tt_rows)
    for c in range(tt_rows // rc):
        rows = pl.ds(c * rc, rc)
        g = _gla_log2_gates(z_ref[rows, :], w2_ref, bias_ref)
        for gi in range(4):
            g_scr[gi, rows, :] = g[:, gi * LANES:(gi + 1) * LANES]

    def slab(ref, tt, lead=()):
        return jnp.concatenate([ref[lead + (pl.ds(u * PM_UNIT + tt * ub, ub), slice(None))] for u in range(units)],
                               axis=0)

    def store_slab(ref, lead, tt, val):
        for u in range(units):
            ref[lead + (pl.ds(u * PM_UNIT + tt * ub, ub), slice(None))] = val[u * ub:(u + 1) * ub, :]

    qk_refs = ((q0_ref, k0_ref), (q1_ref, k1_ref))
    last = {}
    for lg, (_, k_ref) in enumerate(qk_refs):
        for tt in range(GLA_BLK):
            kbf_scr[lg, tt] = slab(k_ref, tt).astype(_BF)
        for d in range(N_DIR):
            gi = d * 2 + lg
            acc = None
            for tt in (range(GLA_BLK) if d == 0 else reversed(range(GLA_BLK))):
                gs = slab(g_scr, tt, (gi,))
                acc = gs if acc is None else acc + gs
                b_scr[gi, tt] = acc
            last[gi] = acc
            dec_scr[gi] = jnp.exp2(acc)
    for tt in range(GLA_BLK):
        prods, sel_rows = [], []
        for d in range(N_DIR):
            positions = list(range(0, tt + 1)) if d == 0 else list(range(tt + 1, GLA_BLK))
            for lg, (q_ref, k_ref) in enumerate(qk_refs):
                gi = d * 2 + lg
                q = slab(q_ref, tt)
                b = b_scr[gi, tt]
                store_slab(qk_scr, (d * 4 + lg,), tt, q * jnp.exp2(b))
                store_slab(qk_scr, (d * 4 + 2 + lg,), tt, slab(k_ref, tt) * jnp.exp2(last[gi] - b))
                if not positions:
                    continue
                qb = q.astype(_BF)
                for s in positions:
                    p = qb * kbf_scr[lg, s]
                    if s != tt:
                        p = p * jnp.exp2(b - b_scr[gi, s]).astype(_BF)
                    prods.append(p)
                r0 = lg * GLA_BLK * LANES + positions[0] * LANES
                sel_rows.append(sel_ref[d, r0:r0 + len(positions) * LANES, :])
        a_tt = _dot(jnp.concatenate(prods, axis=1), jnp.concatenate(sel_rows, axis=0))
        store_slab(qk_scr, (8,), tt, a_tt)
    for u in range(units):
        rows = pl.ds(u * PM_UNIT, PM_UNIT)
        x = jnp.concatenate([qk_scr[c, rows, :] for c in range(9)], axis=1).astype(_BF)
        y = _dot(perm_ref[...], x).astype(_BF)
        qk_ref[rows, :] = y[:, :8 * LANES]
        r_ref[rows, :] = y[:, 8 * LANES:]
    for gi in range(4):
        dec_ref[:, gi * LANES:(gi + 1) * LANES] = dec_scr[gi]


def _gla_prep(gq, gk, gz, w2s, bias, sel):
    m = gq.shape[0]
    tt = min(PREP_TILE, m)
    assert m % tt == 0 and tt % PM_UNIT == 0
    nb = tt // GLA_BLK
    perm = jnp.asarray(_position_major_perm(PM_UNIT).T, _BF)
    col = lambda j: pl.BlockSpec((tt, LANES), lambda i: (i, j))
    return pl.pallas_call(
        _gla_prep_kernel,
        grid=(m // tt,),
        in_specs=[col(0), col(1), col(0), col(1), col(0),
                  _const_spec(w2s.shape), _const_spec(bias.shape), _const_spec(sel.shape),
                  _const_spec((PM_UNIT, PM_UNIT))],
        out_specs=[pl.BlockSpec((tt, 8 * LANES), lambda i: (i, 0)),
                   pl.BlockSpec((nb, 4 * LANES), lambda i: (i, 0)),
                   pl.BlockSpec((tt, LANES), lambda i: (i, 0))],
        out_shape=[jax.ShapeDtypeStruct((m, 8 * LANES), _BF),
                   jax.ShapeDtypeStruct((m // GLA_BLK, 4 * LANES), _F32),
                   jax.ShapeDtypeStruct((m, LANES), _BF)],
        scratch_shapes=[pltpu.VMEM((4, tt, LANES), _F32),
                        pltpu.VMEM((4, GLA_BLK, nb, LANES), _F32),
                        pltpu.VMEM((2, GLA_BLK, nb, LANES), _BF),
                        pltpu.VMEM((9, tt, LANES), _F32),
                        pltpu.VMEM((4, nb, LANES), _F32)],
        compiler_params=pltpu.CompilerParams(dimension_semantics=("arbitrary",), vmem_limit_bytes=VMEM_LIMIT),
        name="gla_prep",
    )(gq, gq, gk, gk, gz, w2s, bias, sel, perm)


def _gla_meta_state_kernel(k_ref, z_ref, v_ref, w2_ref, bias_ref, s0_ref):
    g = _gla_log2_gates(z_ref[...], w2_ref, bias_ref)[:, :GLA_QK_WIDTH]
    r = lax.broadcasted_iota(jnp.int32, (LANES, LANES), 0)
    c = lax.broadcasted_iota(jnp.int32, (LANES, LANES), 1)
    tail = jnp.where((c > r) & (c < N_META), 1.0, 0.0).astype(_BF)
    gh, gm, gl = _split3(g)
    rest = _dot(tail, gh) + _dot(tail, gm) + _dot(tail, gl)
    is_meta = lax.broadcasted_iota(jnp.int32, (LANES, GLA_QK_WIDTH), 0) < N_META
    ke = jnp.where(is_meta, k_ref[...] * jnp.exp2(jnp.where(is_meta, rest, 0.0)), 0.0).astype(_BF)
    lane = lax.broadcasted_iota(jnp.int32, (GLA_DV, GLA_QK_WIDTH), 1)
    v = v_ref[...].astype(_F32)
    s0 = jnp.zeros((GLA_DV, GLA_QK_WIDTH), _F32)
    for h in range(GLA_HEADS):
        vt = v[:, h * GLA_DV:(h + 1) * GLA_DV].T.astype(_BF)
        in_head = (lane >= h * GLA_DK) & (lane < (h + 1) * GLA_DK)
        s0 = s0 + jnp.where(in_head, _dot(vt, ke), 0.0)
    s0_ref[...] = s0


def _gla_meta_state(gk_m, gz_m, gv_m, w2s, bias):
    return pl.pallas_call(
        _gla_meta_state_kernel,
        out_shape=jax.ShapeDtypeStruct((GLA_DV, GLA_QK_WIDTH), _F32),
        name="gla_meta_state",
    )(gk_m, gz_m, gv_m, w2s, bias)


def _gla_scan_kernel(qkf_ref, qkb_ref, rf_ref, rb_ref, df_ref, db_ref, vf_ref, vb_ref, s0_ref,
                     of_ref, ob_ref, sf_scr, sb_scr):
    n_seq = sf_scr.shape[0]

    @pl.when(pl.program_id(1) == 0)
    def _():
        for b in range(n_seq):
            sf_scr[b] = s0_ref[...]
        sb_scr[...] = jnp.zeros_like(sb_scr)

    def lane_band(shape, width, n):
        lane = lax.broadcasted_iota(jnp.int32, shape, 1)
        return [(lane >= c * width) & (lane < (c + 1) * width) for c in range(n)]

    head_lanes = lane_band((GLA_BLK, GLA_QK_WIDTH), GLA_DK, GLA_HEADS)
    a_lanes = lane_band((GLA_BLK, LANES), GLA_BLK, N_DIR * GLA_HEADS)

    def head_stack(x, bands):
        return jnp.concatenate([jnp.where(bands[h], x, jnp.zeros_like(x)) for h in range(GLA_HEADS)], axis=0)

    dirs = ((qkf_ref, rf_ref, df_ref, vf_ref, of_ref, sf_scr),
            (qkb_ref, rb_ref, db_ref, vb_ref, ob_ref, sb_scr))
    for step in range(BLK_PER_GROUP):
        for b in range(n_seq):
            for d, (qk_ref, r_ref, d_ref, v_ref, o_ref, s_scr) in enumerate(dirs):
                j = step if d == 0 else BLK_PER_GROUP - 1 - step
                rows = slice(j * GLA_BLK, (j + 1) * GLA_BLK)
                state_t = s_scr[b]
                lhs1 = head_stack(qk_ref[b, rows, :GLA_QK_WIDTH], head_lanes)
                lhs2 = head_stack(r_ref[b, rows, :], a_lanes[d * GLA_HEADS:(d + 1) * GLA_HEADS])
                vblk = v_ref[b, rows, :]
                vstack = jnp.concatenate([vblk[:, h * GLA_DV:(h + 1) * GLA_DV] for h in range(GLA_HEADS)], axis=0)
                vv = jnp.concatenate([vstack, vstack], axis=0)
                o = _dot_nt(lhs1, state_t.astype(_BF)) + _dot(lhs2, vv)
                for h in range(GLA_HEADS):
                    o_ref[h, b, rows, :] = o[h * GLA_BLK:(h + 1) * GLA_BLK, :]
                kbd = head_stack(qk_ref[b, rows, GLA_QK_WIDTH:], head_lanes)
                vst = vstack.astype(_F32).T.astype(_BF)
                s_scr[b] = state_t * d_ref[b, j:j + 1, :] + _dot(vst, kbd)


def _gla_scan(qk, r, dec, gv, s0, batch, seq):
    ng = seq // GLA_GROUP
    nbs = SCAN_SEQS if batch % SCAN_SEQS == 0 else 1
    qk = qk.reshape(batch, seq, N_DIR * 2 * GLA_QK_WIDTH)
    r = r.reshape(batch, seq, LANES)
    dec = dec.reshape(batch, seq // GLA_BLK, N_DIR * GLA_QK_WIDTH)
    gv = gv.reshape(batch, seq, GLA_WIDTH)
    fwd = lambda i: i
    bwd = lambda i: ng - 1 - i
    o_shape = jax.ShapeDtypeStruct((GLA_HEADS, batch, seq, GLA_DV), _F32)

    def spec(rows, width, grp, col):
        return pl.BlockSpec((nbs, rows, width), lambda b, i: (b, grp(i), col))

    o_f, o_b = pl.pallas_call(
        _gla_scan_kernel,
        grid=(batch // nbs, ng),
        in_specs=[spec(GLA_GROUP, 2 * GLA_QK_WIDTH, fwd, 0), spec(GLA_GROUP, 2 * GLA_QK_WIDTH, bwd, 1),
                  spec(GLA_GROUP, LANES, fwd, 0), spec(GLA_GROUP, LANES, bwd, 0),
                  spec(BLK_PER_GROUP, GLA_QK_WIDTH, fwd, 0), spec(BLK_PER_GROUP, GLA_QK_WIDTH, bwd, 1),
                  spec(GLA_GROUP, GLA_WIDTH, fwd, 0), spec(GLA_GROUP, GLA_WIDTH, bwd, 0),
                  pl.BlockSpec((GLA_DV, GLA_QK_WIDTH), lambda b, i: (0, 0))],
        out_specs=[pl.BlockSpec((GLA_HEADS, nbs, GLA_GROUP, GLA_DV), lambda b, i: (0, b, fwd(i), 0)),
                   pl.BlockSpec((GLA_HEADS, nbs, GLA_GROUP, GLA_DV), lambda b, i: (0, b, bwd(i), 0))],
        out_shape=[o_shape, o_shape],
        scratch_shapes=[pltpu.VMEM((nbs, GLA_DV, GLA_QK_WIDTH), _F32), pltpu.VMEM((nbs, GLA_DV, GLA_QK_WIDTH), _F32)],
        compiler_params=pltpu.CompilerParams(dimension_semantics=("arbitrary", "arbitrary"),
                                             vmem_limit_bytes=VMEM_LIMIT),
        name="gla_scan",
    )(qk, qk, r, r, dec, dec, gv, gv, s0)
    return (o_f.reshape(GLA_HEADS, batch * seq, GLA_DV), o_b.reshape(GLA_HEADS, batch * seq, GLA_DV))


def _stage_c_kernel(h1_ref, oatt_ref, of_ref, ob_ref, gg_ref, ng_ref, wout_ref, gm_ref, bm_ref,
                    wg_ref, wu_ref, wd_ref, g2_ref, b2_ref, out_ref, act_ref):
    def steps(rows):
        pieces = []
        for h in range(GLA_HEADS):
            o = of_ref[h, rows, :] + ob_ref[h, rows, :]
            o = o * lax.rsqrt(jnp.mean(o * o, axis=-1, keepdims=True) + LN_EPS) * ng_ref[...]
            pieces.append((o * _silu(gg_ref[rows, h * GLA_DV:(h + 1) * GLA_DV])).astype(_BF))
        mix = jnp.concatenate([oatt_ref[rows, :]] + pieces, axis=1)
        yield
        h2 = yield from _residual_ln_steps(h1_ref[rows, :], _dot(mix, wout_ref[...]), gm_ref, bm_ref)
        yield
        out_ref[rows, :] = yield from _swiglu_ln_steps(h2, wg_ref, wu_ref, wd_ref, g2_ref, b2_ref,
                                                       act_ref.at[rows])

    _run_skewed([steps(rows) for rows in _sub_tiles(h1_ref.shape[0])], SUB_TILE_SKEW)


def _stage_c(h1, oatt, o_f, o_b, gg, ng, wout, gm, bm, wg, wu, wd, g2, b2):
    m, d = h1.shape
    tm = min(ROW_TILE, m)
    assert m % tm == 0
    n_ff = wg.shape[1]
    row = lambda w: pl.BlockSpec((tm, w), lambda i: (i, 0))
    hrow = pl.BlockSpec((GLA_HEADS, tm, GLA_DV), lambda i: (0, i, 0))
    return pl.pallas_call(
        _stage_c_kernel,
        grid=(m // tm,),
        in_specs=[row(d), row(ATT_WIDTH), hrow, hrow, row(GLA_WIDTH), _const_spec((1, GLA_DV)),
                  _const_spec(wout.shape), _const_spec((1, d)), _const_spec((1, d)),
                  _const_spec((d, n_ff)), _const_spec((d, n_ff)), _const_spec((n_ff, d)),
                  _const_spec((1, d)), _const_spec((1, d))],
        out_specs=row(d),
        out_shape=jax.ShapeDtypeStruct((m, d), _F32),
        scratch_shapes=[pltpu.VMEM((tm, n_ff), _BF)],
        compiler_params=pltpu.CompilerParams(dimension_semantics=("arbitrary",), vmem_limit_bytes=VMEM_LIMIT),
        name="stage_c",
    )(h1, oatt, o_f, o_b, gg, ng, wout, gm, bm, wg, wu, wd, g2, b2)


def _projection_weight(w_in):
    aq = w_in[:, 0:512]
    ak = w_in[:, 512:640]
    rest = w_in[:, 640:2304]
    gz = w_in[:, 2304:2336]
    ak_dup = jnp.concatenate([ak[:, 0:64], ak[:, 0:64], ak[:, 64:128], ak[:, 64:128]], axis=1)
    gz3 = jnp.pad(jnp.concatenate([gz, gz, gz], axis=1), ((0, 0), (0, LANES - 3 * gz.shape[1])))
    return jnp.concatenate([aq, ak_dup, rest, gz3], axis=1).astype(_BF)


def _gate_weight(w2, bias):
    k = N_DIR * GLA_RANK
    wcat = jnp.zeros((k, N_DIR * GLA_QK_WIDTH), _F32)
    for n in range(N_DIR):
        wcat = wcat.at[n * GLA_RANK:(n + 1) * GLA_RANK, n * GLA_QK_WIDTH:(n + 1) * GLA_QK_WIDTH].set(w2[n])
    hi = wcat.astype(_BF)
    mid = (wcat - hi.astype(_F32)).astype(_BF)
    pad = jnp.zeros((LANES - 3 * k, N_DIR * GLA_QK_WIDTH), _BF)
    return jnp.concatenate([hi, hi, mid, pad], axis=0), bias.reshape(1, N_DIR * GLA_QK_WIDTH)


def _pad_rows(a, rows):
    return jnp.pad(a, ((0, rows - a.shape[0]), (0, 0)))


def kernel(x, meta_tokens, ln_in_g, ln_in_b, ffn1_ln_g, ffn1_ln_b, ffn1_w_gate, ffn1_w_up, ffn1_w_down, w_in,
           attn_sink, gla_gate_w2, gla_gate_b, gla_norm_g, w_out, mix_ln_g, mix_ln_b, ffn2_w_gate, ffn2_w_up,
           ffn2_w_down, ffn2_ln_g, ffn2_ln_b):
    batch, seq, d = x.shape
    assert seq % BLOCK == 0 and seq >= _KWIN and seq % GLA_GROUP == 0
    vec = lambda a: a.reshape(1, -1)
    x2d = x.reshape(batch * seq, d)
    wg1, wu1, wd1 = ffn1_w_gate[0].astype(_BF), ffn1_w_up[0].astype(_BF), ffn1_w_down[0].astype(_BF)
    wg2, wu2, wd2 = ffn2_w_gate[0].astype(_BF), ffn2_w_up[0].astype(_BF), ffn2_w_down[0].astype(_BF)
    win = _projection_weight(w_in[0])
    w2s, gbias = _gate_weight(gla_gate_w2[0], gla_gate_b[0])
    sel = jnp.asarray(_sel_matrix(), _BF)

    a_args = (vec(ln_in_g), vec(ln_in_b), wg1, wu1, wd1, vec(ffn1_ln_g[0]), vec(ffn1_ln_b[0]), win)
    h1, aq, ak2, avt, gq, gk, gv, gg, gz = _stage_a(x2d, *a_args)
    _, _, ak2_m, avt_m, _, gk_m, gv_m, _, gz_m = _stage_a(meta_tokens.astype(x.dtype), *a_args)

    vmt = jnp.pad(avt_m, ((0, 0), (0, META_ROWS - avt_m.shape[1])))
    o_att = _attention(aq, ak2, avt, _pad_rows(ak2_m, META_ROWS), vmt, attn_sink[0].astype(_F32), batch, seq)

    qk, dec, r = _gla_prep(gq, gk, gz, w2s, gbias, sel)
    s0 = _gla_meta_state(_pad_rows(gk_m, LANES), _pad_rows(gz_m, LANES), _pad_rows(gv_m, LANES), w2s, gbias)
    o_f, o_b = _gla_scan(qk, r, dec, gv, s0, batch, seq)

    out = _stage_c(h1, o_att, o_f, o_b, gg, vec(gla_norm_g[0]), w_out[0].astype(_BF),
                   vec(mix_ln_g[0]), vec(mix_ln_b[0]), wg2, wu2, wd2, vec(ffn2_ln_g[0]), vec(ffn2_ln_b[0]))
    return out.reshape(batch, seq, d)
```

```python
import numpy as np
import jax
import jax.numpy as jnp
from jax import lax
from jax.experimental import pallas as pl
from jax.experimental.pallas import tpu as pltpu

N_META = 16
ATT_HEADS = 8
ATT_KV_HEADS = 2
ATT_HEAD_DIM = 64
ATT_REP = ATT_HEADS // ATT_KV_HEADS
WINDOW = 128
BLOCK = 128
GLA_HEADS = 4
GLA_DK = 64
GLA_DV = 128
GLA_RANK = 16
GLA_TAU = 16.0
N_DIR = 2
ATT_WIDTH = ATT_HEADS * ATT_HEAD_DIM
ATT_KV_WIDTH = ATT_KV_HEADS * ATT_HEAD_DIM
GLA_QK_WIDTH = GLA_HEADS * GLA_DK
GLA_WIDTH = GLA_HEADS * GLA_DV
LN_EPS = 1e-5
DEPTH = 1
DN_ALPHA = (2.0 * DEPTH) ** 0.25

LANES = 128
GLA_BLK = 16
GLA_GROUP = 128
BLK_PER_GROUP = GLA_GROUP // GLA_BLK
SCAN_SEQS = 8
FF_CHUNK = 256
ROW_TILE = 512
ROW_SUB_TILES = 2
SUB_TILE_SKEW = 3
LN_PIECES = 4
PM_UNIT = ROW_TILE // ROW_SUB_TILES
PREP_TILE = 2048
VMEM_LIMIT = 56 * 1024 * 1024

_BF = jnp.bfloat16
_F32 = jnp.float32


def _dot(a, b):
    return jnp.dot(a, b, preferred_element_type=_F32)


def _dot_nt(a, b):
    return lax.dot_general(a, b, (((1,), (1,)), ((), ())), preferred_element_type=_F32)


def _layer_norm(x, g, b):
    mu = jnp.mean(x, axis=-1, keepdims=True)
    xc = x - mu
    var = jnp.mean(xc * xc, axis=-1, keepdims=True)
    return xc * lax.rsqrt(var + LN_EPS) * g + b


def _silu(x):
    return x * (1.0 / (1.0 + jnp.exp(-x)))


def _split3(x):
    hi = x.astype(_BF)
    r1 = x - hi.astype(_F32)
    mid = r1.astype(_BF)
    lo = (r1 - mid.astype(_F32)).astype(_BF)
    return hi, mid, lo


def _sub_tiles(rows):
    n = ROW_SUB_TILES if rows % (ROW_SUB_TILES * 16) == 0 else 1
    return [pl.ds(s * (rows // n), rows // n) for s in range(n)]


def _run_skewed(step_gens, skew):
    pending = list(enumerate(step_gens))
    tick = 0
    while pending:
        for k, gen in list(pending):
            if tick < k * skew:
                continue
            try:
                next(gen)
            except StopIteration:
                pending.remove((k, gen))
        tick += 1


def _swiglu_ln_steps(h, wg_ref, wu_ref, wd_ref, g_ref, b_ref, act_ref):
    hb = h.astype(_BF)
    n_ff = wg_ref.shape[1]
    assert n_ff % FF_CHUNK == 0
    for c in range(n_ff // FF_CHUNK):
        sl = slice(c * FF_CHUNK, (c + 1) * FF_CHUNK)
        gate = _dot(hb, wg_ref[:, sl])
        up = _dot(hb, wu_ref[:, sl])
        act_ref[:, sl] = (_silu(gate) * up).astype(_BF)
        yield
    y = _dot(act_ref[...], wd_ref[...])
    return (yield from _residual_ln_steps(h, 0.5 * y, g_ref, b_ref))


def _residual_ln_steps(h, branch, g_ref, b_ref):
    rows = h.shape[0]
    piece = rows // LN_PIECES if rows % (LN_PIECES * 8) == 0 else rows
    out = []
    for r in range(0, rows, piece):
        out.append(_layer_norm(DN_ALPHA * h[r:r + piece] + branch[r:r + piece], g_ref[...], b_ref[...]))
        if r + piece < rows:
            yield
    return out[0] if len(out) == 1 else jnp.concatenate(out, axis=0)


_P_AQ = (0, 512)
_P_AK = (512, 768)
_P_AV = (768, 896)
_P_GQ = (896, 1152)
_P_GK = (1152, 1408)
_P_GV = (1408, 1920)
_P_GG = (1920, 2432)
_P_GZ = (2432, 2560)
_P_WIDTH = 2560


def _position_major_perm(rows):
    nblk = rows // GLA_BLK
    perm = np.zeros((rows, rows), np.float32)
    for n in range(nblk):
        for t in range(GLA_BLK):
            perm[t * nblk + n, n * GLA_BLK + t] = 1.0
    return perm


def _stage_a_kernel(x_ref, g0_ref, b0_ref, wg_ref, wu_ref, wd_ref, g1_ref, b1_ref, win_ref, perm_ref,
                    h1_ref, aq_ref, ak_ref, avt_ref, gq_ref, gk_ref, gv_ref, gg_ref, gz_ref, act_ref):
    def steps(rows):
        h0 = _layer_norm(x_ref[rows, :], g0_ref[...], b0_ref[...])
        yield
        h1 = yield from _swiglu_ln_steps(h0, wg_ref, wu_ref, wd_ref, g1_ref, b1_ref, act_ref.at[rows])
        h1_ref[rows, :] = h1
        hb = h1.astype(_BF)
        hb_pm = _dot(perm_ref[...], hb).astype(_BF)
        yield

        def proj(act, cols):
            return _dot(act, win_ref[:, cols[0]:cols[1]])

        aq_ref[rows, :] = (proj(hb, _P_AQ) * (ATT_HEAD_DIM ** -0.5)).astype(_BF)
        ak_ref[rows, :] = proj(hb, _P_AK).astype(_BF)
        avt_ref[:, rows] = proj(hb, _P_AV).T.astype(_BF)
        gq_ref[rows, :] = proj(hb_pm, _P_GQ) * (GLA_DK ** -0.5)
        yield
        gk_ref[rows, :] = proj(hb_pm, _P_GK)
        gv_ref[rows, :] = proj(hb, _P_GV).astype(_BF)
        gg_ref[rows, :] = proj(hb, _P_GG)
        gz_ref[rows, :] = proj(hb_pm, _P_GZ)

    _run_skewed([steps(rows) for rows in _sub_tiles(x_ref.shape[0])], SUB_TILE_SKEW)


def _const_spec(shape):
    nd = len(shape)
    return pl.BlockSpec(shape, lambda *_: (0,) * nd, pipeline_mode=pl.Buffered(1))


def _stage_a(x2d, g0, b0, wg, wu, wd, g1, b1, win):
    m, d = x2d.shape
    tm = min(ROW_TILE, m)
    assert m % tm == 0
    n_ff = wg.shape[1]
    sub = _sub_tiles(tm)[0].size
    perm = jnp.asarray(_position_major_perm(sub), _BF)
    row = lambda w: pl.BlockSpec((tm, w), lambda i: (i, 0))
    out_widths = [(d, _F32), (512, _BF), (256, _BF), None, (256, _F32), (256, _F32), (512, _BF), (512, _F32),
                  (128, _F32)]
    out_specs = [row(o[0]) if o else pl.BlockSpec((ATT_KV_WIDTH, tm), lambda i: (0, i)) for o in out_widths]
    out_shape = [jax.ShapeDtypeStruct((m, o[0]), o[1]) if o else jax.ShapeDtypeStruct((ATT_KV_WIDTH, m), _BF)
                 for o in out_widths]
    return pl.pallas_call(
        _stage_a_kernel,
        grid=(m // tm,),
        in_specs=[row(d), _const_spec((1, d)), _const_spec((1, d)), _const_spec((d, n_ff)), _const_spec((d, n_ff)),
                  _const_spec((n_ff, d)), _const_spec((1, d)), _const_spec((1, d)), _const_spec((d, _P_WIDTH)),
                  _const_spec((sub, sub))],
        out_specs=out_specs,
        out_shape=out_shape,
        scratch_shapes=[pltpu.VMEM((tm, n_ff), _BF)],
        compiler_params=pltpu.CompilerParams(dimension_semantics=("arbitrary",), vmem_limit_bytes=VMEM_LIMIT),
        name="stage_a",
    )(x2d, g0, b0, wg, wu, wd, g1, b1, win, perm)


_KWIN = 3 * BLOCK


def _alibi_slopes():
    return [2.0 ** (-8.0 * (i + 1) / ATT_HEADS) for i in range(ATT_HEADS)]


_WIN_OFFSETS = (0, -BLOCK, -2 * BLOCK)
ATT_QBLOCKS = 16
META_ROWS = 32


def _attn_kernel(sink_ref, q_ref, k_ref, vt_ref, km_ref, vmt_ref, o_ref, band_scr, meta_scr):
    i = pl.program_id(1)
    s_len = k_ref.shape[0]
    nq = s_len // BLOCK
    pairs_per_group = ATT_REP // 2

    @pl.when((pl.program_id(0) == 0) & (i == 0))
    def _():
        col = lax.broadcasted_iota(jnp.int32, (_KWIN, 2 * BLOCK), 1)
        rel0 = lax.broadcasted_iota(jnp.int32, (_KWIN, 2 * BLOCK), 0) - jnp.where(col < BLOCK, col, col - BLOCK)
        mrow = lax.broadcasted_iota(jnp.int32, (META_ROWS, 2 * BLOCK), 0)
        mcol = lax.broadcasted_iota(jnp.int32, (META_ROWS, 2 * BLOCK), 1)
        slopes = _alibi_slopes()
        for g in range(ATT_KV_HEADS):
            for pl_ in range(pairs_per_group):
                h0 = g * ATT_REP + 2 * pl_
                slope = jnp.where(col < BLOCK, slopes[h0], slopes[h0 + 1])
                for var, off in enumerate(_WIN_OFFSETS):
                    absrel = jnp.abs(rel0 + off).astype(_F32)
                    band_scr[var, g, pl_] = jnp.where(absrel <= float(WINDOW), -slope * absrel, -jnp.inf)
                sink = jnp.where(mcol < BLOCK, sink_ref[h0], sink_ref[h0 + 1])
                meta_scr[g, pl_] = jnp.where(mrow < N_META, 0.0, jnp.where(mrow == N_META, sink, -jnp.inf))

    n_blk = q_ref.shape[0] // BLOCK
    starts, variants = [], []
    for blk in range(n_blk):
        ib = i * n_blk + blk
        starts.append(pl.multiple_of(jnp.clip((ib - 1) * BLOCK, 0, s_len - _KWIN), BLOCK))
        variants.append(jnp.where(ib == 0, 0, jnp.where(ib == nq - 1, 2, 1)))
    low_half = lax.broadcasted_iota(jnp.int32, (BLOCK, LANES), 1) < ATT_HEAD_DIM
    zero = jnp.zeros((BLOCK, LANES), _BF)

    chains = [(blk, g, pl_) for blk in range(n_blk) for g in range(ATT_KV_HEADS) for pl_ in range(pairs_per_group)]

    def scores(blk, g, pl_):
        lanes_g = slice(g * LANES, (g + 1) * LANES)
        pair = g * pairs_per_group + pl_
        qp = q_ref[blk * BLOCK:(blk + 1) * BLOCK, pair * LANES:(pair + 1) * LANES]
        qs = jnp.concatenate([jnp.where(low_half, qp, zero), jnp.where(low_half, zero, qp)], axis=0)
        kg = k_ref[pl.ds(starts[blk], _KWIN), lanes_g]
        s_band = _dot_nt(kg, qs) + band_scr[variants[blk], g, pl_]
        s_meta = _dot_nt(km_ref[:, lanes_g], qs) + meta_scr[g, pl_]
        return s_band, s_meta

    def rows_reduce(op, arrays):
        parts = [a[r:r + 8, :] for a in arrays for r in range(0, a.shape[0], 8)]
        while len(parts) > 1:
            parts = [op(parts[j], parts[j + 1]) if j + 1 < len(parts) else parts[j]
                     for j in range(0, len(parts), 2)]
        return parts[0]

    def col_max(s_band, s_meta):
        return s_band, s_meta, jnp.max(rows_reduce(jnp.maximum, [s_band, s_meta]), axis=0, keepdims=True)

    def softmax(s_band, s_meta, mx):
        return jnp.exp(s_band - mx).astype(_BF), jnp.exp(s_meta - mx).astype(_BF)

    ones_band = jnp.ones((16, _KWIN), _BF)
    ones_meta = jnp.where(lax.broadcasted_iota(jnp.int32, (16, META_ROWS), 1) <= N_META, 1.0, 0.0).astype(_BF)

    def values(blk, g, pl_, p_band, p_meta):
        drows = slice(g * ATT_HEAD_DIM, (g + 1) * ATT_HEAD_DIM)
        vt = jnp.concatenate([vt_ref[drows, pl.ds(starts[blk], _KWIN)], ones_band], axis=0)
        vmt = jnp.concatenate([vmt_ref[drows, :], ones_meta], axis=0)
        acc = _dot(vt, p_band) + _dot(vmt, p_meta)
        ot = acc[:ATT_HEAD_DIM, :] * (1.0 / acc[ATT_HEAD_DIM:ATT_HEAD_DIM + 1, :])
        o2 = jnp.concatenate([ot[:, :BLOCK], ot[:, BLOCK:]], axis=0)
        pair = g * pairs_per_group + pl_
        o_ref[blk * BLOCK:(blk + 1) * BLOCK, pair * LANES:(pair + 1) * LANES] = o2.T.astype(_BF)

    s_vals, m_vals, p_vals = {}, {}, {}
    for t in range(len(chains) + 3):
        if t < len(chains):
            s_vals[t] = scores(*chains[t])
        if 0 <= t - 1 < len(chains):
            m_vals[t - 1] = col_max(*s_vals.pop(t - 1))
        if 0 <= t - 2 < len(chains):
            p_vals[t - 2] = softmax(*m_vals.pop(t - 2))
        if 0 <= t - 3 < len(chains):
            values(*chains[t - 3], *p_vals.pop(t - 3))


def _attention(aq, ak2, avt, km2, vmt, sink, batch, seq):
    m = aq.shape[0]
    nblk = ATT_QBLOCKS if (seq // BLOCK) % ATT_QBLOCKS == 0 else 1
    nq = seq // (BLOCK * nblk)
    return pl.pallas_call(
        _attn_kernel,
        grid=(batch, nq),
        in_specs=[pl.BlockSpec(memory_space=pltpu.SMEM),
                  pl.BlockSpec((nblk * BLOCK, ATT_WIDTH), lambda b, i: (b * nq + i, 0)),
                  pl.BlockSpec((seq, 2 * LANES), lambda b, i: (b, 0)),
                  pl.BlockSpec((ATT_KV_WIDTH, seq), lambda b, i: (0, b)),
                  pl.BlockSpec((META_ROWS, 2 * LANES), lambda b, i: (0, 0)),
                  pl.BlockSpec((ATT_KV_WIDTH, META_ROWS), lambda b, i: (0, 0))],
        out_specs=pl.BlockSpec((nblk * BLOCK, ATT_WIDTH), lambda b, i: (b * nq + i, 0)),
        out_shape=jax.ShapeDtypeStruct((m, ATT_WIDTH), _BF),
        scratch_shapes=[pltpu.VMEM((len(_WIN_OFFSETS), ATT_KV_HEADS, ATT_REP // 2, _KWIN, 2 * BLOCK), _F32),
                        pltpu.VMEM((ATT_KV_HEADS, ATT_REP // 2, META_ROWS, 2 * BLOCK), _F32)],
        compiler_params=pltpu.CompilerParams(dimension_semantics=("arbitrary", "arbitrary"),
                                             vmem_limit_bytes=VMEM_LIMIT),
        name="attention",
    )(sink, aq, ak2, avt, km2, vmt)


_LOG2E = 1.4426950408889634


def _gla_log2_gates(z, w2_ref, bias_ref):
    zh = z.astype(_BF)
    zm = (z - zh.astype(_F32)).astype(_BF)
    lane = lax.broadcasted_iota(jnp.int32, z.shape, 1)
    mid_lanes = (lane >= N_DIR * GLA_RANK) & (lane < 2 * N_DIR * GLA_RANK)
    logits = _dot(jnp.where(mid_lanes, zm, zh), w2_ref[...]) + bias_ref[...]
    y = logits * _LOG2E
    log2_sig = jnp.minimum(y, 0.0) - jnp.log2(1.0 + jnp.exp2(-jnp.abs(y)))
    return log2_sig * (1.0 / GLA_TAU)


def _sel_matrix():
    sel = np.zeros((N_DIR, 2, GLA_BLK, LANES, LANES), np.float32)
    for d in range(N_DIR):
        for lg in range(2):
            for s in range(GLA_BLK):
                for hl in range(2):
                    col = d * 64 + (2 * lg + hl) * GLA_BLK + s
                    sel[d, lg, s, hl * GLA_DK:(hl + 1) * GLA_DK, col] = 1.0
    return sel.reshape(N_DIR, 2 * GLA_BLK * LANES, LANES)


def _gla_prep_kernel(q0_ref, q1_ref, k0_ref, k1_ref, z_ref, w2_ref, bias_ref, sel_ref, perm_ref,
                     qk_ref, dec_ref, r_ref,
                     g_scr, b_scr, kbf_scr, qk_scr, dec_scr):
    tt_rows = z_ref.shape[0]
    units = tt_rows // PM_UNIT
    ub = PM_UNIT // GLA_BLK
    rc = min(512, tt_rows)
    for c in range(tt_rows // rc):
        rows = pl.ds(c * rc, rc)
        g = _gla_log2_gates(z_ref[rows, :], w2_ref, bias_ref)
        for gi in range(4):
            g_scr[gi, rows, :] = g[:, gi * LANES:(gi + 1) * LANES]

    def slab(ref, tt, lead=()):
        return jnp.concatenate([ref[lead + (pl.ds(u * PM_UNIT + tt * ub, ub), slice(None))] for u in range(units)],
                               axis=0)

    def store_slab(ref, lead, tt, val):
        for u in range(units):
            ref[lead + (pl.ds(u * PM_UNIT + tt * ub, ub), slice(None))] = val[u * ub:(u + 1) * ub, :]

    qk_refs = ((q0_ref, k0_ref), (q1_ref, k1_ref))
    last = {}
    for lg, (_, k_ref) in enumerate(qk_refs):
        for tt in range(GLA_BLK):
            kbf_scr[lg, tt] = slab(k_ref, tt).astype(_BF)
        for d in range(N_DIR):
            gi = d * 2 + lg
            acc = None
            for tt in (range(GLA_BLK) if d == 0 else reversed(range(GLA_BLK))):
                gs = slab(g_scr, tt, (gi,))
                acc = gs if acc is None else acc + gs
                b_scr[gi, tt] = acc
            last[gi] = acc
            dec_scr[gi] = jnp.exp2(acc)
    for tt in range(GLA_BLK):
        prods, sel_rows = [], []
        for d in range(N_DIR):
            positions = list(range(0, tt + 1)) if d == 0 else list(range(tt + 1, GLA_BLK))
            for lg, (q_ref, k_ref) in enumerate(qk_refs):
                gi = d * 2 + lg
                q = slab(q_ref, tt)
                b = b_scr[gi, tt]
                store_slab(qk_scr, (d * 4 + lg,), tt, q * jnp.exp2(b))
                store_slab(qk_scr, (d * 4 + 2 + lg,), tt, slab(k_ref, tt) * jnp.exp2(last[gi] - b))
                if not positions:
                    continue
                qb = q.astype(_BF)
                for s in positions:
                    p = qb * kbf_scr[lg, s]
                    if s != tt:
                        p = p * jnp.exp2(b - b_scr[gi, s]).astype(_BF)
                    prods.append(p)
                r0 = lg * GLA_BLK * LANES + positions[0] * LANES
                sel_rows.append(sel_ref[d, r0:r0 + len(positions) * LANES, :])
        a_tt = _dot(jnp.concatenate(prods, axis=1), jnp.concatenate(sel_rows, axis=0))
        store_slab(qk_scr, (8,), tt, a_tt)
    for u in range(units):
        rows = pl.ds(u * PM_UNIT, PM_UNIT)
        x = jnp.concatenate([qk_scr[c, rows, :] for c in range(9)], axis=1).astype(_BF)
        y = _dot(perm_ref[...], x).astype(_BF)
        qk_ref[rows, :] = y[:, :8 * LANES]
        r_ref[rows, :] = y[:, 8 * LANES:]
    for gi in range(4):
        dec_ref[:, gi * LANES:(gi + 1) * LANES] = dec_scr[gi]


def _gla_prep(gq, gk, gz, w2s, bias, sel):
    m = gq.shape[0]
    tt = min(PREP_TILE, m)
    assert m % tt == 0 and tt % PM_UNIT == 0
    nb = tt // GLA_BLK
    perm = jnp.asarray(_position_major_perm(PM_UNIT).T, _BF)
    col = lambda j: pl.BlockSpec((tt, LANES), lambda i: (i, j))
    return pl.pallas_call(
        _gla_prep_kernel,
        grid=(m // tt,),
        in_specs=[col(0), col(1), col(0), col(1), col(0),
                  _const_spec(w2s.shape), _const_spec(bias.shape), _const_spec(sel.shape),
                  _const_spec((PM_UNIT, PM_UNIT))],
        out_specs=[pl.BlockSpec((tt, 8 * LANES), lambda i: (i, 0)),
                   pl.BlockSpec((nb, 4 * LANES), lambda i: (i, 0)),
                   pl.BlockSpec((tt, LANES), lambda i: (i, 0))],
        out_shape=[jax.ShapeDtypeStruct((m, 8 * LANES), _BF),
                   jax.ShapeDtypeStruct((m // GLA_BLK, 4 * LANES), _F32),
                   jax.ShapeDtypeStruct((m, LANES), _BF)],
        scratch_shapes=[pltpu.VMEM((4, tt, LANES), _F32),
                        pltpu.VMEM((4, GLA_BLK, nb, LANES), _F32),
                        pltpu.VMEM((2, GLA_BLK, nb, LANES), _BF),
                        pltpu.VMEM((9, tt, LANES), _F32),
                        pltpu.VMEM((4, nb, LANES), _F32)],
        compiler_params=pltpu.CompilerParams(dimension_semantics=("arbitrary",), vmem_limit_bytes=VMEM_LIMIT),
        name="gla_prep",
    )(gq, gq, gk, gk, gz, w2s, bias, sel, perm)


def _gla_meta_state_kernel(k_ref, z_ref, v_ref, w2_ref, bias_ref, s0_ref):
    g = _gla_log2_gates(z_ref[...], w2_ref, bias_ref)[:, :GLA_QK_WIDTH]
    r = lax.broadcasted_iota(jnp.int32, (LANES, LANES), 0)
    c = lax.broadcasted_iota(jnp.int32, (LANES, LANES), 1)
    tail = jnp.where((c > r) & (c < N_META), 1.0, 0.0).astype(_BF)
    gh, gm, gl = _split3(g)
    rest = _dot(tail, gh) + _dot(tail, gm) + _dot(tail, gl)
    is_meta = lax.broadcasted_iota(jnp.int32, (LANES, GLA_QK_WIDTH), 0) < N_META
    ke = jnp.where(is_meta, k_ref[...] * jnp.exp2(jnp.where(is_meta, rest, 0.0)), 0.0).astype(_BF)
    lane = lax.broadcasted_iota(jnp.int32, (GLA_DV, GLA_QK_WIDTH), 1)
    v = v_ref[...].astype(_F32)
    s0 = jnp.zeros((GLA_DV, GLA_QK_WIDTH), _F32)
    for h in range(GLA_HEADS):
        vt = v[:, h * GLA_DV:(h + 1) * GLA_DV].T.astype(_BF)
        in_head = (lane >= h * GLA_DK) & (lane < (h + 1) * GLA_DK)
        s0 = s0 + jnp.where(in_head, _dot(vt, ke), 0.0)
    s0_ref[...] = s0


def _gla_meta_state(gk_m, gz_m, gv_m, w2s, bias):
    return pl.pallas_call(
        _gla_meta_state_kernel,
        out_shape=jax.ShapeDtypeStruct((GLA_DV, GLA_QK_WIDTH), _F32),
        name="gla_meta_state",
    )(gk_m, gz_m, gv_m, w2s, bias)


def _gla_scan_kernel(qkf_ref, qkb_ref, rf_ref, rb_ref, df_ref, db_ref, vf_ref, vb_ref, s0_ref,
                     of_ref, ob_ref, sf_scr, sb_scr):
    n_seq = sf_scr.shape[0]

    @pl.when(pl.program_id(1) == 0)
    def _():
        for b in range(n_seq):
            sf_scr[b] = s0_ref[...]
        sb_scr[...] = jnp.zeros_like(sb_scr)

    def lane_band(shape, width, n):
        lane = lax.broadcasted_iota(jnp.int32, shape, 1)
        return [(lane >= c * width) & (lane < (c + 1) * width) for c in range(n)]

    head_lanes = lane_band((GLA_BLK, GLA_QK_WIDTH), GLA_DK, GLA_HEADS)
    a_lanes = lane_band((GLA_BLK, LANES), GLA_BLK, N_DIR * GLA_HEADS)

    def head_stack(x, bands):
        return jnp.concatenate([jnp.where(bands[h], x, jnp.zeros_like(x)) for h in range(GLA_HEADS)], axis=0)

    dirs = ((qkf_ref, rf_ref, df_ref, vf_ref, of_ref, sf_scr),
            (qkb_ref, rb_ref, db_ref, vb_ref, ob_ref, sb_scr))
    for step in range(BLK_PER_GROUP):
        for b in range(n_seq):
            for d, (qk_ref, r_ref, d_ref, v_ref, o_ref, s_scr) in enumerate(dirs):
                j = step if d == 0 else BLK_PER_GROUP - 1 - step
                rows = slice(j * GLA_BLK, (j + 1) * GLA_BLK)
                state_t = s_scr[b]
                lhs1 = head_stack(qk_ref[b, rows, :GLA_QK_WIDTH], head_lanes)
                lhs2 = head_stack(r_ref[b, rows, :], a_lanes[d * GLA_HEADS:(d + 1) * GLA_HEADS])
                vblk = v_ref[b, rows, :]
                vstack = jnp.concatenate([vblk[:, h * GLA_DV:(h + 1) * GLA_DV] for h in range(GLA_HEADS)], axis=0)
                vv = jnp.concatenate([vstack, vstack], axis=0)
                o = _dot_nt(lhs1, state_t.astype(_BF)) + _dot(lhs2, vv)
                for h in range(GLA_HEADS):
                    o_ref[h, b, rows, :] = o[h * GLA_BLK:(h + 1) * GLA_BLK, :]
                kbd = head_stack(qk_ref[b, rows, GLA_QK_WIDTH:], head_lanes)
                vst = vstack.astype(_F32).T.astype(_BF)
                s_scr[b] = state_t * d_ref[b, j:j + 1, :] + _dot(vst, kbd)


def _gla_scan(qk, r, dec, gv, s0, batch, seq):
    ng = seq // GLA_GROUP
    nbs = SCAN_SEQS if batch % SCAN_SEQS == 0 else 1
    qk = qk.reshape(batch, seq, N_DIR * 2 * GLA_QK_WIDTH)
    r = r.reshape(batch, seq, LANES)
    dec = dec.reshape(batch, seq // GLA_BLK, N_DIR * GLA_QK_WIDTH)
    gv = gv.reshape(batch, seq, GLA_WIDTH)
    fwd = lambda i: i
    bwd = lambda i: ng - 1 - i
    o_shape = jax.ShapeDtypeStruct((GLA_HEADS, batch, seq, GLA_DV), _F32)

    def spec(rows, width, grp, col):
        return pl.BlockSpec((nbs, rows, width), lambda b, i: (b, grp(i), col))

    o_f, o_b = pl.pallas_call(
        _gla_scan_kernel,
        grid=(batch // nbs, ng),
        in_specs=[spec(GLA_GROUP, 2 * GLA_QK_WIDTH, fwd, 0), spec(GLA_GROUP, 2 * GLA_QK_WIDTH, bwd, 1),
                  spec(GLA_GROUP, LANES, fwd, 0), spec(GLA_GROUP, LANES, bwd, 0),
                  spec(BLK_PER_GROUP, GLA_QK_WIDTH, fwd, 0), spec(BLK_PER_GROUP, GLA_QK_WIDTH, bwd, 1),
                  spec(GLA_GROUP, GLA_WIDTH, fwd, 0), spec(GLA_GROUP, GLA_WIDTH, bwd, 0),
                  pl.BlockSpec((GLA_DV, GLA_QK_WIDTH), lambda b, i: (0, 0))],
        out_specs=[pl.BlockSpec((GLA_HEADS, nbs, GLA_GROUP, GLA_DV), lambda b, i: (0, b, fwd(i), 0)),
                   pl.BlockSpec((GLA_HEADS, nbs, GLA_GROUP, GLA_DV), lambda b, i: (0, b, bwd(i), 0))],
        out_shape=[o_shape, o_shape],
        scratch_shapes=[pltpu.VMEM((nbs, GLA_DV, GLA_QK_WIDTH), _F32), pltpu.VMEM((nbs, GLA_DV, GLA_QK_WIDTH), _F32)],
        compiler_params=pltpu.CompilerParams(dimension_semantics=("arbitrary", "arbitrary"),
                                             vmem_limit_bytes=VMEM_LIMIT),
        name="gla_scan",
    )(qk, qk, r, r, dec, dec, gv, gv, s0)
    return (o_f.reshape(GLA_HEADS, batch * seq, GLA_DV), o_b.reshape(GLA_HEADS, batch * seq, GLA_DV))


def _stage_c_kernel(h1_ref, oatt_ref, of_ref, ob_ref, gg_ref, ng_ref, wout_ref, gm_ref, bm_ref,
                    wg_ref, wu_ref, wd_ref, g2_ref, b2_ref, out_ref, act_ref):
    def steps(rows):
        y_att = _dot(oatt_ref[rows, :], wout_ref[:ATT_WIDTH, :])
        pieces = []
        for h in range(GLA_HEADS):
            o = of_ref[h, rows, :] + ob_ref[h, rows, :]
            o = o * lax.rsqrt(jnp.mean(o * o, axis=-1, keepdims=True) + LN_EPS) * ng_ref[...]
            pieces.append((o * _silu(gg_ref[rows, h * GLA_DV:(h + 1) * GLA_DV])).astype(_BF))
        gla = jnp.concatenate(pieces, axis=1)
        yield
        y = y_att + _dot(gla, wout_ref[ATT_WIDTH:, :])
        h2 = yield from _residual_ln_steps(h1_ref[rows, :], y, gm_ref, bm_ref)
        yield
        out_ref[rows, :] = yield from _swiglu_ln_steps(h2, wg_ref, wu_ref, wd_ref, g2_ref, b2_ref,
                                                       act_ref.at[rows])

    _run_skewed([steps(rows) for rows in _sub_tiles(h1_ref.shape[0])], SUB_TILE_SKEW)


def _stage_c(h1, oatt, o_f, o_b, gg, ng, wout, gm, bm, wg, wu, wd, g2, b2):
    m, d = h1.shape
    tm = min(ROW_TILE, m)
    assert m % tm == 0
    n_ff = wg.shape[1]
    row = lambda w: pl.BlockSpec((tm, w), lambda i: (i, 0))
    hrow = pl.BlockSpec((GLA_HEADS, tm, GLA_DV), lambda i: (0, i, 0))
    return pl.pallas_call(
        _stage_c_kernel,
        grid=(m // tm,),
        in_specs=[row(d), row(ATT_WIDTH), hrow, hrow, row(GLA_WIDTH), _const_spec((1, GLA_DV)),
                  _const_spec(wout.shape), _const_spec((1, d)), _const_spec((1, d)),
                  _const_spec((d, n_ff)), _const_spec((d, n_ff)), _const_spec((n_ff, d)),
                  _const_spec((1, d)), _const_spec((1, d))],
        out_specs=row(d),
        out_shape=jax.ShapeDtypeStruct((m, d), _F32),
        scratch_shapes=[pltpu.VMEM((tm, n_ff), _BF)],
        compiler_params=pltpu.CompilerParams(dimension_semantics=("arbitrary",), vmem_limit_bytes=VMEM_LIMIT),
        name="stage_c",
    )(h1, oatt, o_f, o_b, gg, ng, wout, gm, bm, wg, wu, wd, g2, b2)


def _projection_weight(w_in):
    aq = w_in[:, 0:512]
    ak = w_in[:, 512:640]
    rest = w_in[:, 640:2304]
    gz = w_in[:, 2304:2336]
    ak_dup = jnp.concatenate([ak[:, 0:64], ak[:, 0:64], ak[:, 64:128], ak[:, 64:128]], axis=1)
    gz3 = jnp.pad(jnp.concatenate([gz, gz, gz], axis=1), ((0, 0), (0, LANES - 3 * gz.shape[1])))
    return jnp.concatenate([aq, ak_dup, rest, gz3], axis=1).astype(_BF)


def _gate_weight(w2, bias):
    k = N_DIR * GLA_RANK
    wcat = jnp.zeros((k, N_DIR * GLA_QK_WIDTH), _F32)
    for n in range(N_DIR):
        wcat = wcat.at[n * GLA_RANK:(n + 1) * GLA_RANK, n * GLA_QK_WIDTH:(n + 1) * GLA_QK_WIDTH].set(w2[n])
    hi = wcat.astype(_BF)
    mid = (wcat - hi.astype(_F32)).astype(_BF)
    pad = jnp.zeros((LANES - 3 * k, N_DIR * GLA_QK_WIDTH), _BF)
    return jnp.concatenate([hi, hi, mid, pad], axis=0), bias.reshape(1, N_DIR * GLA_QK_WIDTH)


def _pad_rows(a, rows):
    return jnp.pad(a, ((0, rows - a.shape[0]), (0, 0)))


def kernel(x, meta_tokens, ln_in_g, ln_in_b, ffn1_ln_g, ffn1_ln_b, ffn1_w_gate, ffn1_w_up, ffn1_w_down, w_in,
           attn_sink, gla_gate_w2, gla_gate_b, gla_norm_g, w_out, mix_ln_g, mix_ln_b, ffn2_w_gate, ffn2_w_up,
           ffn2_w_down, ffn2_ln_g, ffn2_ln_b):
    batch, seq, d = x.shape
    assert seq % BLOCK == 0 and seq >= _KWIN and seq % GLA_GROUP == 0
    vec = lambda a: a.reshape(1, -1)
    x2d = x.reshape(batch * seq, d)
    wg1, wu1, wd1 = ffn1_w_gate[0].astype(_BF), ffn1_w_up[0].astype(_BF), ffn1_w_down[0].astype(_BF)
    wg2, wu2, wd2 = ffn2_w_gate[0].astype(_BF), ffn2_w_up[0].astype(_BF), ffn2_w_down[0].astype(_BF)
    win = _projection_weight(w_in[0])
    w2s, gbias = _gate_weight(gla_gate_w2[0], gla_gate_b[0])
    sel = jnp.asarray(_sel_matrix(), _BF)

    a_args = (vec(ln_in_g), vec(ln_in_b), wg1, wu1, wd1, vec(ffn1_ln_g[0]), vec(ffn1_ln_b[0]), win)
    h1, aq, ak2, avt, gq, gk, gv, gg, gz = _stage_a(x2d, *a_args)
    _, _, ak2_m, avt_m, _, gk_m, gv_m, _, gz_m = _stage_a(meta_tokens.astype(x.dtype), *a_args)

    vmt = jnp.pad(avt_m, ((0, 0), (0, META_ROWS - avt_m.shape[1])))
    o_att = _attention(aq, ak2, avt, _pad_rows(ak2_m, META_ROWS), vmt, attn_sink[0].astype(_F32), batch, seq)

    qk, dec, r = _gla_prep(gq, gk, gz, w2s, gbias, sel)
    s0 = _gla_meta_state(_pad_rows(gk_m, LANES), _pad_rows(gz_m, LANES), _pad_rows(gv_m, LANES), w2s, gbias)
    o_f, o_b = _gla_scan(qk, r, dec, gv, s0, batch, seq)

    out = _stage_c(h1, o_att, o_f, o_b, gg, vec(gla_norm_g[0]), w_out[0].astype(_BF),
                   vec(mix_ln_g[0]), vec(mix_ln_b[0]), wg2, wu2, wd2, vec(ffn2_ln_g[0]), vec(ffn2_ln_b[0]))
    return out.reshape(batch, seq, d)
```

```python
import numpy as np
import jax
import jax.numpy as jnp
from jax import lax
from jax.experimental import pallas as pl
from jax.experimental.pallas import tpu as pltpu

N_META = 16
ATT_HEADS = 8
ATT_KV_HEADS = 2
ATT_HEAD_DIM = 64
ATT_REP = ATT_HEADS // ATT_KV_HEADS
WINDOW = 128
BLOCK = 128
GLA_HEADS = 4
GLA_DK = 64
GLA_DV = 128
GLA_RANK = 16
GLA_TAU = 16.0
N_DIR = 2
ATT_WIDTH = ATT_HEADS * ATT_HEAD_DIM
ATT_KV_WIDTH = ATT_KV_HEADS * ATT_HEAD_DIM
GLA_QK_WIDTH = GLA_HEADS * GLA_DK
GLA_WIDTH = GLA_HEADS * GLA_DV
LN_EPS = 1e-5
DEPTH = 1
DN_ALPHA = (2.0 * DEPTH) ** 0.25

LANES = 128
GLA_BLK = 16
GLA_GROUP = 128
BLK_PER_GROUP = GLA_GROUP // GLA_BLK
SCAN_SEQS = 8
FF_CHUNK = 256
ROW_TILE = 512
ROW_SUB_TILES = 2
SUB_TILE_SKEW = 3
LN_PIECES = 4
PM_UNIT = ROW_TILE // ROW_SUB_TILES
PREP_TILE = 2048
VMEM_LIMIT = 56 * 1024 * 1024

_BF = jnp.bfloat16
_F32 = jnp.float32


def _dot(a, b):
    return jnp.dot(a, b, preferred_element_type=_F32)


def _dot_nt(a, b):
    return lax.dot_general(a, b, (((1,), (1,)), ((), ())), preferred_element_type=_F32)


def _layer_norm(x, g, b):
    mu = jnp.mean(x, axis=-1, keepdims=True)
    xc = x - mu
    var = jnp.mean(xc * xc, axis=-1, keepdims=True)
    return xc * lax.rsqrt(var + LN_EPS) * g + b


def _silu(x):
    return x * (1.0 / (1.0 + jnp.exp(-x)))


def _split3(x):
    hi = x.astype(_BF)
    r1 = x - hi.astype(_F32)
    mid = r1.astype(_BF)
    lo = (r1 - mid.astype(_F32)).astype(_BF)
    return hi, mid, lo


def _sub_tiles(rows):
    n = ROW_SUB_TILES if rows % (ROW_SUB_TILES * 16) == 0 else 1
    return [pl.ds(s * (rows // n), rows // n) for s in range(n)]


def _run_skewed(step_gens, skew):
    pending = list(enumerate(step_gens))
    tick = 0
    while pending:
        for k, gen in list(pending):
            if tick < k * skew:
                continue
            try:
                next(gen)
            except StopIteration:
                pending.remove((k, gen))
        tick += 1


def _swiglu_ln_steps(h, wg_ref, wu_ref, wd_ref, g_ref, b_ref, act_ref):
    hb = h.astype(_BF)
    n_ff = wg_ref.shape[1]
    assert n_ff % FF_CHUNK == 0
    for c in range(n_ff // FF_CHUNK):
        sl = slice(c * FF_CHUNK, (c + 1) * FF_CHUNK)
        gate = _dot(hb, wg_ref[:, sl])
        up = _dot(hb, wu_ref[:, sl])
        act_ref[:, sl] = (_silu(gate) * up).astype(_BF)
        yield
    y = _dot(act_ref[...], wd_ref[...])
    return (yield from _residual_ln_steps(h, 0.5 * y, g_ref, b_ref))


def _residual_ln_steps(h, branch, g_ref, b_ref):
    rows = h.shape[0]
    piece = rows // LN_PIECES if rows % (LN_PIECES * 8) == 0 else rows
    out = []
    for r in range(0, rows, piece):
        out.append(_layer_norm(DN_ALPHA * h[r:r + piece] + branch[r:r + piece], g_ref[...], b_ref[...]))
        if r + piece < rows:
            yield
    return out[0] if len(out) == 1 else jnp.concatenate(out, axis=0)


_P_AQ = (0, 512)
_P_AK = (512, 768)
_P_AV = (768, 896)
_P_GQ = (896, 1152)
_P_GK = (1152, 1408)
_P_GV = (1408, 1920)
_P_GG = (1920, 2432)
_P_GZ = (2432, 2560)
_P_WIDTH = 2560


def _position_major_perm(rows):
    nblk = rows // GLA_BLK
    perm = np.zeros((rows, rows), np.float32)
    for n in range(nblk):
        for t in range(GLA_BLK):
            perm[t * nblk + n, n * GLA_BLK + t] = 1.0
    return perm


def _stage_a_kernel(x_ref, g0_ref, b0_ref, wg_ref, wu_ref, wd_ref, g1_ref, b1_ref, win_ref, perm_ref,
                    h1_ref, aq_ref, ak_ref, avt_ref, gq_ref, gk_ref, gv_ref, gg_ref, gz_ref, act_ref):
    def steps(rows):
        h0 = _layer_norm(x_ref[rows, :], g0_ref[...], b0_ref[...])
        yield
        h1 = yield from _swiglu_ln_steps(h0, wg_ref, wu_ref, wd_ref, g1_ref, b1_ref, act_ref.at[rows])
        h1_ref[rows, :] = h1
        hb = h1.astype(_BF)
        hb_pm = _dot(perm_ref[...], hb).astype(_BF)
        yield

        def proj(act, cols):
            return _dot(act, win_ref[:, cols[0]:cols[1]])

        aq_ref[rows, :] = (proj(hb, _P_AQ) * (ATT_HEAD_DIM ** -0.5)).astype(_BF)
        ak_ref[rows, :] = proj(hb, _P_AK).astype(_BF)
        avt_ref[:, rows] = proj(hb, _P_AV).T.astype(_BF)
        gq_ref[rows, :] = proj(hb_pm, _P_GQ) * (GLA_DK ** -0.5)
        yield
        gk_ref[rows, :] = proj(hb_pm, _P_GK)
        gv_ref[rows, :] = proj(hb, _P_GV).astype(_BF)
        gg_ref[rows, :] = proj(hb, _P_GG)
        gz_ref[rows, :] = proj(hb_pm, _P_GZ)

    _run_skewed([steps(rows) for rows in _sub_tiles(x_ref.shape[0])], SUB_TILE_SKEW)


def _const_spec(shape):
    nd = len(shape)
    return pl.BlockSpec(shape, lambda *_: (0,) * nd, pipeline_mode=pl.Buffered(1))


def _stage_a(x2d, g0, b0, wg, wu, wd, g1, b1, win):
    m, d = x2d.shape
    tm = min(ROW_TILE, m)
    assert m % tm == 0
    n_ff = wg.shape[1]
    sub = _sub_tiles(tm)[0].size
    perm = jnp.asarray(_position_major_perm(sub), _BF)
    row = lambda w: pl.BlockSpec((tm, w), lambda i: (i, 0))
    out_widths = [(d, _F32), (512, _BF), (256, _BF), None, (256, _F32), (256, _F32), (512, _BF), (512, _F32),
                  (128, _F32)]
    out_specs = [row(o[0]) if o else pl.BlockSpec((ATT_KV_WIDTH, tm), lambda i: (0, i)) for o in out_widths]
    out_shape = [jax.ShapeDtypeStruct((m, o[0]), o[1]) if o else jax.ShapeDtypeStruct((ATT_KV_WIDTH, m), _BF)
                 for o in out_widths]
    return pl.pallas_call(
        _stage_a_kernel,
        grid=(m // tm,),
        in_specs=[row(d), _const_spec((1, d)), _const_spec((1, d)), _const_spec((d, n_ff)), _const_spec((d, n_ff)),
                  _const_spec((n_ff, d)), _const_spec((1, d)), _const_spec((1, d)), _const_spec((d, _P_WIDTH)),
                  _const_spec((sub, sub))],
        out_specs=out_specs,
        out_shape=out_shape,
        scratch_shapes=[pltpu.VMEM((tm, n_ff), _BF)],
        compiler_params=pltpu.CompilerParams(dimension_semantics=("arbitrary",), vmem_limit_bytes=VMEM_LIMIT),
        name="stage_a",
    )(x2d, g0, b0, wg, wu, wd, g1, b1, win, perm)


_KWIN = 3 * BLOCK


def _alibi_slopes():
    return [2.0 ** (-8.0 * (i + 1) / ATT_HEADS) for i in range(ATT_HEADS)]


_WIN_OFFSETS = (0, -BLOCK, -2 * BLOCK)
ATT_QBLOCKS = 16
META_ROWS = 32


def _attn_kernel(sink_ref, q_ref, k_ref, vt_ref, km_ref, vmt_ref, o_ref, band_scr, meta_scr):
    i = pl.program_id(1)
    s_len = k_ref.shape[0]
    nq = s_len // BLOCK
    pairs_per_group = ATT_REP // 2

    @pl.when((pl.program_id(0) == 0) & (i == 0))
    def _():
        col = lax.broadcasted_iota(jnp.int32, (_KWIN, 2 * BLOCK), 1)
        rel0 = lax.broadcasted_iota(jnp.int32, (_KWIN, 2 * BLOCK), 0) - jnp.where(col < BLOCK, col, col - BLOCK)
        mrow = lax.broadcasted_iota(jnp.int32, (META_ROWS, 2 * BLOCK), 0)
        mcol = lax.broadcasted_iota(jnp.int32, (META_ROWS, 2 * BLOCK), 1)
        slopes = _alibi_slopes()
        for g in range(ATT_KV_HEADS):
            for pl_ in range(pairs_per_group):
                h0 = g * ATT_REP + 2 * pl_
                slope = jnp.where(col < BLOCK, slopes[h0], slopes[h0 + 1])
                for var, off in enumerate(_WIN_OFFSETS):
                    absrel = jnp.abs(rel0 + off).astype(_F32)
                    band_scr[var, g, pl_] = jnp.where(absrel <= float(WINDOW), -slope * absrel, -jnp.inf)
                sink = jnp.where(mcol < BLOCK, sink_ref[h0], sink_ref[h0 + 1])
                meta_scr[g, pl_] = jnp.where(mrow < N_META, 0.0, jnp.where(mrow == N_META, sink, -jnp.inf))

    n_blk = q_ref.shape[0] // BLOCK
    starts, variants = [], []
    for blk in range(n_blk):
        ib = i * n_blk + blk
        starts.append(pl.multiple_of(jnp.clip((ib - 1) * BLOCK, 0, s_len - _KWIN), BLOCK))
        variants.append(jnp.where(ib == 0, 0, jnp.where(ib == nq - 1, 2, 1)))
    low_half = lax.broadcasted_iota(jnp.int32, (BLOCK, LANES), 1) < ATT_HEAD_DIM
    zero = jnp.zeros((BLOCK, LANES), _BF)

    chains = [(blk, g, pl_) for blk in range(n_blk) for g in range(ATT_KV_HEADS) for pl_ in range(pairs_per_group)]

    def scores(blk, g, pl_):
        lanes_g = slice(g * LANES, (g + 1) * LANES)
        pair = g * pairs_per_group + pl_
        qp = q_ref[blk * BLOCK:(blk + 1) * BLOCK, pair * LANES:(pair + 1) * LANES]
        qs = jnp.concatenate([jnp.where(low_half, qp, zero), jnp.where(low_half, zero, qp)], axis=0)
        kg = k_ref[pl.ds(starts[blk], _KWIN), lanes_g]
        s_band = _dot_nt(kg, qs) + band_scr[variants[blk], g, pl_]
        s_meta = _dot_nt(km_ref[:, lanes_g], qs) + meta_scr[g, pl_]
        return s_band, s_meta

    def rows_reduce(op, arrays):
        parts = [a[r:r + 8, :] for a in arrays for r in range(0, a.shape[0], 8)]
        while len(parts) > 1:
            parts = [op(parts[j], parts[j + 1]) if j + 1 < len(parts) else parts[j]
                     for j in range(0, len(parts), 2)]
        return parts[0]

    def col_max(s_band, s_meta):
        return s_band, s_meta, jnp.max(rows_reduce(jnp.maximum, [s_band, s_meta]), axis=0, keepdims=True)

    def softmax(s_band, s_meta, mx):
        return jnp.exp(s_band - mx).astype(_BF), jnp.exp(s_meta - mx).astype(_BF)

    ones_band = jnp.ones((16, _KWIN), _BF)
    ones_meta = jnp.where(lax.broadcasted_iota(jnp.int32, (16, META_ROWS), 1) <= N_META, 1.0, 0.0).astype(_BF)

    def values(blk, g, pl_, p_band, p_meta):
        drows = slice(g * ATT_HEAD_DIM, (g + 1) * ATT_HEAD_DIM)
        vt = jnp.concatenate([vt_ref[drows, pl.ds(starts[blk], _KWIN)], ones_band], axis=0)
        vmt = jnp.concatenate([vmt_ref[drows, :], ones_meta], axis=0)
        acc = _dot(vt, p_band) + _dot(vmt, p_meta)
        ot = acc[:ATT_HEAD_DIM, :] * (1.0 / acc[ATT_HEAD_DIM:ATT_HEAD_DIM + 1, :])
        o2 = jnp.concatenate([ot[:, :BLOCK], ot[:, BLOCK:]], axis=0)
        pair = g * pairs_per_group + pl_
        o_ref[blk * BLOCK:(blk + 1) * BLOCK, pair * LANES:(pair + 1) * LANES] = o2.T.astype(_BF)

    s_vals, m_vals, p_vals = {}, {}, {}
    for t in range(len(chains) + 3):
        if t < len(chains):
            s_vals[t] = scores(*chains[t])
        if 0 <= t - 1 < len(chains):
            m_vals[t - 1] = col_max(*s_vals.pop(t - 1))
        if 0 <= t - 2 < len(chains):
            p_vals[t - 2] = softmax(*m_vals.pop(t - 2))
        if 0 <= t - 3 < len(chains):
            values(*chains[t - 3], *p_vals.pop(t - 3))


def _attention(aq, ak2, avt, km2, vmt, sink, batch, seq):
    m = aq.shape[0]
    nblk = ATT_QBLOCKS if (seq // BLOCK) % ATT_QBLOCKS == 0 else 1
    nq = seq // (BLOCK * nblk)
    return pl.pallas_call(
        _attn_kernel,
        grid=(batch, nq),
        in_specs=[pl.BlockSpec(memory_space=pltpu.SMEM),
                  pl.BlockSpec((nblk * BLOCK, ATT_WIDTH), lambda b, i: (b * nq + i, 0)),
                  pl.BlockSpec((seq, 2 * LANES), lambda b, i: (b, 0)),
                  pl.BlockSpec((ATT_KV_WIDTH, seq), lambda b, i: (0, b)),
                  pl.BlockSpec((META_ROWS, 2 * LANES), lambda b, i: (0, 0)),
                  pl.BlockSpec((ATT_KV_WIDTH, META_ROWS), lambda b, i: (0, 0))],
        out_specs=pl.BlockSpec((nblk * BLOCK, ATT_WIDTH), lambda b, i: (b * nq + i, 0)),
        out_shape=jax.ShapeDtypeStruct((m, ATT_WIDTH), _BF),
        scratch_shapes=[pltpu.VMEM((len(_WIN_OFFSETS), ATT_KV_HEADS, ATT_REP // 2, _KWIN, 2 * BLOCK), _F32),
                        pltpu.VMEM((ATT_KV_HEADS, ATT_REP // 2, META_ROWS, 2 * BLOCK), _F32)],
        compiler_params=pltpu.CompilerParams(dimension_semantics=("arbitrary", "arbitrary"),
                                             vmem_limit_bytes=VMEM_LIMIT),
        name="attention",
    )(sink, aq, ak2, avt, km2, vmt)


_LOG2E = 1.4426950408889634


def _gla_log2_gates(z, w2_ref, bias_ref):
    zh = z.astype(_BF)
    zm = (z - zh.astype(_F32)).astype(_BF)
    lane = lax.broadcasted_iota(jnp.int32, z.shape, 1)
    mid_lanes = (lane >= N_DIR * GLA_RANK) & (lane < 2 * N_DIR * GLA_RANK)
    logits = _dot(jnp.where(mid_lanes, zm, zh), w2_ref[...]) + bias_ref[...]
    y = logits * _LOG2E
    log2_sig = jnp.minimum(y, 0.0) - jnp.log2(1.0 + jnp.exp2(-jnp.abs(y)))
    return log2_sig * (1.0 / GLA_TAU)


def _sel_matrix():
    sel = np.zeros((N_DIR, 2, GLA_BLK, LANES, LANES), np.float32)
    for d in range(N_DIR):
        for lg in range(2):
            for s in range(GLA_BLK):
                for hl in range(2):
                    col = d * 64 + (2 * lg + hl) * GLA_BLK + s
                    sel[d, lg, s, hl * GLA_DK:(hl + 1) * GLA_DK, col] = 1.0
    return sel.reshape(N_DIR, 2 * GLA_BLK * LANES, LANES)


_KEY_SIDE = (tuple(range(GLA_BLK // 2)), tuple(range(GLA_BLK // 2, GLA_BLK)))
_MID = (GLA_BLK // 2 - 1, GLA_BLK // 2)


def _gla_prep_kernel(q0_ref, q1_ref, k0_ref, k1_ref, z_ref, w2_ref, bias_ref, sel_ref, perm_ref,
                     qk_ref, dec_ref, r_ref,
                     g_scr, b_scr, kbf_scr, kg_scr, qk_scr, dec_scr):
    tt_rows = z_ref.shape[0]
    units = tt_rows // PM_UNIT
    ub = PM_UNIT // GLA_BLK
    rc = min(512, tt_rows)
    for c in range(tt_rows // rc):
        rows = pl.ds(c * rc, rc)
        g = _gla_log2_gates(z_ref[rows, :], w2_ref, bias_ref)
        for gi in range(4):
            g_scr[gi, rows, :] = g[:, gi * LANES:(gi + 1) * LANES]

    def slab(ref, tt, lead=()):
        return jnp.concatenate([ref[lead + (pl.ds(u * PM_UNIT + tt * ub, ub), slice(None))] for u in range(units)],
                               axis=0)

    def store_slab(ref, lead, tt, val):
        for u in range(units):
            ref[lead + (pl.ds(u * PM_UNIT + tt * ub, ub), slice(None))] = val[u * ub:(u + 1) * ub, :]

    qk_refs = ((q0_ref, k0_ref), (q1_ref, k1_ref))
    last = {}
    for lg, (_, k_ref) in enumerate(qk_refs):
        for tt in range(GLA_BLK):
            kbf_scr[lg, tt] = slab(k_ref, tt).astype(_BF)
        for d in range(N_DIR):
            gi = d * 2 + lg
            acc = None
            for tt in (range(GLA_BLK) if d == 0 else reversed(range(GLA_BLK))):
                gs = slab(g_scr, tt, (gi,))
                acc = gs if acc is None else acc + gs
                b_scr[gi, tt] = acc
            last[gi] = acc
            dec_scr[gi] = jnp.exp2(acc)
            mid = _MID[d]
            for j, s in enumerate(_KEY_SIDE[d]):
                kg_scr[gi, j] = (slab(k_ref, s) * jnp.exp2(b_scr[gi, mid] - b_scr[gi, s])).astype(_BF)
    for tt in range(GLA_BLK):
        prods, sel_rows = [], []
        for d in range(N_DIR):
            positions = list(range(0, tt + 1)) if d == 0 else list(range(tt + 1, GLA_BLK))
            for lg, (q_ref, k_ref) in enumerate(qk_refs):
                gi = d * 2 + lg
                q = slab(q_ref, tt)
                b = b_scr[gi, tt]
                store_slab(qk_scr, (d * 4 + lg,), tt, q * jnp.exp2(b))
                store_slab(qk_scr, (d * 4 + 2 + lg,), tt, slab(k_ref, tt) * jnp.exp2(last[gi] - b))
                if not positions:
                    continue
                qb = q.astype(_BF)
                crosses = tt not in _KEY_SIDE[d]
                if crosses:
                    qf = (q * jnp.exp2(b - b_scr[gi, _MID[d]])).astype(_BF)
                for s in positions:
                    if crosses and s in _KEY_SIDE[d]:
                        p = qf * kg_scr[gi, _KEY_SIDE[d].index(s)]
                    else:
                        p = qb * kbf_scr[lg, s]
                        if s != tt:
                            p = p * jnp.exp2(b - b_scr[gi, s]).astype(_BF)
                    prods.append(p)
                r0 = lg * GLA_BLK * LANES + positions[0] * LANES
                sel_rows.append(sel_ref[d, r0:r0 + len(positions) * LANES, :])
        a_tt = _dot(jnp.concatenate(prods, axis=1), jnp.concatenate(sel_rows, axis=0))
        store_slab(qk_scr, (8,), tt, a_tt)
    for u in range(units):
        rows = pl.ds(u * PM_UNIT, PM_UNIT)
        x = jnp.concatenate([qk_scr[c, rows, :] for c in range(9)], axis=1).astype(_BF)
        y = _dot(perm_ref[...], x).astype(_BF)
        qk_ref[rows, :] = y[:, :8 * LANES]
        r_ref[rows, :] = y[:, 8 * LANES:]
    for gi in range(4):
        dec_ref[:, gi * LANES:(gi + 1) * LANES] = dec_scr[gi]


def _gla_prep(gq, gk, gz, w2s, bias, sel):
    m = gq.shape[0]
    tt = min(PREP_TILE, m)
    assert m % tt == 0 and tt % PM_UNIT == 0
    nb = tt // GLA_BLK
    perm = jnp.asarray(_position_major_perm(PM_UNIT).T, _BF)
    col = lambda j: pl.BlockSpec((tt, LANES), lambda i: (i, j))
    return pl.pallas_call(
        _gla_prep_kernel,
        grid=(m // tt,),
        in_specs=[col(0), col(1), col(0), col(1), col(0),
                  _const_spec(w2s.shape), _const_spec(bias.shape), _const_spec(sel.shape),
                  _const_spec((PM_UNIT, PM_UNIT))],
        out_specs=[pl.BlockSpec((tt, 8 * LANES), lambda i: (i, 0)),
                   pl.BlockSpec((nb, 4 * LANES), lambda i: (i, 0)),
                   pl.BlockSpec((tt, LANES), lambda i: (i, 0))],
        out_shape=[jax.ShapeDtypeStruct((m, 8 * LANES), _BF),
                   jax.ShapeDtypeStruct((m // GLA_BLK, 4 * LANES), _F32),
                   jax.ShapeDtypeStruct((m, LANES), _BF)],
        scratch_shapes=[pltpu.VMEM((4, tt, LANES), _F32),
                        pltpu.VMEM((4, GLA_BLK, nb, LANES), _F32),
                        pltpu.VMEM((2, GLA_BLK, nb, LANES), _BF),
                        pltpu.VMEM((4, GLA_BLK // 2, nb, LANES), _BF),
                        pltpu.VMEM((9, tt, LANES), _F32),
                        pltpu.VMEM((4, nb, LANES), _F32)],
        compiler_params=pltpu.CompilerParams(dimension_semantics=("arbitrary",), vmem_limit_bytes=VMEM_LIMIT),
        name="gla_prep",
    )(gq, gq, gk, gk, gz, w2s, bias, sel, perm)


def _gla_meta_state_kernel(k_ref, z_ref, v_ref, w2_ref, bias_ref, s0_ref):
    g = _gla_log2_gates(z_ref[...], w2_ref, bias_ref)[:, :GLA_QK_WIDTH]
    r = lax.broadcasted_iota(jnp.int32, (LANES, LANES), 0)
    c = lax.broadcasted_iota(jnp.int32, (LANES, LANES), 1)
    tail = jnp.where((c > r) & (c < N_META), 1.0, 0.0).astype(_BF)
    gh, gm, gl = _split3(g)
    rest = _dot(tail, gh) + _dot(tail, gm) + _dot(tail, gl)
    is_meta = lax.broadcasted_iota(jnp.int32, (LANES, GLA_QK_WIDTH), 0) < N_META
    ke = jnp.where(is_meta, k_ref[...] * jnp.exp2(jnp.where(is_meta, rest, 0.0)), 0.0).astype(_BF)
    lane = lax.broadcasted_iota(jnp.int32, (GLA_DV, GLA_QK_WIDTH), 1)
    v = v_ref[...].astype(_F32)
    s0 = jnp.zeros((GLA_DV, GLA_QK_WIDTH), _F32)
    for h in range(GLA_HEADS):
        vt = v[:, h * GLA_DV:(h + 1) * GLA_DV].T.astype(_BF)
        in_head = (lane >= h * GLA_DK) & (lane < (h + 1) * GLA_DK)
        s0 = s0 + jnp.where(in_head, _dot(vt, ke), 0.0)
    s0_ref[...] = s0


def _gla_meta_state(gk_m, gz_m, gv_m, w2s, bias):
    return pl.pallas_call(
        _gla_meta_state_kernel,
        out_shape=jax.ShapeDtypeStruct((GLA_DV, GLA_QK_WIDTH), _F32),
        name="gla_meta_state",
    )(gk_m, gz_m, gv_m, w2s, bias)


def _gla_scan_kernel(qkf_ref, qkb_ref, rf_ref, rb_ref, df_ref, db_ref, vf_ref, vb_ref, s0_ref,
                     of_ref, ob_ref, sf_scr, sb_scr):
    n_seq = sf_scr.shape[0]

    @pl.when(pl.program_id(1) == 0)
    def _():
        for b in range(n_seq):
            sf_scr[b] = s0_ref[...]
        sb_scr[...] = jnp.zeros_like(sb_scr)

    def lane_band(shape, width, n):
        lane = lax.broadcasted_iota(jnp.int32, shape, 1)
        return [(lane >= c * width) & (lane < (c + 1) * width) for c in range(n)]

    head_lanes = lane_band((GLA_BLK, GLA_QK_WIDTH), GLA_DK, GLA_HEADS)
    a_lanes = lane_band((GLA_BLK, LANES), GLA_BLK, N_DIR * GLA_HEADS)

    def head_stack(x, bands):
        return jnp.concatenate([jnp.where(bands[h], x, jnp.zeros_like(x)) for h in range(GLA_HEADS)], axis=0)

    dirs = ((qkf_ref, rf_ref, df_ref, vf_ref, of_ref, sf_scr),
            (qkb_ref, rb_ref, db_ref, vb_ref, ob_ref, sb_scr))
    for step in range(BLK_PER_GROUP):
        for b in range(n_seq):
            for d, (qk_ref, r_ref, d_ref, v_ref, o_ref, s_scr) in enumerate(dirs):
                j = step if d == 0 else BLK_PER_GROUP - 1 - step
                rows = slice(j * GLA_BLK, (j + 1) * GLA_BLK)
                state_t = s_scr[b]
                lhs1 = head_stack(qk_ref[b, rows, :GLA_QK_WIDTH], head_lanes)
                lhs2 = head_stack(r_ref[b, rows, :], a_lanes[d * GLA_HEADS:(d + 1) * GLA_HEADS])
                vblk = v_ref[b, rows, :]
                vstack = jnp.concatenate([vblk[:, h * GLA_DV:(h + 1) * GLA_DV] for h in range(GLA_HEADS)], axis=0)
                vv = jnp.concatenate([vstack, vstack], axis=0)
                o = _dot_nt(lhs1, state_t.astype(_BF)) + _dot(lhs2, vv)
                for h in range(GLA_HEADS):
                    o_ref[h, b, rows, :] = o[h * GLA_BLK:(h + 1) * GLA_BLK, :]
                kbd = head_stack(qk_ref[b, rows, GLA_QK_WIDTH:], head_lanes)
                vst = vstack.astype(_F32).T.astype(_BF)
                s_scr[b] = state_t * d_ref[b, j:j + 1, :] + _dot(vst, kbd)


def _gla_scan(qk, r, dec, gv, s0, batch, seq):
    ng = seq // GLA_GROUP
    nbs = SCAN_SEQS if batch % SCAN_SEQS == 0 else 1
    qk = qk.reshape(batch, seq, N_DIR * 2 * GLA_QK_WIDTH)
    r = r.reshape(batch, seq, LANES)
    dec = dec.reshape(batch, seq // GLA_BLK, N_DIR * GLA_QK_WIDTH)
    gv = gv.reshape(batch, seq, GLA_WIDTH)
    fwd = lambda i: i
    bwd = lambda i: ng - 1 - i
    o_shape = jax.ShapeDtypeStruct((GLA_HEADS, batch, seq, GLA_DV), _F32)

    def spec(rows, width, grp, col):
        return pl.BlockSpec((nbs, rows, width), lambda b, i: (b, grp(i), col))

    o_f, o_b = pl.pallas_call(
        _gla_scan_kernel,
        grid=(batch // nbs, ng),
        in_specs=[spec(GLA_GROUP, 2 * GLA_QK_WIDTH, fwd, 0), spec(GLA_GROUP, 2 * GLA_QK_WIDTH, bwd, 1),
                  spec(GLA_GROUP, LANES, fwd, 0), spec(GLA_GROUP, LANES, bwd, 0),
                  spec(BLK_PER_GROUP, GLA_QK_WIDTH, fwd, 0), spec(BLK_PER_GROUP, GLA_QK_WIDTH, bwd, 1),
                  spec(GLA_GROUP, GLA_WIDTH, fwd, 0), spec(GLA_GROUP, GLA_WIDTH, bwd, 0),
                  pl.BlockSpec((GLA_DV, GLA_QK_WIDTH), lambda b, i: (0, 0))],
        out_specs=[pl.BlockSpec((GLA_HEADS, nbs, GLA_GROUP, GLA_DV), lambda b, i: (0, b, fwd(i), 0)),
                   pl.BlockSpec((GLA_HEADS, nbs, GLA_GROUP, GLA_DV), lambda b, i: (0, b, bwd(i), 0))],
        out_shape=[o_shape, o_shape],
        scratch_shapes=[pltpu.VMEM((nbs, GLA_DV, GLA_QK_WIDTH), _F32), pltpu.VMEM((nbs, GLA_DV, GLA_QK_WIDTH), _F32)],
        compiler_params=pltpu.CompilerParams(dimension_semantics=("arbitrary", "arbitrary"),
                                             vmem_limit_bytes=VMEM_LIMIT),
        name="gla_scan",
    )(qk, qk, r, r, dec, dec, gv, gv, s0)
    return (o_f.reshape(GLA_HEADS, batch * seq, GLA_DV), o_b.reshape(GLA_HEADS, batch * seq, GLA_DV))


def _stage_c_kernel(h1_ref, oatt_ref, of_ref, ob_ref, gg_ref, ng_ref, wout_ref, gm_ref, bm_ref,
                    wg_ref, wu_ref, wd_ref, g2_ref, b2_ref, out_ref, act_ref):
    def steps(rows):
        y_att = _dot(oatt_ref[rows, :], wout_ref[:ATT_WIDTH, :])
        pieces = []
        for h in range(GLA_HEADS):
            o = of_ref[h, rows, :] + ob_ref[h, rows, :]
            o = o * lax.rsqrt(jnp.mean(o * o, axis=-1, keepdims=True) + LN_EPS) * ng_ref[...]
            pieces.append((o * _silu(gg_ref[rows, h * GLA_DV:(h + 1) * GLA_DV])).astype(_BF))
        gla = jnp.concatenate(pieces, axis=1)
        yield
        y = y_att + _dot(gla, wout_ref[ATT_WIDTH:, :])
        h2 = yield from _residual_ln_steps(h1_ref[rows, :], y, gm_ref, bm_ref)
        yield
        out_ref[rows, :] = yield from _swiglu_ln_steps(h2, wg_ref, wu_ref, wd_ref, g2_ref, b2_ref,
                                                       act_ref.at[rows])

    _run_skewed([steps(rows) for rows in _sub_tiles(h1_ref.shape[0])], SUB_TILE_SKEW)


def _stage_c(h1, oatt, o_f, o_b, gg, ng, wout, gm, bm, wg, wu, wd, g2, b2):
    m, d = h1.shape
    tm = min(ROW_TILE, m)
    assert m % tm == 0
    n_ff = wg.shape[1]
    row = lambda w: pl.BlockSpec((tm, w), lambda i: (i, 0))
    hrow = pl.BlockSpec((GLA_HEADS, tm, GLA_DV), lambda i: (0, i, 0))
    return pl.pallas_call(
        _stage_c_kernel,
        grid=(m // tm,),
        in_specs=[row(d), row(ATT_WIDTH), hrow, hrow, row(GLA_WIDTH), _const_spec((1, GLA_DV)),
                  _const_spec(wout.shape), _const_spec((1, d)), _const_spec((1, d)),
                  _const_spec((d, n_ff)), _const_spec((d, n_ff)), _const_spec((n_ff, d)),
                  _const_spec((1, d)), _const_spec((1, d))],
        out_specs=row(d),
        out_shape=jax.ShapeDtypeStruct((m, d), _F32),
        scratch_shapes=[pltpu.VMEM((tm, n_ff), _BF)],
        compiler_params=pltpu.CompilerParams(dimension_semantics=("arbitrary",), vmem_limit_bytes=VMEM_LIMIT),
        name="stage_c",
    )(h1, oatt, o_f, o_b, gg, ng, wout, gm, bm, wg, wu, wd, g2, b2)


def _projection_weight(w_in):
    aq = w_in[:, 0:512]
    ak = w_in[:, 512:640]
    rest = w_in[:, 640:2304]
    gz = w_in[:, 2304:2336]
    ak_dup = jnp.concatenate([ak[:, 0:64], ak[:, 0:64], ak[:, 64:128], ak[:, 64:128]], axis=1)
    gz3 = jnp.pad(jnp.concatenate([gz, gz, gz], axis=1), ((0, 0), (0, LANES - 3 * gz.shape[1])))
    return jnp.concatenate([aq, ak_dup, rest, gz3], axis=1).astype(_BF)


def _gate_weight(w2, bias):
    k = N_DIR * GLA_RANK
    wcat = jnp.zeros((k, N_DIR * GLA_QK_WIDTH), _F32)
    for n in range(N_DIR):
        wcat = wcat.at[n * GLA_RANK:(n + 1) * GLA_RANK, n * GLA_QK_WIDTH:(n + 1) * GLA_QK_WIDTH].set(w2[n])
    hi = wcat.astype(_BF)
    mid = (wcat - hi.astype(_F32)).astype(_BF)
    pad = jnp.zeros((LANES - 3 * k, N_DIR * GLA_QK_WIDTH), _BF)
    return jnp.concatenate([hi, hi, mid, pad], axis=0), bias.reshape(1, N_DIR * GLA_QK_WIDTH)


def _pad_rows(a, rows):
    return jnp.pad(a, ((0, rows - a.shape[0]), (0, 0)))


def kernel(x, meta_tokens, ln_in_g, ln_in_b, ffn1_ln_g, ffn1_ln_b, ffn1_w_gate, ffn1_w_up, ffn1_w_down, w_in,
           attn_sink, gla_gate_w2, gla_gate_b, gla_norm_g, w_out, mix_ln_g, mix_ln_b, ffn2_w_gate, ffn2_w_up,
           ffn2_w_down, ffn2_ln_g, ffn2_ln_b):
    batch, seq, d = x.shape
    assert seq % BLOCK == 0 and seq >= _KWIN and seq % GLA_GROUP == 0
    vec = lambda a: a.reshape(1, -1)
    x2d = x.reshape(batch * seq, d)
    wg1, wu1, wd1 = ffn1_w_gate[0].astype(_BF), ffn1_w_up[0].astype(_BF), ffn1_w_down[0].astype(_BF)
    wg2, wu2, wd2 = ffn2_w_gate[0].astype(_BF), ffn2_w_up[0].astype(_BF), ffn2_w_down[0].astype(_BF)
    win = _projection_weight(w_in[0])
    w2s, gbias = _gate_weight(gla_gate_w2[0], gla_gate_b[0])
    sel = jnp.asarray(_sel_matrix(), _BF)

    a_args = (vec(ln_in_g), vec(ln_in_b), wg1, wu1, wd1, vec(ffn1_ln_g[0]), vec(ffn1_ln_b[0]), win)
    h1, aq, ak2, avt, gq, gk, gv, gg, gz = _stage_a(x2d, *a_args)
    _, _, ak2_m, avt_m, _, gk_m, gv_m, _, gz_m = _stage_a(meta_tokens.astype(x.dtype), *a_args)

    vmt = jnp.pad(avt_m, ((0, 0), (0, META_ROWS - avt_m.shape[1])))
    o_att = _attention(aq, ak2, avt, _pad_rows(ak2_m, META_ROWS), vmt, attn_sink[0].astype(_F32), batch, seq)

    qk, dec, r = _gla_prep(gq, gk, gz, w2s, gbias, sel)
    s0 = _gla_meta_state(_pad_rows(gk_m, LANES), _pad_rows(gz_m, LANES), _pad_rows(gv_m, LANES), w2s, gbias)
    o_f, o_b = _gla_scan(qk, r, dec, gv, s0, batch, seq)

    out = _stage_c(h1, o_att, o_f, o_b, gg, vec(gla_norm_g[0]), w_out[0].astype(_BF),
                   vec(mix_ln_g[0]), vec(mix_ln_b[0]), wg2, wu2, wd2, vec(ffn2_ln_g[0]), vec(ffn2_ln_b[0]))
    return out.reshape(batch, seq, d)
```

```python
import numpy as np
import jax
import jax.numpy as jnp
from jax import lax
from jax.experimental import pallas as pl
from jax.experimental.pallas import tpu as pltpu

N_META = 16
ATT_HEADS = 8
ATT_KV_HEADS = 2
ATT_HEAD_DIM = 64
ATT_REP = ATT_HEADS // ATT_KV_HEADS
WINDOW = 128
BLOCK = 128
GLA_HEADS = 4
GLA_DK = 64
GLA_DV = 128
GLA_RANK = 16
GLA_TAU = 16.0
N_DIR = 2
ATT_WIDTH = ATT_HEADS * ATT_HEAD_DIM
ATT_KV_WIDTH = ATT_KV_HEADS * ATT_HEAD_DIM
GLA_QK_WIDTH = GLA_HEADS * GLA_DK
GLA_WIDTH = GLA_HEADS * GLA_DV
LN_EPS = 1e-5
DEPTH = 1
DN_ALPHA = (2.0 * DEPTH) ** 0.25

LANES = 128
GLA_BLK = 16
GLA_GROUP = 256
BLK_PER_GROUP = GLA_GROUP // GLA_BLK
SCAN_SEQS = 8
FF_CHUNK = 256
ROW_TILE = 512
ROW_SUB_TILES = 2
SUB_TILE_SKEW = 3
LN_PIECES = 4
PM_UNIT = ROW_TILE // ROW_SUB_TILES
PREP_TILE = 2048
VMEM_LIMIT = 56 * 1024 * 1024

_BF = jnp.bfloat16
_F32 = jnp.float32


def _dot(a, b):
    return jnp.dot(a, b, preferred_element_type=_F32)


def _dot_nt(a, b):
    return lax.dot_general(a, b, (((1,), (1,)), ((), ())), preferred_element_type=_F32)


def _layer_norm(x, g, b):
    mu = jnp.mean(x, axis=-1, keepdims=True)
    xc = x - mu
    var = jnp.mean(xc * xc, axis=-1, keepdims=True)
    return xc * lax.rsqrt(var + LN_EPS) * g + b


def _silu(x):
    return x * (1.0 / (1.0 + jnp.exp(-x)))


def _split3(x):
    hi = x.astype(_BF)
    r1 = x - hi.astype(_F32)
    mid = r1.astype(_BF)
    lo = (r1 - mid.astype(_F32)).astype(_BF)
    return hi, mid, lo


def _sub_tiles(rows):
    n = ROW_SUB_TILES if rows % (ROW_SUB_TILES * 16) == 0 else 1
    return [pl.ds(s * (rows // n), rows // n) for s in range(n)]


def _run_skewed(step_gens, skew):
    pending = list(enumerate(step_gens))
    tick = 0
    while pending:
        for k, gen in list(pending):
            if tick < k * skew:
                continue
            try:
                next(gen)
            except StopIteration:
                pending.remove((k, gen))
        tick += 1


def _swiglu_ln_steps(h, wg_ref, wu_ref, wd_ref, g_ref, b_ref, act_ref):
    hb = h.astype(_BF)
    n_ff = wg_ref.shape[1]
    assert n_ff % FF_CHUNK == 0
    for c in range(n_ff // FF_CHUNK):
        sl = slice(c * FF_CHUNK, (c + 1) * FF_CHUNK)
        gate = _dot(hb, wg_ref[:, sl])
        up = _dot(hb, wu_ref[:, sl])
        act_ref[:, sl] = (_silu(gate) * up).astype(_BF)
        yield
    y = _dot(act_ref[...], wd_ref[...])
    return (yield from _residual_ln_steps(h, 0.5 * y, g_ref, b_ref))


def _residual_ln_steps(h, branch, g_ref, b_ref):
    rows = h.shape[0]
    piece = rows // LN_PIECES if rows % (LN_PIECES * 8) == 0 else rows
    out = []
    for r in range(0, rows, piece):
        out.append(_layer_norm(DN_ALPHA * h[r:r + piece] + branch[r:r + piece], g_ref[...], b_ref[...]))
        if r + piece < rows:
            yield
    return out[0] if len(out) == 1 else jnp.concatenate(out, axis=0)


_P_AQ = (0, 512)
_P_AK = (512, 768)
_P_AV = (768, 896)
_P_GQ = (896, 1152)
_P_GK = (1152, 1408)
_P_GV = (1408, 1920)
_P_GG = (1920, 2432)
_P_GZ = (2432, 2560)
_P_WIDTH = 2560


def _position_major_perm(rows):
    nblk = rows // GLA_BLK
    perm = np.zeros((rows, rows), np.float32)
    for n in range(nblk):
        for t in range(GLA_BLK):
            perm[t * nblk + n, n * GLA_BLK + t] = 1.0
    return perm


def _stage_a_kernel(x_ref, g0_ref, b0_ref, wg_ref, wu_ref, wd_ref, g1_ref, b1_ref, win_ref, perm_ref,
                    h1_ref, aq_ref, ak_ref, avt_ref, gq_ref, gk_ref, gv_ref, gg_ref, gz_ref, act_ref):
    def steps(rows):
        h0 = _layer_norm(x_ref[rows, :], g0_ref[...], b0_ref[...])
        yield
        h1 = yield from _swiglu_ln_steps(h0, wg_ref, wu_ref, wd_ref, g1_ref, b1_ref, act_ref.at[rows])
        h1_ref[rows, :] = h1
        hb = h1.astype(_BF)
        hb_pm = _dot(perm_ref[...], hb).astype(_BF)
        yield

        def proj(act, cols):
            return _dot(act, win_ref[:, cols[0]:cols[1]])

        aq_ref[rows, :] = (proj(hb, _P_AQ) * (ATT_HEAD_DIM ** -0.5)).astype(_BF)
        ak_ref[rows, :] = proj(hb, _P_AK).astype(_BF)
        avt_ref[:, rows] = proj(hb, _P_AV).T.astype(_BF)
        gq_ref[rows, :] = proj(hb_pm, _P_GQ) * (GLA_DK ** -0.5)
        yield
        gk_ref[rows, :] = proj(hb_pm, _P_GK)
        gv_ref[rows, :] = proj(hb, _P_GV).astype(_BF)
        gg_ref[rows, :] = proj(hb, _P_GG)
        gz_ref[rows, :] = proj(hb_pm, _P_GZ)

    _run_skewed([steps(rows) for rows in _sub_tiles(x_ref.shape[0])], SUB_TILE_SKEW)


def _const_spec(shape):
    nd = len(shape)
    return pl.BlockSpec(shape, lambda *_: (0,) * nd, pipeline_mode=pl.Buffered(1))


def _stage_a(x2d, g0, b0, wg, wu, wd, g1, b1, win):
    m, d = x2d.shape
    tm = min(ROW_TILE, m)
    assert m % tm == 0
    n_ff = wg.shape[1]
    sub = _sub_tiles(tm)[0].size
    perm = jnp.asarray(_position_major_perm(sub), _BF)
    row = lambda w: pl.BlockSpec((tm, w), lambda i: (i, 0))
    out_widths = [(d, _F32), (512, _BF), (256, _BF), None, (256, _F32), (256, _F32), (512, _BF), (512, _F32),
                  (128, _F32)]
    out_specs = [row(o[0]) if o else pl.BlockSpec((ATT_KV_WIDTH, tm), lambda i: (0, i)) for o in out_widths]
    out_shape = [jax.ShapeDtypeStruct((m, o[0]), o[1]) if o else jax.ShapeDtypeStruct((ATT_KV_WIDTH, m), _BF)
                 for o in out_widths]
    return pl.pallas_call(
        _stage_a_kernel,
        grid=(m // tm,),
        in_specs=[row(d), _const_spec((1, d)), _const_spec((1, d)), _const_spec((d, n_ff)), _const_spec((d, n_ff)),
                  _const_spec((n_ff, d)), _const_spec((1, d)), _const_spec((1, d)), _const_spec((d, _P_WIDTH)),
                  _const_spec((sub, sub))],
        out_specs=out_specs,
        out_shape=out_shape,
        scratch_shapes=[pltpu.VMEM((tm, n_ff), _BF)],
        compiler_params=pltpu.CompilerParams(dimension_semantics=("arbitrary",), vmem_limit_bytes=VMEM_LIMIT),
        name="stage_a",
    )(x2d, g0, b0, wg, wu, wd, g1, b1, win, perm)


_KWIN = 3 * BLOCK


def _alibi_slopes():
    return [2.0 ** (-8.0 * (i + 1) / ATT_HEADS) for i in range(ATT_HEADS)]


_WIN_OFFSETS = (0, -BLOCK, -2 * BLOCK)
ATT_QBLOCKS = 16
META_ROWS = 32


def _attn_kernel(sink_ref, q_ref, k_ref, vt_ref, km_ref, vmt_ref, o_ref, band_scr, meta_scr):
    i = pl.program_id(1)
    s_len = k_ref.shape[0]
    nq = s_len // BLOCK
    pairs_per_group = ATT_REP // 2

    @pl.when((pl.program_id(0) == 0) & (i == 0))
    def _():
        col = lax.broadcasted_iota(jnp.int32, (_KWIN, 2 * BLOCK), 1)
        rel0 = lax.broadcasted_iota(jnp.int32, (_KWIN, 2 * BLOCK), 0) - jnp.where(col < BLOCK, col, col - BLOCK)
        mrow = lax.broadcasted_iota(jnp.int32, (META_ROWS, 2 * BLOCK), 0)
        mcol = lax.broadcasted_iota(jnp.int32, (META_ROWS, 2 * BLOCK), 1)
        slopes = _alibi_slopes()
        for g in range(ATT_KV_HEADS):
            for pl_ in range(pairs_per_group):
                h0 = g * ATT_REP + 2 * pl_
                slope = jnp.where(col < BLOCK, slopes[h0], slopes[h0 + 1])
                for var, off in enumerate(_WIN_OFFSETS):
                    absrel = jnp.abs(rel0 + off).astype(_F32)
                    band_scr[var, g, pl_] = jnp.where(absrel <= float(WINDOW), -slope * absrel, -jnp.inf)
                sink = jnp.where(mcol < BLOCK, sink_ref[h0], sink_ref[h0 + 1])
                meta_scr[g, pl_] = jnp.where(mrow < N_META, 0.0, jnp.where(mrow == N_META, sink, -jnp.inf))

    n_blk = q_ref.shape[0] // BLOCK
    starts, variants = [], []
    for blk in range(n_blk):
        ib = i * n_blk + blk
        starts.append(pl.multiple_of(jnp.clip((ib - 1) * BLOCK, 0, s_len - _KWIN), BLOCK))
        variants.append(jnp.where(ib == 0, 0, jnp.where(ib == nq - 1, 2, 1)))
    low_half = lax.broadcasted_iota(jnp.int32, (BLOCK, LANES), 1) < ATT_HEAD_DIM
    zero = jnp.zeros((BLOCK, LANES), _BF)

    chains = [(blk, g, pl_) for blk in range(n_blk) for g in range(ATT_KV_HEADS) for pl_ in range(pairs_per_group)]

    def scores(blk, g, pl_):
        lanes_g = slice(g * LANES, (g + 1) * LANES)
        pair = g * pairs_per_group + pl_
        qp = q_ref[blk * BLOCK:(blk + 1) * BLOCK, pair * LANES:(pair + 1) * LANES]
        qs = jnp.concatenate([jnp.where(low_half, qp, zero), jnp.where(low_half, zero, qp)], axis=0)
        kg = k_ref[pl.ds(starts[blk], _KWIN), lanes_g]
        s_band = _dot_nt(kg, qs) + band_scr[variants[blk], g, pl_]
        s_meta = _dot_nt(km_ref[:, lanes_g], qs) + meta_scr[g, pl_]
        return s_band, s_meta

    def rows_reduce(op, arrays):
        parts = [a[r:r + 8, :] for a in arrays for r in range(0, a.shape[0], 8)]
        while len(parts) > 1:
            parts = [op(parts[j], parts[j + 1]) if j + 1 < len(parts) else parts[j]
                     for j in range(0, len(parts), 2)]
        return parts[0]

    def col_max(s_band, s_meta):
        return s_band, s_meta, jnp.max(rows_reduce(jnp.maximum, [s_band, s_meta]), axis=0, keepdims=True)

    def softmax(s_band, s_meta, mx):
        return jnp.exp(s_band - mx).astype(_BF), jnp.exp(s_meta - mx).astype(_BF)

    ones_band = jnp.ones((16, _KWIN), _BF)
    ones_meta = jnp.where(lax.broadcasted_iota(jnp.int32, (16, META_ROWS), 1) <= N_META, 1.0, 0.0).astype(_BF)

    def values(blk, g, pl_, p_band, p_meta):
        drows = slice(g * ATT_HEAD_DIM, (g + 1) * ATT_HEAD_DIM)
        vt = jnp.concatenate([vt_ref[drows, pl.ds(starts[blk], _KWIN)], ones_band], axis=0)
        vmt = jnp.concatenate([vmt_ref[drows, :], ones_meta], axis=0)
        acc = _dot(vt, p_band) + _dot(vmt, p_meta)
        ot = acc[:ATT_HEAD_DIM, :] * (1.0 / acc[ATT_HEAD_DIM:ATT_HEAD_DIM + 1, :])
        o2 = jnp.concatenate([ot[:, :BLOCK], ot[:, BLOCK:]], axis=0)
        pair = g * pairs_per_group + pl_
        o_ref[blk * BLOCK:(blk + 1) * BLOCK, pair * LANES:(pair + 1) * LANES] = o2.T.astype(_BF)

    s_vals, m_vals, p_vals = {}, {}, {}
    for t in range(len(chains) + 3):
        if t < len(chains):
            s_vals[t] = scores(*chains[t])
        if 0 <= t - 1 < len(chains):
            m_vals[t - 1] = col_max(*s_vals.pop(t - 1))
        if 0 <= t - 2 < len(chains):
            p_vals[t - 2] = softmax(*m_vals.pop(t - 2))
        if 0 <= t - 3 < len(chains):
            values(*chains[t - 3], *p_vals.pop(t - 3))


def _attention(aq, ak2, avt, km2, vmt, sink, batch, seq):
    m = aq.shape[0]
    nblk = ATT_QBLOCKS if (seq // BLOCK) % ATT_QBLOCKS == 0 else 1
    nq = seq // (BLOCK * nblk)
    return pl.pallas_call(
        _attn_kernel,
        grid=(batch, nq),
        in_specs=[pl.BlockSpec(memory_space=pltpu.SMEM),
                  pl.BlockSpec((nblk * BLOCK, ATT_WIDTH), lambda b, i: (b * nq + i, 0)),
                  pl.BlockSpec((seq, 2 * LANES), lambda b, i: (b, 0)),
                  pl.BlockSpec((ATT_KV_WIDTH, seq), lambda b, i: (0, b)),
                  pl.BlockSpec((META_ROWS, 2 * LANES), lambda b, i: (0, 0)),
                  pl.BlockSpec((ATT_KV_WIDTH, META_ROWS), lambda b, i: (0, 0))],
        out_specs=pl.BlockSpec((nblk * BLOCK, ATT_WIDTH), lambda b, i: (b * nq + i, 0)),
        out_shape=jax.ShapeDtypeStruct((m, ATT_WIDTH), _BF),
        scratch_shapes=[pltpu.VMEM((len(_WIN_OFFSETS), ATT_KV_HEADS, ATT_REP // 2, _KWIN, 2 * BLOCK), _F32),
                        pltpu.VMEM((ATT_KV_HEADS, ATT_REP // 2, META_ROWS, 2 * BLOCK), _F32)],
        compiler_params=pltpu.CompilerParams(dimension_semantics=("arbitrary", "arbitrary"),
                                             vmem_limit_bytes=VMEM_LIMIT),
        name="attention",
    )(sink, aq, ak2, avt, km2, vmt)


_LOG2E = 1.4426950408889634


def _gla_log2_gates(z, w2_ref, bias_ref):
    zh = z.astype(_BF)
    zm = (z - zh.astype(_F32)).astype(_BF)
    lane = lax.broadcasted_iota(jnp.int32, z.shape, 1)
    mid_lanes = (lane >= N_DIR * GLA_RANK) & (lane < 2 * N_DIR * GLA_RANK)
    logits = _dot(jnp.where(mid_lanes, zm, zh), w2_ref[...]) + bias_ref[...]
    y = logits * _LOG2E
    log2_sig = jnp.minimum(y, 0.0) - jnp.log2(1.0 + jnp.exp2(-jnp.abs(y)))
    return log2_sig * (1.0 / GLA_TAU)


def _sel_matrix():
    sel = np.zeros((N_DIR, 2, GLA_BLK, LANES, LANES), np.float32)
    for d in range(N_DIR):
        for lg in range(2):
            for s in range(GLA_BLK):
                for hl in range(2):
                    col = d * 64 + (2 * lg + hl) * GLA_BLK + s
                    sel[d, lg, s, hl * GLA_DK:(hl + 1) * GLA_DK, col] = 1.0
    return sel.reshape(N_DIR, 2 * GLA_BLK * LANES, LANES)


_KEY_SIDE = (tuple(range(GLA_BLK // 2)), tuple(range(GLA_BLK // 2, GLA_BLK)))
_MID = (GLA_BLK // 2 - 1, GLA_BLK // 2)


def _gla_prep_kernel(q0_ref, q1_ref, k0_ref, k1_ref, z_ref, w2_ref, bias_ref, sel_ref, perm_ref,
                     qk_ref, dec_ref, r_ref,
                     g_scr, b_scr, kbf_scr, kg_scr, qk_scr, dec_scr):
    tt_rows = z_ref.shape[0]
    units = tt_rows // PM_UNIT
    ub = PM_UNIT // GLA_BLK
    rc = min(512, tt_rows)
    for c in range(tt_rows // rc):
        rows = pl.ds(c * rc, rc)
        g = _gla_log2_gates(z_ref[rows, :], w2_ref, bias_ref)
        for gi in range(4):
            g_scr[gi, rows, :] = g[:, gi * LANES:(gi + 1) * LANES]

    def slab(ref, tt, lead=()):
        return jnp.concatenate([ref[lead + (pl.ds(u * PM_UNIT + tt * ub, ub), slice(None))] for u in range(units)],
                               axis=0)

    def store_slab(ref, lead, tt, val):
        for u in range(units):
            ref[lead + (pl.ds(u * PM_UNIT + tt * ub, ub), slice(None))] = val[u * ub:(u + 1) * ub, :]

    qk_refs = ((q0_ref, k0_ref), (q1_ref, k1_ref))
    last = {}
    for lg, (_, k_ref) in enumerate(qk_refs):
        for tt in range(GLA_BLK):
            kbf_scr[lg, tt] = slab(k_ref, tt).astype(_BF)
        for d in range(N_DIR):
            gi = d * 2 + lg
            acc = None
            for tt in (range(GLA_BLK) if d == 0 else reversed(range(GLA_BLK))):
                gs = slab(g_scr, tt, (gi,))
                acc = gs if acc is None else acc + gs
                b_scr[gi, tt] = acc
            last[gi] = acc
            dec_scr[gi] = jnp.exp2(acc)
            mid = _MID[d]
            for j, s in enumerate(_KEY_SIDE[d]):
                kg_scr[gi, j] = (slab(k_ref, s) * jnp.exp2(b_scr[gi, mid] - b_scr[gi, s])).astype(_BF)
    for tt in range(GLA_BLK):
        prods, sel_rows = [], []
        for d in range(N_DIR):
            positions = list(range(0, tt + 1)) if d == 0 else list(range(tt + 1, GLA_BLK))
            for lg, (q_ref, k_ref) in enumerate(qk_refs):
                gi = d * 2 + lg
                q = slab(q_ref, tt)
                b = b_scr[gi, tt]
                store_slab(qk_scr, (d * 4 + lg,), tt, q * jnp.exp2(b))
                store_slab(qk_scr, (d * 4 + 2 + lg,), tt, slab(k_ref, tt) * jnp.exp2(last[gi] - b))
                if not positions:
                    continue
                qb = q.astype(_BF)
                crosses = tt not in _KEY_SIDE[d]
                if crosses:
                    qf = (q * jnp.exp2(b - b_scr[gi, _MID[d]])).astype(_BF)
                for s in positions:
                    if crosses and s in _KEY_SIDE[d]:
                        p = qf * kg_scr[gi, _KEY_SIDE[d].index(s)]
                    else:
                        p = qb * kbf_scr[lg, s]
                        if s != tt:
                            p = p * jnp.exp2(b - b_scr[gi, s]).astype(_BF)
                    prods.append(p)
                r0 = lg * GLA_BLK * LANES + positions[0] * LANES
                sel_rows.append(sel_ref[d, r0:r0 + len(positions) * LANES, :])
        a_tt = _dot(jnp.concatenate(prods, axis=1), jnp.concatenate(sel_rows, axis=0))
        store_slab(qk_scr, (8,), tt, a_tt)
    for u in range(units):
        rows = pl.ds(u * PM_UNIT, PM_UNIT)
        x = jnp.concatenate([qk_scr[c, rows, :] for c in range(9)], axis=1).astype(_BF)
        y = _dot(perm_ref[...], x).astype(_BF)
        qk_ref[rows, :] = y[:, :8 * LANES]
        r_ref[rows, :] = y[:, 8 * LANES:]
    for gi in range(4):
        dec_ref[:, gi * LANES:(gi + 1) * LANES] = dec_scr[gi]


def _gla_prep(gq, gk, gz, w2s, bias, sel):
    m = gq.shape[0]
    tt = min(PREP_TILE, m)
    assert m % tt == 0 and tt % PM_UNIT == 0
    nb = tt // GLA_BLK
    perm = jnp.asarray(_position_major_perm(PM_UNIT).T, _BF)
    col = lambda j: pl.BlockSpec((tt, LANES), lambda i: (i, j))
    return pl.pallas_call(
        _gla_prep_kernel,
        grid=(m // tt,),
        in_specs=[col(0), col(1), col(0), col(1), col(0),
                  _const_spec(w2s.shape), _const_spec(bias.shape), _const_spec(sel.shape),
                  _const_spec((PM_UNIT, PM_UNIT))],
        out_specs=[pl.BlockSpec((tt, 8 * LANES), lambda i: (i, 0)),
                   pl.BlockSpec((nb, 4 * LANES), lambda i: (i, 0)),
                   pl.BlockSpec((tt, LANES), lambda i: (i, 0))],
        out_shape=[jax.ShapeDtypeStruct((m, 8 * LANES), _BF),
                   jax.ShapeDtypeStruct((m // GLA_BLK, 4 * LANES), _F32),
                   jax.ShapeDtypeStruct((m, LANES), _BF)],
        scratch_shapes=[pltpu.VMEM((4, tt, LANES), _F32),
                        pltpu.VMEM((4, GLA_BLK, nb, LANES), _F32),
                        pltpu.VMEM((2, GLA_BLK, nb, LANES), _BF),
                        pltpu.VMEM((4, GLA_BLK // 2, nb, LANES), _BF),
                        pltpu.VMEM((9, tt, LANES), _F32),
                        pltpu.VMEM((4, nb, LANES), _F32)],
        compiler_params=pltpu.CompilerParams(dimension_semantics=("arbitrary",), vmem_limit_bytes=VMEM_LIMIT),
        name="gla_prep",
    )(gq, gq, gk, gk, gz, w2s, bias, sel, perm)


def _gla_meta_state_kernel(k_ref, z_ref, v_ref, w2_ref, bias_ref, s0_ref):
    g = _gla_log2_gates(z_ref[...], w2_ref, bias_ref)[:, :GLA_QK_WIDTH]
    r = lax.broadcasted_iota(jnp.int32, (LANES, LANES), 0)
    c = lax.broadcasted_iota(jnp.int32, (LANES, LANES), 1)
    tail = jnp.where((c > r) & (c < N_META), 1.0, 0.0).astype(_BF)
    gh, gm, gl = _split3(g)
    rest = _dot(tail, gh) + _dot(tail, gm) + _dot(tail, gl)
    is_meta = lax.broadcasted_iota(jnp.int32, (LANES, GLA_QK_WIDTH), 0) < N_META
    ke = jnp.where(is_meta, k_ref[...] * jnp.exp2(jnp.where(is_meta, rest, 0.0)), 0.0).astype(_BF)
    lane = lax.broadcasted_iota(jnp.int32, (GLA_DV, GLA_QK_WIDTH), 1)
    v = v_ref[...].astype(_F32)
    s0 = jnp.zeros((GLA_DV, GLA_QK_WIDTH), _F32)
    for h in range(GLA_HEADS):
        vt = v[:, h * GLA_DV:(h + 1) * GLA_DV].T.astype(_BF)
        in_head = (lane >= h * GLA_DK) & (lane < (h + 1) * GLA_DK)
        s0 = s0 + jnp.where(in_head, _dot(vt, ke), 0.0)
    s0_ref[...] = s0


def _gla_meta_state(gk_m, gz_m, gv_m, w2s, bias):
    return pl.pallas_call(
        _gla_meta_state_kernel,
        out_shape=jax.ShapeDtypeStruct((GLA_DV, GLA_QK_WIDTH), _F32),
        name="gla_meta_state",
    )(gk_m, gz_m, gv_m, w2s, bias)


def _gla_scan_kernel(qkf_ref, qkb_ref, rf_ref, rb_ref, df_ref, db_ref, vf_ref, vb_ref, s0_ref,
                     of_ref, ob_ref, sf_scr, sb_scr):
    n_seq = sf_scr.shape[0]

    @pl.when(pl.program_id(1) == 0)
    def _():
        for b in range(n_seq):
            sf_scr[b] = s0_ref[...]
        sb_scr[...] = jnp.zeros_like(sb_scr)

    def lane_band(shape, width, n):
        lane = lax.broadcasted_iota(jnp.int32, shape, 1)
        return [(lane >= c * width) & (lane < (c + 1) * width) for c in range(n)]

    head_lanes = lane_band((GLA_BLK, GLA_QK_WIDTH), GLA_DK, GLA_HEADS)
    a_lanes = lane_band((GLA_BLK, LANES), GLA_BLK, N_DIR * GLA_HEADS)

    def head_stack(x, bands):
        return jnp.concatenate([jnp.where(bands[h], x, jnp.zeros_like(x)) for h in range(GLA_HEADS)], axis=0)

    dirs = ((qkf_ref, rf_ref, df_ref, vf_ref, of_ref, sf_scr),
            (qkb_ref, rb_ref, db_ref, vb_ref, ob_ref, sb_scr))
    for step in range(BLK_PER_GROUP):
        for b in range(n_seq):
            for d, (qk_ref, r_ref, d_ref, v_ref, o_ref, s_scr) in enumerate(dirs):
                j = step if d == 0 else BLK_PER_GROUP - 1 - step
                rows = slice(j * GLA_BLK, (j + 1) * GLA_BLK)
                state_t = s_scr[b]
                lhs1 = head_stack(qk_ref[b, rows, :GLA_QK_WIDTH], head_lanes)
                lhs2 = head_stack(r_ref[b, rows, :], a_lanes[d * GLA_HEADS:(d + 1) * GLA_HEADS])
                vblk = v_ref[b, rows, :]
                vstack = jnp.concatenate([vblk[:, h * GLA_DV:(h + 1) * GLA_DV] for h in range(GLA_HEADS)], axis=0)
                vv = jnp.concatenate([vstack, vstack], axis=0)
                o = _dot_nt(lhs1, state_t.astype(_BF)) + _dot(lhs2, vv)
                for h in range(GLA_HEADS):
                    o_ref[h, b, rows, :] = o[h * GLA_BLK:(h + 1) * GLA_BLK, :]
                kbd = head_stack(qk_ref[b, rows, GLA_QK_WIDTH:], head_lanes)
                vst = vstack.astype(_F32).T.astype(_BF)
                s_scr[b] = state_t * d_ref[b, j:j + 1, :] + _dot(vst, kbd)


def _gla_scan(qk, r, dec, gv, s0, batch, seq):
    ng = seq // GLA_GROUP
    nbs = SCAN_SEQS if batch % SCAN_SEQS == 0 else 1
    qk = qk.reshape(batch, seq, N_DIR * 2 * GLA_QK_WIDTH)
    r = r.reshape(batch, seq, LANES)
    dec = dec.reshape(batch, seq // GLA_BLK, N_DIR * GLA_QK_WIDTH)
    gv = gv.reshape(batch, seq, GLA_WIDTH)
    fwd = lambda i: i
    bwd = lambda i: ng - 1 - i
    o_shape = jax.ShapeDtypeStruct((GLA_HEADS, batch, seq, GLA_DV), _F32)

    def spec(rows, width, grp, col):
        return pl.BlockSpec((nbs, rows, width), lambda b, i: (b, grp(i), col))

    o_f, o_b = pl.pallas_call(
        _gla_scan_kernel,
        grid=(batch // nbs, ng),
        in_specs=[spec(GLA_GROUP, 2 * GLA_QK_WIDTH, fwd, 0), spec(GLA_GROUP, 2 * GLA_QK_WIDTH, bwd, 1),
                  spec(GLA_GROUP, LANES, fwd, 0), spec(GLA_GROUP, LANES, bwd, 0),
                  spec(BLK_PER_GROUP, GLA_QK_WIDTH, fwd, 0), spec(BLK_PER_GROUP, GLA_QK_WIDTH, bwd, 1),
                  spec(GLA_GROUP, GLA_WIDTH, fwd, 0), spec(GLA_GROUP, GLA_WIDTH, bwd, 0),
                  pl.BlockSpec((GLA_DV, GLA_QK_WIDTH), lambda b, i: (0, 0))],
        out_specs=[pl.BlockSpec((GLA_HEADS, nbs, GLA_GROUP, GLA_DV), lambda b, i: (0, b, fwd(i), 0)),
                   pl.BlockSpec((GLA_HEADS, nbs, GLA_GROUP, GLA_DV), lambda b, i: (0, b, bwd(i), 0))],
        out_shape=[o_shape, o_shape],
        scratch_shapes=[pltpu.VMEM((nbs, GLA_DV, GLA_QK_WIDTH), _F32), pltpu.VMEM((nbs, GLA_DV, GLA_QK_WIDTH), _F32)],
        compiler_params=pltpu.CompilerParams(dimension_semantics=("arbitrary", "arbitrary"),
                                             vmem_limit_bytes=VMEM_LIMIT),
        name="gla_scan",
    )(qk, qk, r, r, dec, dec, gv, gv, s0)
    return (o_f.reshape(GLA_HEADS, batch * seq, GLA_DV), o_b.reshape(GLA_HEADS, batch * seq, GLA_DV))


def _stage_c_kernel(h1_ref, oatt_ref, of_ref, ob_ref, gg_ref, ng_ref, wout_ref, gm_ref, bm_ref,
                    wg_ref, wu_ref, wd_ref, g2_ref, b2_ref, out_ref, act_ref):
    def steps(rows):
        y_att = _dot(oatt_ref[rows, :], wout_ref[:ATT_WIDTH, :])
        pieces = []
        for h in range(GLA_HEADS):
            o = of_ref[h, rows, :] + ob_ref[h, rows, :]
            o = o * lax.rsqrt(jnp.mean(o * o, axis=-1, keepdims=True) + LN_EPS) * ng_ref[...]
            pieces.append((o * _silu(gg_ref[rows, h * GLA_DV:(h + 1) * GLA_DV])).astype(_BF))
        gla = jnp.concatenate(pieces, axis=1)
        yield
        y = y_att + _dot(gla, wout_ref[ATT_WIDTH:, :])
        h2 = yield from _residual_ln_steps(h1_ref[rows, :], y, gm_ref, bm_ref)
        yield
        out_ref[rows, :] = yield from _swiglu_ln_steps(h2, wg_ref, wu_ref, wd_ref, g2_ref, b2_ref,
                                                       act_ref.at[rows])

    _run_skewed([steps(rows) for rows in _sub_tiles(h1_ref.shape[0])], SUB_TILE_SKEW)


def _stage_c(h1, oatt, o_f, o_b, gg, ng, wout, gm, bm, wg, wu, wd, g2, b2):
    m, d = h1.shape
    tm = min(ROW_TILE, m)
    assert m % tm == 0
    n_ff = wg.shape[1]
    row = lambda w: pl.BlockSpec((tm, w), lambda i: (i, 0))
    hrow = pl.BlockSpec((GLA_HEADS, tm, GLA_DV), lambda i: (0, i, 0))
    return pl.pallas_call(
        _stage_c_kernel,
        grid=(m // tm,),
        in_specs=[row(d), row(ATT_WIDTH), hrow, hrow, row(GLA_WIDTH), _const_spec((1, GLA_DV)),
                  _const_spec(wout.shape), _const_spec((1, d)), _const_spec((1, d)),
                  _const_spec((d, n_ff)), _const_spec((d, n_ff)), _const_spec((n_ff, d)),
                  _const_spec((1, d)), _const_spec((1, d))],
        out_specs=row(d),
        out_shape=jax.ShapeDtypeStruct((m, d), _F32),
        scratch_shapes=[pltpu.VMEM((tm, n_ff), _BF)],
        compiler_params=pltpu.CompilerParams(dimension_semantics=("arbitrary",), vmem_limit_bytes=VMEM_LIMIT),
        name="stage_c",
    )(h1, oatt, o_f, o_b, gg, ng, wout, gm, bm, wg, wu, wd, g2, b2)


def _projection_weight(w_in):
    aq = w_in[:, 0:512]
    ak = w_in[:, 512:640]
    rest = w_in[:, 640:2304]
    gz = w_in[:, 2304:2336]
    ak_dup = jnp.concatenate([ak[:, 0:64], ak[:, 0:64], ak[:, 64:128], ak[:, 64:128]], axis=1)
    gz3 = jnp.pad(jnp.concatenate([gz, gz, gz], axis=1), ((0, 0), (0, LANES - 3 * gz.shape[1])))
    return jnp.concatenate([aq, ak_dup, rest, gz3], axis=1).astype(_BF)


def _gate_weight(w2, bias):
    k = N_DIR * GLA_RANK
    wcat = jnp.zeros((k, N_DIR * GLA_QK_WIDTH), _F32)
    for n in range(N_DIR):
        wcat = wcat.at[n * GLA_RANK:(n + 1) * GLA_RANK, n * GLA_QK_WIDTH:(n + 1) * GLA_QK_WIDTH].set(w2[n])
    hi = wcat.astype(_BF)
    mid = (wcat - hi.astype(_F32)).astype(_BF)
    pad = jnp.zeros((LANES - 3 * k, N_DIR * GLA_QK_WIDTH), _BF)
    return jnp.concatenate([hi, hi, mid, pad], axis=0), bias.reshape(1, N_DIR * GLA_QK_WIDTH)


def _pad_rows(a, rows):
    return jnp.pad(a, ((0, rows - a.shape[0]), (0, 0)))


def kernel(x, meta_tokens, ln_in_g, ln_in_b, ffn1_ln_g, ffn1_ln_b, ffn1_w_gate, ffn1_w_up, ffn1_w_down, w_in,
           attn_sink, gla_gate_w2, gla_gate_b, gla_norm_g, w_out, mix_ln_g, mix_ln_b, ffn2_w_gate, ffn2_w_up,
           ffn2_w_down, ffn2_ln_g, ffn2_ln_b):
    batch, seq, d = x.shape
    assert seq % BLOCK == 0 and seq >= _KWIN and seq % GLA_GROUP == 0
    vec = lambda a: a.reshape(1, -1)
    x2d = x.reshape(batch * seq, d)
    wg1, wu1, wd1 = ffn1_w_gate[0].astype(_BF), ffn1_w_up[0].astype(_BF), ffn1_w_down[0].astype(_BF)
    wg2, wu2, wd2 = ffn2_w_gate[0].astype(_BF), ffn2_w_up[0].astype(_BF), ffn2_w_down[0].astype(_BF)
    win = _projection_weight(w_in[0])
    w2s, gbias = _gate_weight(gla_gate_w2[0], gla_gate_b[0])
    sel = jnp.asarray(_sel_matrix(), _BF)

    a_args = (vec(ln_in_g), vec(ln_in_b), wg1, wu1, wd1, vec(ffn1_ln_g[0]), vec(ffn1_ln_b[0]), win)
    h1, aq, ak2, avt, gq, gk, gv, gg, gz = _stage_a(x2d, *a_args)
    _, _, ak2_m, avt_m, _, gk_m, gv_m, _, gz_m = _stage_a(meta_tokens.astype(x.dtype), *a_args)

    vmt = jnp.pad(avt_m, ((0, 0), (0, META_ROWS - avt_m.shape[1])))
    o_att = _attention(aq, ak2, avt, _pad_rows(ak2_m, META_ROWS), vmt, attn_sink[0].astype(_F32), batch, seq)

    qk, dec, r = _gla_prep(gq, gk, gz, w2s, gbias, sel)
    s0 = _gla_meta_state(_pad_rows(gk_m, LANES), _pad_rows(gz_m, LANES), _pad_rows(gv_m, LANES), w2s, gbias)
    o_f, o_b = _gla_scan(qk, r, dec, gv, s0, batch, seq)

    out = _stage_c(h1, o_att, o_f, o_b, gg, vec(gla_norm_g[0]), w_out[0].astype(_BF),
                   vec(mix_ln_g[0]), vec(mix_ln_b[0]), wg2, wu2, wd2, vec(ffn2_ln_g[0]), vec(ffn2_ln_b[0]))
    return out.reshape(batch, seq, d)
```
